```python
import math
import jax
import jax.numpy as jnp
from jax import lax
import numpy as np

D_MODEL = 4096
BATCH = 4
SEQ = 2048
DEPTH = 1

CHUNK = 64
LEFT_CHUNKS = 8
BAND = (LEFT_CHUNKS + 1) * CHUNK
MEM_LEN = 256

A_HEADS = 16
A_HEAD_DIM = 128
MAX_REL = 128
B_HEADS = 4
B_HEAD_DIM = 128
C_HEADS = 4
C_HEAD_DIM = 256
A_WIDTH = A_HEADS * A_HEAD_DIM
B_QK_WIDTH = B_HEADS * 2 * B_HEAD_DIM
B_V_WIDTH = B_HEADS * 2 * B_HEAD_DIM
C_WIDTH = C_HEADS * C_HEAD_DIM
IN_WIDTH = 3 * A_WIDTH + 2 * B_QK_WIDTH + B_V_WIDTH + C_WIDTH
N_BRANCHES = 3
ROPE_THETA = 10000.0
Q_BLOCK = 128

N_EXPERTS = 32
TOP_K = 4
D_EXPERT = 1536
SWIGLU_LIMIT = 7.0
SWIGLU_ALPHA = 1.702
EXPERT_BLOCK = 256

LN_EPS = 1e-5
RMS_EPS = 1e-5
DEEPNORM_ALPHA = (2 * DEPTH) ** 0.25
DEEPNORM_BETA = (8 * DEPTH) ** -0.25

kernel_name = 'hybrid_chunk_stream_block'


def layer_norm(x, g, b):
    xf = x.astype(jnp.float32)
    mu = jnp.mean(xf, -1, keepdims=True)
    var = jnp.mean(jnp.square(xf - mu), -1, keepdims=True)
    return ((xf - mu) * lax.rsqrt(var + LN_EPS)).astype(x.dtype) * g + b


def rms_norm(x, g):
    xf = x.astype(jnp.float32)
    y = xf * lax.rsqrt(jnp.mean(xf * xf, -1, keepdims=True) + RMS_EPS)
    return y.astype(x.dtype) * g


def rope_tables(seq, dim, dtype):
    inv = 1.0 / (ROPE_THETA ** (jnp.arange(0, dim, 2, dtype=jnp.float32) / dim))
    ang = jnp.arange(seq, dtype=jnp.float32)[:, None] * inv[None, :]
    ang = jnp.concatenate([ang, ang], -1)
    return jnp.cos(ang).astype(dtype), jnp.sin(ang).astype(dtype)


def apply_rope(x, cos, sin):
    x1, x2 = jnp.split(x, 2, axis=-1)
    rot = jnp.concatenate([-x2, x1], -1)
    shape = (1, cos.shape[0]) + (1,) * (x.ndim - 3) + (cos.shape[1],)
    return x * cos.reshape(shape) + rot * sin.reshape(shape)


def chunk_relbias_attention(q, k, v, rel_bias):
    b, s, h, dh = q.shape
    nc = s // CHUNK
    pad = LEFT_CHUNKS * CHUNK
    kp = jnp.pad(k, ((0, 0), (pad, 0), (0, 0), (0, 0)))
    vp = jnp.pad(v, ((0, 0), (pad, 0), (0, 0), (0, 0)))
    qc = jnp.moveaxis(q.reshape(b, nc, CHUNK, h, dh), 1, 0)
    rel = jnp.arange(CHUNK)[:, None] - jnp.arange(BAND)[None, :] + pad
    bias = rel_bias[:, jnp.clip(rel, -MAX_REL, MAX_REL) + MAX_REL].astype(jnp.float32)
    scale = dh ** -0.5

    def one_chunk(args):
        c, qb = args
        start = c * CHUNK
        kb = lax.dynamic_slice_in_dim(kp, start, BAND, axis=1)
        vb = lax.dynamic_slice_in_dim(vp, start, BAND, axis=1)
        valid = (start - pad + jnp.arange(BAND)) >= 0
        sc = jnp.einsum('bqhd,bkhd->bhqk', qb, kb).astype(jnp.float32) * scale + bias
        sc = jnp.where(valid[None, None, None, :], sc, -jnp.inf)
        p = jax.nn.softmax(sc, axis=-1).astype(v.dtype)
        return jnp.einsum('bhqk,bkhd->bqhd', p, vb)

    out = lax.map(one_chunk, (jnp.arange(nc), qc))
    return jnp.moveaxis(out, 0, 1).reshape(b, s, h * dh)


def differential_attention(q, k, v, lam, norm_g, lambda_init):
    b, s, h, _, dh = q.shape
    nqb = s // Q_BLOCK
    qbs = jnp.moveaxis(q.reshape(b, nqb, Q_BLOCK, h, 2, dh), 1, 0)
    k_chunk = jnp.arange(s) // CHUNK
    scale = dh ** -0.5

    def one_block(args):
        n, qb = args
        q_chunk = (n * Q_BLOCK + jnp.arange(Q_BLOCK)) // CHUNK
        allowed = k_chunk[None, :] <= q_chunk[:, None]
        sc = jnp.einsum('bqhmd,bkhmd->bhmqk', qb, k).astype(jnp.float32) * scale
        sc = jnp.where(allowed, sc, -jnp.inf)
        p = jax.nn.softmax(sc, axis=-1)
        w = (p[:, :, 0] - lam * p[:, :, 1]).astype(v.dtype)
        return jnp.einsum('bhqk,bkhe->bqhe', w, v)

    o = lax.map(one_block, (jnp.arange(nqb), qbs))
    o = jnp.moveaxis(o, 0, 1).reshape(b, s, h, 2 * dh)
    o = rms_norm(o, norm_g) * (1.0 - lambda_init)
    return o.reshape(b, s, h * 2 * dh)


def memory_attention(q, k, v):
    b, s, h, dc = q.shape
    sc = jnp.einsum('bqhd,bkhd->bhqk', q, k).astype(jnp.float32) * dc ** -0.5
    p = jax.nn.softmax(sc, axis=-1).astype(v.dtype)
    return jnp.einsum('bhqk,bkhd->bqhd', p, v).reshape(b, s, h * dc)


def hybrid_mixer(x, mem, w_in, w_mem_kv, rel_bias, lambda_q1, lambda_k1, lambda_q2, lambda_k2,
                 diff_norm_g, w_branch_a, w_branch_b, w_branch_c, w_gates, b_gates, w_o,
                 lambda_init, cos, sin):
    b, s, d = x.shape
    proj = x @ w_in
    cuts = np.cumsum([A_WIDTH, A_WIDTH, A_WIDTH, B_QK_WIDTH, B_QK_WIDTH, B_V_WIDTH]).tolist()
    a_q, a_k, a_v, b_q, b_k, b_v, c_q = jnp.split(proj, cuts, axis=-1)

    shp_a = (b, s, A_HEADS, A_HEAD_DIM)
    y_a = chunk_relbias_attention(a_q.reshape(shp_a), a_k.reshape(shp_a), a_v.reshape(shp_a), rel_bias)

    shp_b = (b, s, B_HEADS, 2, B_HEAD_DIM)
    bq = apply_rope(b_q.reshape(shp_b), cos, sin)
    bk = apply_rope(b_k.reshape(shp_b), cos, sin)
    f32 = jnp.float32
    lam = (jnp.exp(jnp.sum(lambda_q1.astype(f32) * lambda_k1.astype(f32)))
           - jnp.exp(jnp.sum(lambda_q2.astype(f32) * lambda_k2.astype(f32))) + lambda_init)
    y_b = differential_attention(bq, bk, b_v.reshape(b, s, B_HEADS, 2 * B_HEAD_DIM), lam,
                                 diff_norm_g, lambda_init)

    m = mem.shape[1]
    c_k, c_v = jnp.split(mem @ w_mem_kv, 2, axis=-1)
    y_c = memory_attention(c_q.reshape(b, s, C_HEADS, C_HEAD_DIM),
                           c_k.reshape(b, m, C_HEADS, C_HEAD_DIM),
                           c_v.reshape(b, m, C_HEADS, C_HEAD_DIM))

    gates = jax.nn.sigmoid(x @ w_gates + b_gates).reshape(b, s, N_BRANCHES, d)
    merged = (gates[:, :, 0] * (y_a @ w_branch_a)
              + gates[:, :, 1] * (y_b @ w_branch_b)
              + gates[:, :, 2] * (y_c @ w_branch_c))
    return merged @ w_o


def moe_ffn(x, w_router, b_router, w_mlp1, b_mlp1, w_mlp2, b_mlp2):
    b, s, d = x.shape
    xt = x.reshape(-1, d)
    n = xt.shape[0]
    logits = (xt @ w_router + b_router).astype(jnp.float32)
    top_vals, top_idx = lax.top_k(logits, TOP_K)
    gates = jax.nn.softmax(top_vals, axis=-1).astype(x.dtype)

    flat_e = top_idx.reshape(-1)
    p = flat_e.shape[0]
    order = jnp.argsort(flat_e)
    sorted_e = flat_e[order]
    counts = jnp.bincount(flat_e, length=N_EXPERTS)
    padded = (counts + EXPERT_BLOCK - 1) // EXPERT_BLOCK * EXPERT_BLOCK
    start = jnp.cumsum(counts) - counts
    pend = jnp.cumsum(padded)
    pstart = pend - padded
    dest = pstart[sorted_e] + jnp.arange(p) - start[sorted_e]
    n_blocks = -(-p // EXPERT_BLOCK) + N_EXPERTS
    cap = n_blocks * EXPERT_BLOCK
    buf_tok = jnp.full((cap,), n, jnp.int32).at[dest].set((order // TOP_K).astype(jnp.int32))
    buf_gate = jnp.zeros((cap,), x.dtype).at[dest].set(gates.reshape(-1)[order])
    block_e = jnp.minimum(jnp.searchsorted(pend, jnp.arange(n_blocks) * EXPERT_BLOCK, side='right'),
                          N_EXPERTS - 1)
    x_pad = jnp.concatenate([xt, jnp.zeros((1, d), xt.dtype)], axis=0)

    def expert_block(args):
        tok, e = args
        xb = x_pad[tok]
        h = xb @ w_mlp1[e] + b_mlp1[e]
        x_glu = jnp.minimum(h[:, ::2], SWIGLU_LIMIT)
        x_lin = jnp.clip(h[:, 1::2], -SWIGLU_LIMIT, SWIGLU_LIMIT)
        act = x_glu * jax.nn.sigmoid(SWIGLU_ALPHA * x_glu) * (x_lin + 1.0)
        return act @ w_mlp2[e] + b_mlp2[e]

    y = lax.map(expert_block, (buf_tok.reshape(n_blocks, EXPERT_BLOCK), block_e))
    y = y.reshape(cap, d) * buf_gate[:, None]
    out = jnp.zeros((n + 1, d), x.dtype).at[buf_tok].add(y)[:n]
    return out.reshape(b, s, d)


def setup_inputs(seed: int = 0) -> dict:
    key = jax.random.key(seed)
    ks = jax.random.split(key, 32)
    f32 = jnp.float32
    L, D, E, F = DEPTH, D_MODEL, N_EXPERTS, D_EXPERT

    def nrm(k, shape, scale):
        return jax.random.normal(k, shape, f32) * scale

    return {
        'x': nrm(ks[0], (BATCH, SEQ, D), 1.0),
        'mem': nrm(ks[1], (BATCH, MEM_LEN, D), 1.0),
        'w_in': nrm(ks[2], (L, D, IN_WIDTH), D ** -0.5),
        'w_mem_kv': nrm(ks[3], (L, D, 2 * C_WIDTH), D ** -0.5),
        'rel_bias': nrm(ks[4], (L, A_HEADS, 2 * MAX_REL + 1), 0.5),
        'lambda_q1': nrm(ks[5], (L, B_HEAD_DIM), 0.1),
        'lambda_k1': nrm(ks[6], (L, B_HEAD_DIM), 0.1),
        'lambda_q2': nrm(ks[7], (L, B_HEAD_DIM), 0.1),
        'lambda_k2': nrm(ks[8], (L, B_HEAD_DIM), 0.1),
        'diff_norm_g': 1.0 + nrm(ks[9], (L, 2 * B_HEAD_DIM), 0.02),
        'w_branch_a': nrm(ks[10], (L, A_WIDTH, D), A_WIDTH ** -0.5 * DEEPNORM_BETA),
        'w_branch_b': nrm(ks[11], (L, B_V_WIDTH, D), B_V_WIDTH ** -0.5 * DEEPNORM_BETA),
        'w_branch_c': nrm(ks[12], (L, C_WIDTH, D), C_WIDTH ** -0.5 * DEEPNORM_BETA),
        'w_gates': nrm(ks[13], (L, D, N_BRANCHES * D), D ** -0.5),
        'b_gates': nrm(ks[14], (L, N_BRANCHES * D), 0.02),
        'w_o': nrm(ks[15], (L, D, D), D ** -0.5 * DEEPNORM_BETA),
        'ln1_g': 1.0 + nrm(ks[16], (L, D), 0.02),
        'ln1_b': nrm(ks[17], (L, D), 0.02),
        'w_router': nrm(ks[18], (L, D, E), D ** -0.5),
        'b_router': nrm(ks[19], (L, E), 0.01),
        'w_mlp1': nrm(ks[20], (L, E, D, 2 * F), D ** -0.5),
        'b_mlp1': nrm(ks[21], (L, E, 2 * F), 0.02),
        'w_mlp2': nrm(ks[22], (L, E, F, D), F ** -0.5 * DEEPNORM_BETA),
        'b_mlp2': nrm(ks[23], (L, E, D), 0.02),
        'ln2_g': 1.0 + nrm(ks[24], (L, D), 0.02),
        'ln2_b': nrm(ks[25], (L, D), 0.02),
    }


def reference(x, mem, w_in, w_mem_kv, rel_bias, lambda_q1, lambda_k1, lambda_q2, lambda_k2,
              diff_norm_g, w_branch_a, w_branch_b, w_branch_c, w_gates, b_gates, w_o,
              ln1_g, ln1_b, w_router, b_router, w_mlp1, b_mlp1, w_mlp2, b_mlp2, ln2_g, ln2_b):
    cos, sin = rope_tables(x.shape[1], B_HEAD_DIM, x.dtype)
    h = x
    for l in range(DEPTH):
        lambda_init = 0.8 - 0.6 * math.exp(-0.3 * l)
        mix = hybrid_mixer(h, mem, w_in[l], w_mem_kv[l], rel_bias[l], lambda_q1[l], lambda_k1[l],
                           lambda_q2[l], lambda_k2[l], diff_norm_g[l], w_branch_a[l], w_branch_b[l],
                           w_branch_c[l], w_gates[l], b_gates[l], w_o[l], lambda_init, cos, sin)
        h = layer_norm(DEEPNORM_ALPHA * h + mix, ln1_g[l], ln1_b[l])
        ffn = moe_ffn(h, w_router[l], b_router[l], w_mlp1[l], b_mlp1[l], w_mlp2[l], b_mlp2[l])
        h = layer_norm(DEEPNORM_ALPHA * h + ffn, ln2_g[l], ln2_b[l])
    return h
```

```python
import functools
import math

import jax
import jax.numpy as jnp
from jax import lax
from jax.experimental import pallas as pl
from jax.experimental.pallas import tpu as pltpu

F32 = jnp.float32
BF16 = jnp.bfloat16
U32 = jnp.uint32
I32 = jnp.int32

CHUNK = 64
LEFT_CHUNKS = 8
MAX_REL = 128
A_HEADS = 16
A_HEAD_DIM = 128
B_HEADS = 4
B_HEAD_DIM = 128
C_HEADS = 4
C_HEAD_DIM = 256
N_BRANCHES = 3
ROPE_THETA = 10000.0
TOP_K = 4
SWIGLU_LIMIT = 7.0
SWIGLU_ALPHA = 1.702
LN_EPS = 1e-5
RMS_EPS = 1e-5
MASK_VALUE = -1e30

V7X_VMEM_BYTES = 64 * 1024 * 1024
V7X_VMEM_LIMIT = V7X_VMEM_BYTES - 8 * 1024 * 1024
LANES = 128

NT_DIMS = (((1,), (1,)), ((), ()))


def _params(*semantics):
    return pltpu.CompilerParams(dimension_semantics=semantics,
                                vmem_limit_bytes=V7X_VMEM_LIMIT)


def _dot(a, b):
    return jnp.dot(a, b, preferred_element_type=F32)


def _pack_halves(x):
    w = x.shape[1] // 2
    hi = lax.bitcast_convert_type(x[:, :w].astype(jnp.bfloat16).astype(F32), U32)
    lo = lax.bitcast_convert_type(x[:, w:].astype(jnp.bfloat16).astype(F32), U32)
    return hi | (lo >> 16)


def _unpack_halves(p):
    hi = lax.bitcast_convert_type(p & jnp.uint32(0xFFFF0000), F32)
    lo = lax.bitcast_convert_type(p << 16, F32)
    return hi, lo


def _cast_kernel(x_ref, o_ref):
    o_ref[...] = x_ref[...].astype(o_ref.dtype)


def _cast_bf16(x, tm):
    m, d = x.shape
    return pl.pallas_call(
        _cast_kernel,
        out_shape=jax.ShapeDtypeStruct((m, d), BF16),
        grid=(m // tm,),
        in_specs=[pl.BlockSpec((tm, d), lambda i: (i, 0))],
        out_specs=pl.BlockSpec((tm, d), lambda i: (i, 0)),
        compiler_params=_params("parallel"),
        name="cast_bf16",
    )(x)


def _mm_kernel(a_ref, w_ref, o_ref):
    a = a_ref[...].astype(BF16)
    o_ref[...] = _dot(a, w_ref[...].astype(BF16)).astype(o_ref.dtype)


def _matmul(a, w, *, tm, tn, out_dtype, name):
    m, k = a.shape
    n = w.shape[1]
    return pl.pallas_call(
        _mm_kernel,
        out_shape=jax.ShapeDtypeStruct((m, n), out_dtype),
        grid=(n // tn, m // tm),
        in_specs=[pl.BlockSpec((tm, k), lambda j, i: (i, 0)),
                  pl.BlockSpec((k, tn), lambda j, i: (0, j))],
        out_specs=pl.BlockSpec((tm, tn), lambda j, i: (i, j)),
        compiler_params=_params("parallel", "parallel"),
        name=name,
    )(a, w)


A_TQ = 2 * CHUNK
A_WIN = (LEFT_CHUNKS + 2) * CHUNK
A_VARIANTS = LEFT_CHUNKS * CHUNK // A_TQ + 1


def _band_bias_table(rel_bias):
    v = jnp.arange(A_VARIANTS)[:, None, None]
    r = jnp.arange(A_TQ)[None, :, None]
    j = jnp.arange(A_WIN)[None, None, :]
    qp = A_TQ * v + r
    cdiff = qp // CHUNK - j // CHUNK
    valid = (cdiff >= 0) & (cdiff <= LEFT_CHUNKS)
    idx = jnp.clip(qp - j, -MAX_REL, MAX_REL) + MAX_REL
    tb = rel_bias.astype(F32)[:, idx]
    tb = jnp.where(valid[None], tb, MASK_VALUE)
    return jnp.transpose(tb, (1, 0, 2, 3))


def _attn_a_kernel(q_ref, k_ref, v_ref, tb_ref, o_ref, *, heads, dh, scale):
    i = pl.program_id(2)
    var = jnp.minimum(i, A_VARIANTS - 1)
    start = pl.multiple_of(jnp.maximum(i - (A_VARIANTS - 1), 0) * A_TQ, A_TQ)
    for h in range(heads):
        cs = slice(h * dh, (h + 1) * dh)
        q = q_ref[:, cs]
        k = k_ref[pl.ds(start, A_WIN), cs]
        v = v_ref[pl.ds(start, A_WIN), cs]
        s = lax.dot_general(q, k, NT_DIMS, preferred_element_type=F32) * scale + tb_ref[var, h]
        m = jnp.max(s, axis=-1, keepdims=True)
        p = jnp.exp(s - m)
        l = jnp.sum(p, axis=-1, keepdims=True)
        o = _dot(p.astype(BF16), v)
        o_ref[:, cs] = (o / l).astype(o_ref.dtype)


def _attention_a(proj, tb, *, batch, seq, heads, dh, col_q, col_k, col_v, heads_per_step=4):
    n = proj.shape[0]
    gw = heads_per_step * dh
    n_groups = heads // heads_per_step
    n_qb = seq // A_TQ
    kern = functools.partial(_attn_a_kernel, heads=heads_per_step, dh=dh, scale=dh ** -0.5)
    return pl.pallas_call(
        kern,
        out_shape=jax.ShapeDtypeStruct((n, heads * dh), BF16),
        grid=(batch, n_groups, n_qb),
        in_specs=[
            pl.BlockSpec((A_TQ, gw), lambda b, g, i: (b * n_qb + i, col_q // gw + g)),
            pl.BlockSpec((seq, gw), lambda b, g, i: (b, col_k // gw + g)),
            pl.BlockSpec((seq, gw), lambda b, g, i: (b, col_v // gw + g)),
            pl.BlockSpec((A_VARIANTS, heads_per_step, A_TQ, A_WIN), lambda b, g, i: (0, g, 0, 0)),
        ],
        out_specs=pl.BlockSpec((A_TQ, gw), lambda b, g, i: (b * n_qb + i, g)),
        compiler_params=_params("parallel", "parallel", "arbitrary"),
        name="attn_band",
    )(proj, proj, proj, tb)


B_TQ = 256


def _rope_tables(seq, dim):
    inv = 1.0 / (ROPE_THETA ** (jnp.arange(0, dim, 2, dtype=F32) / dim))
    ang = jnp.arange(seq, dtype=F32)[:, None] * inv[None, :]
    ang = jnp.concatenate([ang, ang], -1)
    sign = jnp.where(jnp.arange(dim) < dim // 2, -1.0, 1.0).astype(F32)
    return jnp.cos(ang), jnp.sin(ang) * sign[None, :]


def _rope(x, cos, sin_signed):
    return x * cos + pltpu.roll(x, x.shape[1] // 2, 1) * sin_signed


def _attn_b_kernel(q_ref, k_ref, v_ref, cos_ref, sin_ref, lam_ref, g_ref, o_ref, krot_ref,
                   *, dh, scale, lambda_init):
    qi = pl.program_id(2)
    seq = k_ref.shape[0]

    @pl.when(qi == 0)
    def _():
        for m in range(2):
            kf = k_ref[:, m * dh:(m + 1) * dh].astype(F32)
            krot_ref[m] = _rope(kf, cos_ref[...], sin_ref[...]).astype(BF16)

    row0 = pl.multiple_of(qi * B_TQ, B_TQ)
    cos_q = cos_ref[pl.ds(row0, B_TQ), :]
    sin_q = sin_ref[pl.ds(row0, B_TQ), :]
    lv = lam_ref[...]
    lam = (jnp.exp(jnp.sum(lv[0:1] * lv[1:2], axis=-1, keepdims=True))
           - jnp.exp(jnp.sum(lv[2:3] * lv[3:4], axis=-1, keepdims=True)) + lambda_init)
    q_chunk = (row0 + lax.broadcasted_iota(I32, (B_TQ, seq), 0)) // CHUNK
    k_chunk = lax.broadcasted_iota(I32, (B_TQ, seq), 1) // CHUNK
    allowed = k_chunk <= q_chunk
    probs = []
    for m in range(2):
        qf = q_ref[:, m * dh:(m + 1) * dh].astype(F32)
        qr = _rope(qf, cos_q, sin_q).astype(BF16)
        s = lax.dot_general(qr, krot_ref[m], NT_DIMS, preferred_element_type=F32) * scale
        s = jnp.where(allowed, s, MASK_VALUE)
        e = jnp.exp(s - jnp.max(s, axis=-1, keepdims=True))
        probs.append(e / jnp.sum(e, axis=-1, keepdims=True))
    w = (probs[0] - lam * probs[1]).astype(BF16)
    o = _dot(w, v_ref[...])
    ms = jnp.mean(o * o, axis=-1, keepdims=True)
    y = o * lax.rsqrt(ms + RMS_EPS) * g_ref[...] * (1.0 - lambda_init)
    o_ref[...] = y.astype(o_ref.dtype)


def _attention_b(proj, cos, sin_signed, lam_vecs, norm_g, *, batch, seq, heads, dh,
                 col_q, col_k, col_v, lambda_init):
    n = proj.shape[0]
    hw = 2 * dh
    n_qb = seq // B_TQ
    kern = functools.partial(_attn_b_kernel, dh=dh, scale=dh ** -0.5, lambda_init=lambda_init)
    return pl.pallas_call(
        kern,
        out_shape=jax.ShapeDtypeStruct((n, heads * hw), BF16),
        grid=(batch, heads, n_qb),
        in_specs=[
            pl.BlockSpec((B_TQ, hw), lambda b, h, i: (b * n_qb + i, col_q // hw + h)),
            pl.BlockSpec((seq, hw), lambda b, h, i: (b, col_k // hw + h)),
            pl.BlockSpec((seq, hw), lambda b, h, i: (b, col_v // hw + h)),
            pl.BlockSpec((seq, dh), lambda b, h, i: (0, 0)),
            pl.BlockSpec((seq, dh), lambda b, h, i: (0, 0)),
            pl.BlockSpec((4, dh), lambda b, h, i: (0, 0)),
            pl.BlockSpec((1, hw), lambda b, h, i: (0, 0)),
        ],
        out_specs=pl.BlockSpec((B_TQ, hw), lambda b, h, i: (b * n_qb + i, h)),
        scratch_shapes=[pltpu.VMEM((2, seq, dh), BF16)],
        compiler_params=_params("parallel", "parallel", "arbitrary"),
        name="attn_diff",
    )(proj, proj, proj, cos, sin_signed, lam_vecs, norm_g)


C_TQ = 512


def _attn_c_kernel(q_ref, k_ref, v_ref, o_ref, *, scale):
    s = lax.dot_general(q_ref[...], k_ref[...], NT_DIMS, preferred_element_type=F32) * scale
    e = jnp.exp(s - jnp.max(s, axis=-1, keepdims=True))
    p = (e / jnp.sum(e, axis=-1, keepdims=True)).astype(BF16)
    o_ref[...] = _dot(p, v_ref[...]).astype(o_ref.dtype)


def _attention_c(proj, ckv, *, batch, seq, mem_len, heads, dh, col_q):
    n = proj.shape[0]
    n_qb = seq // C_TQ
    kern = functools.partial(_attn_c_kernel, scale=dh ** -0.5)
    return pl.pallas_call(
        kern,
        out_shape=jax.ShapeDtypeStruct((n, heads * dh), BF16),
        grid=(batch, heads, n_qb),
        in_specs=[
            pl.BlockSpec((C_TQ, dh), lambda b, h, i: (b * n_qb + i, col_q // dh + h)),
            pl.BlockSpec((mem_len, dh), lambda b, h, i: (b, h)),
            pl.BlockSpec((mem_len, dh), lambda b, h, i: (b, heads + h)),
        ],
        out_specs=pl.BlockSpec((C_TQ, dh), lambda b, h, i: (b * n_qb + i, h)),
        compiler_params=_params("parallel", "parallel", "parallel"),
        name="attn_mem",
    )(proj, ckv, ckv)


def _merge_kernel(x_ref, ya_ref, yb_ref, yc_ref, wga_ref, wgb_ref, wgc_ref,
                  bga_ref, bgb_ref, bgc_ref, pa_ref, pb_ref, pc_ref, o_ref):
    x = x_ref[...]
    acc = None
    for wg, bg, y, p in ((wga_ref, bga_ref, ya_ref, pa_ref),
                         (wgb_ref, bgb_ref, yb_ref, pb_ref),
                         (wgc_ref, bgc_ref, yc_ref, pc_ref)):
        gate = jax.nn.sigmoid(_dot(x, wg[...].astype(BF16)) + bg[...])
        term = gate * _dot(y[...], p[...].astype(BF16))
        acc = term if acc is None else acc + term
    o_ref[...] = acc.astype(o_ref.dtype)


def _gated_merge(xb, ya, yb, yc, w_gates, b_gates, pa, pb, pc, *, tm, tn):
    n, d = xb.shape
    nj = d // tn
    row = lambda width: pl.BlockSpec((tm, width), lambda j, i: (i, 0))
    gate_w = lambda br: pl.BlockSpec((d, tn), lambda j, i, br=br: (0, br * nj + j))
    gate_b = lambda br: pl.BlockSpec((1, tn), lambda j, i, br=br: (0, br * nj + j))
    branch_w = lambda width: pl.BlockSpec((width, tn), lambda j, i: (0, j))
    return pl.pallas_call(
        _merge_kernel,
        out_shape=jax.ShapeDtypeStruct((n, d), BF16),
        grid=(nj, n // tm),
        in_specs=[row(d), row(ya.shape[1]), row(yb.shape[1]), row(yc.shape[1]),
                  gate_w(0), gate_w(1), gate_w(2), gate_b(0), gate_b(1), gate_b(2),
                  branch_w(pa.shape[0]), branch_w(pb.shape[0]), branch_w(pc.shape[0])],
        out_specs=pl.BlockSpec((tm, tn), lambda j, i: (i, j)),
        compiler_params=_params("parallel", "parallel"),
        name="gated_merge",
    )(xb, ya, yb, yc, w_gates, w_gates, w_gates, b_gates, b_gates, b_gates, pa, pb, pc)


def _layer_norm(z, g, b):
    mu = jnp.mean(z, axis=-1, keepdims=True)
    zc = z - mu
    var = jnp.mean(zc * zc, axis=-1, keepdims=True)
    return zc * lax.rsqrt(var + LN_EPS) * g + b


def _ln_router_kernel(x_ref, m_ref, g_ref, b_ref, wr_ref, br_ref,
                      h_ref, hp_ref, idx_ref, gate_ref, rank_ref, cnt_ref, carry_ref,
                      *, alpha, n_exp):
    @pl.when(pl.program_id(0) == 0)
    def _():
        carry_ref[...] = jnp.zeros_like(carry_ref)

    tm = x_ref.shape[0]
    h = _layer_norm(alpha * x_ref[...] + m_ref[...], g_ref[...], b_ref[...])
    h_ref[...] = h
    hp_ref[...] = _pack_halves(h)

    h_hi = h.astype(BF16)
    h_lo = (h - h_hi.astype(F32)).astype(BF16)
    w = wr_ref[...]
    r1 = _dot(h_hi, w)
    logits = r1[:, :LANES] + r1[:, LANES:] + _dot(h_lo, w[:, :LANES]) + br_ref[...]

    lane = lax.broadcasted_iota(I32, (tm, LANES), 1)
    lane_f = lane.astype(F32)
    cur = jnp.where(lane < n_exp, logits, -jnp.inf)
    vals, idxs = [], []
    for _ in range(TOP_K):
        mx = jnp.max(cur, axis=-1, keepdims=True)
        ix = jnp.min(jnp.where(cur == mx, lane_f, float(LANES)), axis=-1, keepdims=True).astype(I32)
        vals.append(mx)
        idxs.append(ix)
        cur = jnp.where(lane == ix, -jnp.inf, cur)
    exps = [jnp.exp(v - vals[0]) for v in vals]
    den = exps[0]
    for e in exps[1:]:
        den = den + e

    tri = (lax.broadcasted_iota(I32, (tm, tm), 0) > lax.broadcasted_iota(I32, (tm, tm), 1)).astype(BF16)
    carry = carry_ref[...]
    idx_out = jnp.zeros((tm, LANES), I32)
    gate_out = jnp.zeros((tm, LANES), F32)
    rank_out = jnp.zeros((tm, LANES), I32)
    for k in range(TOP_K):
        onehot = (lane == idxs[k]).astype(F32)
        before = _dot(tri, onehot.astype(BF16)) + carry
        rank = jnp.sum(onehot * before, axis=-1, keepdims=True)
        carry = carry + jnp.sum(onehot, axis=0, keepdims=True)
        idx_out = jnp.where(lane == k, idxs[k], idx_out)
        gate_out = jnp.where(lane == k, exps[k] / den, gate_out)
        rank_out = jnp.where(lane == k, rank.astype(I32), rank_out)
    carry_ref[...] = carry
    idx_ref[...] = idx_out
    gate_ref[...] = gate_out
    rank_ref[...] = rank_out
    cnt_ref[...] = carry


def _ln_router(x, m, g, b, wr_split, br_pad, *, alpha, n_exp, tm):
    n, d = x.shape
    row = pl.BlockSpec((tm, d), lambda i: (i, 0))
    vec = pl.BlockSpec((1, d), lambda i: (0, 0))
    small = pl.BlockSpec((tm, LANES), lambda i: (i, 0))
    kern = functools.partial(_ln_router_kernel, alpha=alpha, n_exp=n_exp)
    return pl.pallas_call(
        kern,
        out_shape=(jax.ShapeDtypeStruct((n, d), F32),
                   jax.ShapeDtypeStruct((n, d // 2), U32),
                   jax.ShapeDtypeStruct((n, LANES), I32),
                   jax.ShapeDtypeStruct((n, LANES), F32),
                   jax.ShapeDtypeStruct((n, LANES), I32),
                   jax.ShapeDtypeStruct((1, LANES), F32)),
        grid=(n // tm,),
        in_specs=[row, row, vec, vec,
                  pl.BlockSpec((d, 2 * LANES), lambda i: (0, 0)),
                  pl.BlockSpec((1, LANES), lambda i: (0, 0))],
        out_specs=(row, pl.BlockSpec((tm, d // 2), lambda i: (i, 0)), small, small, small,
                   pl.BlockSpec((1, LANES), lambda i: (0, 0))),
        scratch_shapes=[pltpu.VMEM((1, LANES), F32)],
        compiler_params=_params("arbitrary"),
        name="ln_router",
    )(x, m, g, b, wr_split, br_pad)


def _row_copy(src, src_row, dst, dst_row, sem):
    return pltpu.make_async_copy(src.at[pl.ds(src_row, 1)], dst.at[pl.ds(dst_row, 1)], sem)


def _dispatch_kernel(dest_ref, hp_ref, xg_ref, sem):
    tm = hp_ref.shape[0]

    def issue(t, carry):
        for k in range(TOP_K):
            _row_copy(hp_ref, t, xg_ref, dest_ref[0, t * TOP_K + k], sem).start()
        return carry

    lax.fori_loop(0, tm, issue, 0)

    def drain(t, carry):
        for k in range(TOP_K):
            _row_copy(hp_ref, 0, xg_ref, 0, sem).wait()
        return carry

    lax.fori_loop(0, tm, drain, 0)


def _dispatch(dest, hp, *, tm):
    n, w = hp.shape
    dest3 = dest.reshape(n // tm, 1, tm * TOP_K)
    return pl.pallas_call(
        _dispatch_kernel,
        out_shape=jax.ShapeDtypeStruct((n * TOP_K, w), U32),
        grid=(n // tm,),
        in_specs=[pl.BlockSpec((None, 1, tm * TOP_K), lambda i: (i, 0, 0), memory_space=pltpu.SMEM),
                  pl.BlockSpec((tm, w), lambda i: (i, 0))],
        out_specs=pl.BlockSpec(memory_space=pl.ANY),
        scratch_shapes=[pltpu.SemaphoreType.DMA(())],
        compiler_params=_params("arbitrary"),
        name="dispatch",
    )(dest3, hp)


def _visit_schedule(counts, n_rows, tr):
    n_exp = counts.shape[0]
    n_tiles = n_rows // tr
    n_vis = n_tiles + n_exp
    gend = jnp.cumsum(counts)
    gstart = gend - counts
    first_tile = gstart // tr
    last_tile = jnp.maximum(gend - 1, 0) // tr
    nvis = jnp.where(counts > 0, last_tile - first_tile + 1, 0)
    vend = jnp.cumsum(nvis)
    vstart = vend - nvis
    total = vend[-1]
    v = jnp.arange(n_vis, dtype=I32)
    vc = jnp.minimum(v, total - 1)
    e_v = jnp.minimum(jnp.searchsorted(vend, vc, side="right"), n_exp - 1).astype(I32)
    tile_v = (first_tile[e_v] + (vc - vstart[e_v])).astype(I32)
    lo = jnp.clip(gstart[e_v] - tile_v * tr, 0, tr)
    hi = jnp.clip(gend[e_v] - tile_v * tr, 0, tr)
    live = v < total
    lo = jnp.where(live, lo, 0).astype(I32)
    hi = jnp.where(live, hi, 0).astype(I32)
    return tile_v, e_v, lo, hi


def _visit_state(vt, vlo, vhi):
    v = pl.program_id(1)
    lo = vlo[v]
    hi = vhi[v]
    first = jnp.logical_or(v == 0, vt[v] != vt[jnp.maximum(v - 1, 0)])
    return lo, hi, first


def _store_rows(o_ref, val, lo, hi, first):
    rows = lax.broadcasted_iota(I32, (o_ref.shape[0], 1), 0)
    mine = (rows >= lo) & (rows < hi)

    @pl.when(first)
    def _():
        o_ref[...] = jnp.where(mine, val, jnp.zeros_like(val))

    @pl.when(jnp.logical_not(first))
    def _():
        o_ref[...] = jnp.where(mine, val, o_ref[...])


SEL_W = 512


def _even_lane_selector():
    r = jnp.arange(SEL_W)[:, None]
    c = jnp.arange(SEL_W // 2)[None, :]
    return (r == 2 * c).astype(BF16)


def _up_kernel(vt, ve, vlo, vhi, xg_ref, w1_ref, b1_ref, sel_ref, o_ref):
    lo, hi, first = _visit_state(vt, vlo, vhi)

    @pl.when(hi > lo)
    def _():
        half = w1_ref.shape[0] // 2
        tn = w1_ref.shape[1]
        xa, xb = _unpack_halves(xg_ref[...])
        h = (_dot(xa.astype(BF16), w1_ref[:half, :].astype(BF16))
             + _dot(xb.astype(BF16), w1_ref[half:, :].astype(BF16)) + b1_ref[...])
        glu = jnp.minimum(h, SWIGLU_LIMIT)
        lin = jnp.clip(h, -SWIGLU_LIMIT, SWIGLU_LIMIT) + 1.0
        gact = glu * jax.nn.sigmoid(SWIGLU_ALPHA * glu)
        parts = []
        for c in range(tn // LANES):
            cs = slice(c * LANES, (c + 1) * LANES)
            parts.append(gact[:, cs] * pltpu.roll(lin[:, cs], LANES - 1, 1))
        inter = jnp.concatenate(parts, axis=1).astype(BF16)
        acts = [_dot(inter[:, s * SEL_W:(s + 1) * SEL_W], sel_ref[...]) for s in range(tn // SEL_W)]
        act = jnp.concatenate(acts, axis=1).astype(o_ref.dtype)
        _store_rows(o_ref, act, lo, hi, first)


def _expert_up(sched, xg, w1, b1, *, tr, tn):
    p_rows, w = xg.shape
    n_exp, d, f2 = w1.shape
    n_vis = sched[0].shape[0]
    grid_spec = pltpu.PrefetchScalarGridSpec(
        num_scalar_prefetch=4,
        grid=(f2 // tn, n_vis),
        in_specs=[
            pl.BlockSpec((tr, w), lambda c, v, vt, ve, vlo, vhi: (vt[v], 0)),
            pl.BlockSpec((None, d, tn), lambda c, v, vt, ve, vlo, vhi: (ve[v], 0, c)),
            pl.BlockSpec((None, 1, tn), lambda c, v, vt, ve, vlo, vhi: (ve[v], 0, c)),
            pl.BlockSpec((SEL_W, SEL_W // 2), lambda c, v, vt, ve, vlo, vhi: (0, 0)),
        ],
        out_specs=pl.BlockSpec((tr, tn // 2), lambda c, v, vt, ve, vlo, vhi: (vt[v], c)),
    )
    return pl.pallas_call(
        _up_kernel,
        out_shape=jax.ShapeDtypeStruct((p_rows, f2 // 2), BF16),
        grid_spec=grid_spec,
        compiler_params=_params("arbitrary", "arbitrary"),
        name="expert_up",
    )(*sched, xg, w1, b1.reshape(n_exp, 1, f2), _even_lane_selector())


def _down_kernel(vt, ve, vlo, vhi, act_ref, w2_ref, b2_ref, o_ref):
    lo, hi, first = _visit_state(vt, vlo, vhi)

    @pl.when(hi > lo)
    def _():
        y = _dot(act_ref[...], w2_ref[...].astype(BF16)) + b2_ref[...]
        _store_rows(o_ref, _pack_halves(y), lo, hi, first)


def _expert_down(sched, act, w2, b2, *, tr, tn):
    p_rows, f = act.shape
    n_exp, _, d = w2.shape
    n_vis = sched[0].shape[0]
    grid_spec = pltpu.PrefetchScalarGridSpec(
        num_scalar_prefetch=4,
        grid=(d // tn, n_vis),
        in_specs=[
            pl.BlockSpec((tr, f), lambda c, v, vt, ve, vlo, vhi: (vt[v], 0)),
            pl.BlockSpec((None, f, tn), lambda c, v, vt, ve, vlo, vhi: (ve[v], 0, c)),
            pl.BlockSpec((None, 1, tn), lambda c, v, vt, ve, vlo, vhi: (ve[v], 0, c)),
        ],
        out_specs=pl.BlockSpec((tr, tn // 2), lambda c, v, vt, ve, vlo, vhi: (vt[v], c)),
    )
    return pl.pallas_call(
        _down_kernel,
        out_shape=jax.ShapeDtypeStruct((p_rows, d // 2), U32),
        grid_spec=grid_spec,
        compiler_params=_params("arbitrary", "arbitrary"),
        name="expert_down",
    )(*sched, act, w2, b2.reshape(n_exp, 1, d))


def _combine_kernel(dest_ref, h_ref, gate_ref, g_ref, b_ref, y_ref, o_ref, buf_ref, sem,
                    *, alpha, chunk):
    tm = h_ref.shape[0]

    def issue(t, carry):
        for k in range(TOP_K):
            pltpu.make_async_copy(y_ref.at[pl.ds(dest_ref[0, t * TOP_K + k], 1)],
                                  buf_ref.at[k, pl.ds(t, 1)], sem).start()
        return carry

    lax.fori_loop(0, tm, issue, 0)

    def drain(t, carry):
        for k in range(TOP_K):
            pltpu.make_async_copy(y_ref.at[pl.ds(0, 1)], buf_ref.at[k, pl.ds(0, 1)], sem).wait()
        return carry

    lax.fori_loop(0, tm, drain, 0)

    gates = gate_ref[...]
    acc_hi = None
    acc_lo = None
    for k in range(TOP_K):
        hi, lo = _unpack_halves(buf_ref[k])
        gk = gates[:, k:k + 1]
        acc_hi = gk * hi if acc_hi is None else acc_hi + gk * hi
        acc_lo = gk * lo if acc_lo is None else acc_lo + gk * lo
    hw = chunk // 2
    pieces = []
    for c in range(acc_hi.shape[1] // hw):
        pieces.append(acc_hi[:, c * hw:(c + 1) * hw])
        pieces.append(acc_lo[:, c * hw:(c + 1) * hw])
    ffn = jnp.concatenate(pieces, axis=1)
    o_ref[...] = _layer_norm(alpha * h_ref[...] + ffn, g_ref[...], b_ref[...]).astype(o_ref.dtype)


def _combine(dest, h, gates, g, b, y, *, alpha, tm, chunk):
    n, d = h.shape
    dest3 = dest.reshape(n // tm, 1, tm * TOP_K)
    kern = functools.partial(_combine_kernel, alpha=alpha, chunk=chunk)
    return pl.pallas_call(
        kern,
        out_shape=jax.ShapeDtypeStruct((n, d), F32),
        grid=(n // tm,),
        in_specs=[pl.BlockSpec((None, 1, tm * TOP_K), lambda i: (i, 0, 0), memory_space=pltpu.SMEM),
                  pl.BlockSpec((tm, d), lambda i: (i, 0)),
                  pl.BlockSpec((tm, LANES), lambda i: (i, 0)),
                  pl.BlockSpec((1, d), lambda i: (0, 0)),
                  pl.BlockSpec((1, d), lambda i: (0, 0)),
                  pl.BlockSpec(memory_space=pl.ANY)],
        out_specs=pl.BlockSpec((tm, d), lambda i: (i, 0)),
        scratch_shapes=[pltpu.VMEM((TOP_K, tm, d // 2), U32), pltpu.SemaphoreType.DMA(())],
        compiler_params=_params("arbitrary"),
        name="combine_ln",
    )(dest3, h, gates, g, b, y)


def _tiles(n_tokens, d_model, d_expert):
    return dict(
        cast_tm=min(512, n_tokens),
        proj=dict(tm=min(1024, n_tokens), tn=512),
        memkv=dict(tm=512, tn=512),
        merge=dict(tm=min(256, n_tokens), tn=256),
        out=dict(tm=min(1024, n_tokens), tn=512),
        ln_tm=min(256, n_tokens),
        dispatch_tm=min(256, n_tokens),
        moe_tr=256,
        up_tn=min(1024, 2 * d_expert),
        down_tn=min(2048, d_model),
        combine_tm=min(128, n_tokens),
    )


def _layer(h, mem2, lw, *, batch, seq, mem_len, lambda_init, alpha):
    n, d = h.shape
    t = _tiles(n, d, lw["w_mlp2"].shape[1])
    a_width = A_HEADS * A_HEAD_DIM
    b_width = B_HEADS * 2 * B_HEAD_DIM
    col = dict(a_q=0, a_k=a_width, a_v=2 * a_width, b_q=3 * a_width, b_k=3 * a_width + b_width,
               b_v=3 * a_width + 2 * b_width, c_q=3 * a_width + 3 * b_width)

    xb = _cast_bf16(h, t["cast_tm"])
    proj = _matmul(xb, lw["w_in"], out_dtype=BF16, name="in_proj", **t["proj"])
    ckv = _matmul(mem2, lw["w_mem_kv"], out_dtype=BF16, name="mem_kv",
                  tm=min(t["memkv"]["tm"], mem2.shape[0]), tn=t["memkv"]["tn"])

    ya = _attention_a(proj, _band_bias_table(lw["rel_bias"]), batch=batch, seq=seq,
                      heads=A_HEADS, dh=A_HEAD_DIM, col_q=col["a_q"], col_k=col["a_k"], col_v=col["a_v"])
    cos, sin_signed = _rope_tables(seq, B_HEAD_DIM)
    lam_vecs = jnp.stack([lw["lambda_q1"], lw["lambda_k1"], lw["lambda_q2"], lw["lambda_k2"]]).astype(F32)
    yb = _attention_b(proj, cos, sin_signed, lam_vecs, lw["diff_norm_g"].reshape(1, -1),
                      batch=batch, seq=seq, heads=B_HEADS, dh=B_HEAD_DIM,
                      col_q=col["b_q"], col_k=col["b_k"], col_v=col["b_v"], lambda_init=lambda_init)
    yc = _attention_c(proj, ckv, batch=batch, seq=seq, mem_len=mem_len, heads=C_HEADS,
                      dh=C_HEAD_DIM, col_q=col["c_q"])

    merged = _gated_merge(xb, ya, yb, yc, lw["w_gates"], lw["b_gates"].reshape(1, -1),
                          lw["w_branch_a"], lw["w_branch_b"], lw["w_branch_c"], **t["merge"])
    mix = _matmul(merged, lw["w_o"], out_dtype=F32, name="out_proj", **t["out"])

    n_exp = lw["w_router"].shape[1]
    wr = jnp.pad(lw["w_router"], ((0, 0), (0, LANES - n_exp)))
    wr_hi = wr.astype(BF16)
    wr_lo = (wr - wr_hi.astype(F32)).astype(BF16)
    br = jnp.pad(lw["b_router"], (0, LANES - n_exp)).reshape(1, LANES)
    h1, h1_packed, top_idx, gates, rank, cnt = _ln_router(
        h, mix, lw["ln1_g"].reshape(1, -1), lw["ln1_b"].reshape(1, -1),
        jnp.concatenate([wr_hi, wr_lo], axis=1), br, alpha=alpha, n_exp=n_exp, tm=t["ln_tm"])

    counts = cnt[0, :n_exp].astype(I32)
    gstart = jnp.cumsum(counts) - counts
    dest = (gstart[top_idx[:, :TOP_K]] + rank[:, :TOP_K]).astype(I32)
    sched = _visit_schedule(counts, n * TOP_K, t["moe_tr"])

    xg = _dispatch(dest, h1_packed, tm=t["dispatch_tm"])
    act = _expert_up(sched, xg, lw["w_mlp1"], lw["b_mlp1"], tr=t["moe_tr"], tn=t["up_tn"])
    y = _expert_down(sched, act, lw["w_mlp2"], lw["b_mlp2"], tr=t["moe_tr"], tn=t["down_tn"])
    return _combine(dest, h1, gates, lw["ln2_g"].reshape(1, -1), lw["ln2_b"].reshape(1, -1), y,
                    alpha=alpha, tm=t["combine_tm"], chunk=t["down_tn"])


def kernel(x, mem, w_in, w_mem_kv, rel_bias, lambda_q1, lambda_k1, lambda_q2, lambda_k2, diff_norm_g,
           w_branch_a, w_branch_b, w_branch_c, w_gates, b_gates, w_o, ln1_g, ln1_b, w_router, b_router,
           w_mlp1, b_mlp1, w_mlp2, b_mlp2, ln2_g, ln2_b):
    batch, seq, d = x.shape
    mem_len = mem.shape[1]
    depth = w_in.shape[0]
    alpha = (2 * depth) ** 0.25
    stacked = dict(w_in=w_in, w_mem_kv=w_mem_kv, rel_bias=rel_bias, lambda_q1=lambda_q1,
                   lambda_k1=lambda_k1, lambda_q2=lambda_q2, lambda_k2=lambda_k2, diff_norm_g=diff_norm_g,
                   w_branch_a=w_branch_a, w_branch_b=w_branch_b, w_branch_c=w_branch_c, w_gates=w_gates,
                   b_gates=b_gates, w_o=w_o, ln1_g=ln1_g, ln1_b=ln1_b, w_router=w_router,
                   b_router=b_router, w_mlp1=w_mlp1, b_mlp1=b_mlp1, w_mlp2=w_mlp2, b_mlp2=b_mlp2,
                   ln2_g=ln2_g, ln2_b=ln2_b)
    h = x.reshape(batch * seq, d)
    mem2 = mem.reshape(batch * mem_len, d)
    for l in range(depth):
        lw = {name: w[l] for name, w in stacked.items()}
        lambda_init = 0.8 - 0.6 * math.exp(-0.3 * l)
        h = _layer(h, mem2, lw, batch=batch, seq=seq, mem_len=mem_len, lambda_init=lambda_init, alpha=alpha)
    return h.reshape(batch, seq, d)
```

```python
import functools
import math

import jax
import jax.numpy as jnp
from jax import lax
from jax.experimental import pallas as pl
from jax.experimental.pallas import tpu as pltpu

F32 = jnp.float32
BF16 = jnp.bfloat16
U32 = jnp.uint32
I32 = jnp.int32

CHUNK = 64
LEFT_CHUNKS = 8
MAX_REL = 128
A_HEADS = 16
A_HEAD_DIM = 128
B_HEADS = 4
B_HEAD_DIM = 128
C_HEADS = 4
C_HEAD_DIM = 256
N_BRANCHES = 3
ROPE_THETA = 10000.0
TOP_K = 4
SWIGLU_LIMIT = 7.0
SWIGLU_ALPHA = 1.702
LN_EPS = 1e-5
RMS_EPS = 1e-5
MASK_VALUE = -1e30

V7X_VMEM_BYTES = 64 * 1024 * 1024
V7X_VMEM_LIMIT = V7X_VMEM_BYTES - 8 * 1024 * 1024
LANES = 128

NT_DIMS = (((1,), (1,)), ((), ()))


def _params(*semantics):
    return pltpu.CompilerParams(dimension_semantics=semantics,
                                vmem_limit_bytes=V7X_VMEM_LIMIT)


def _dot(a, b):
    return jnp.dot(a, b, preferred_element_type=F32)


def _pack_halves(x):
    w = x.shape[1] // 2
    hi = lax.bitcast_convert_type(x[:, :w].astype(jnp.bfloat16).astype(F32), U32)
    lo = lax.bitcast_convert_type(x[:, w:].astype(jnp.bfloat16).astype(F32), U32)
    return hi | (lo >> 16)


def _unpack_halves(p):
    hi = lax.bitcast_convert_type(p & jnp.uint32(0xFFFF0000), F32)
    lo = lax.bitcast_convert_type(p << 16, F32)
    return hi, lo


def _cast_kernel(x_ref, o_ref):
    o_ref[...] = x_ref[...].astype(o_ref.dtype)


def _cast_bf16(x, tm):
    m, d = x.shape
    return pl.pallas_call(
        _cast_kernel,
        out_shape=jax.ShapeDtypeStruct((m, d), BF16),
        grid=(m // tm,),
        in_specs=[pl.BlockSpec((tm, d), lambda i: (i, 0))],
        out_specs=pl.BlockSpec((tm, d), lambda i: (i, 0)),
        compiler_params=_params("parallel"),
        name="cast_bf16",
    )(x)


def _mm_kernel(a_ref, w_ref, o_ref):
    a = a_ref[...].astype(BF16)
    o_ref[...] = _dot(a, w_ref[...].astype(BF16)).astype(o_ref.dtype)


def _matmul(a, w, *, tm, tn, out_dtype, name):
    m, k = a.shape
    n = w.shape[1]
    return pl.pallas_call(
        _mm_kernel,
        out_shape=jax.ShapeDtypeStruct((m, n), out_dtype),
        grid=(n // tn, m // tm),
        in_specs=[pl.BlockSpec((tm, k), lambda j, i: (i, 0)),
                  pl.BlockSpec((k, tn), lambda j, i: (0, j))],
        out_specs=pl.BlockSpec((tm, tn), lambda j, i: (i, j)),
        compiler_params=_params("parallel", "parallel"),
        name=name,
    )(a, w)


A_TQ = 2 * CHUNK
A_WIN = (LEFT_CHUNKS + 2) * CHUNK
A_VARIANTS = LEFT_CHUNKS * CHUNK // A_TQ + 1


A_BASE_W = A_WIN + A_TQ


def _band_bias_base(rel_bias):
    reach = A_WIN
    ext =jnp.pad(rel_bias.astype(F32), ((0, 0), (reach - MAX_REL, reach - MAX_REL)), mode="edge")
    rev = ext[:, ::-1]
    rows = []
    for v in range(A_VARIANTS):
        c = rev[:, reach - A_TQ * v - A_TQ: reach - A_TQ * v + A_WIN]
        rows.append(jnp.concatenate([c[:, A_TQ:], c[:, :A_TQ]], axis=1))
    return jnp.stack(rows)[:, :, None, :]


def _attn_a_kernel(q_ref, k_ref, v_ref, base_ref, o_ref, tb_ref, *, heads, dh, scale):
    i = pl.program_id(2)

    @pl.when(i == 0)
    def _():
        r = lax.broadcasted_iota(I32, (A_TQ, A_WIN), 0)
        j = lax.broadcasted_iota(I32, (A_TQ, A_WIN), 1)
        for var in range(A_VARIANTS):
            cdiff = (A_TQ * var + r) // CHUNK - j // CHUNK
            valid = (cdiff >= 0) & (cdiff <= LEFT_CHUNKS)
            for h in range(heads):
                rows = jnp.broadcast_to(base_ref[var, h], (A_TQ, A_BASE_W))
                toeplitz = pltpu.roll(rows, 0, 1, stride=1, stride_axis=0)[:, :A_WIN]
                tb_ref[var, h] = jnp.where(valid, toeplitz, MASK_VALUE)

    var = jnp.minimum(i, A_VARIANTS - 1)
    start = pl.multiple_of(jnp.maximum(i - (A_VARIANTS - 1), 0) * A_TQ, A_TQ)
    for h in range(heads):
        cs = slice(h * dh, (h + 1) * dh)
        q = q_ref[:, cs]
        k = k_ref[pl.ds(start, A_WIN), cs]
        v = v_ref[pl.ds(start, A_WIN), cs]
        s = lax.dot_general(q, k, NT_DIMS, preferred_element_type=F32) * scale + tb_ref[var, h]
        m = jnp.max(s, axis=-1, keepdims=True)
        p = jnp.exp(s - m)
        l = jnp.sum(p, axis=-1, keepdims=True)
        o = _dot(p.astype(BF16), v)
        o_ref[:, cs] = (o / l).astype(o_ref.dtype)


def _attention_a(proj, base, *, batch, seq, heads, dh, col_q, col_k, col_v, heads_per_step=8):
    n = proj.shape[0]
    gw = heads_per_step * dh
    n_groups = heads // heads_per_step
    n_qb = seq // A_TQ
    kern = functools.partial(_attn_a_kernel, heads=heads_per_step, dh=dh, scale=dh ** -0.5)
    return pl.pallas_call(
        kern,
        out_shape=jax.ShapeDtypeStruct((n, heads * dh), BF16),
        grid=(batch, n_groups, n_qb),
        in_specs=[
            pl.BlockSpec((A_TQ, gw), lambda b, g, i: (b * n_qb + i, col_q // gw + g)),
            pl.BlockSpec((seq, gw), lambda b, g, i: (b, col_k // gw + g)),
            pl.BlockSpec((seq, gw), lambda b, g, i: (b, col_v // gw + g)),
            pl.BlockSpec((A_VARIANTS, heads_per_step, 1, A_BASE_W), lambda b, g, i: (0, g, 0, 0)),
        ],
        out_specs=pl.BlockSpec((A_TQ, gw), lambda b, g, i: (b * n_qb + i, g)),
        scratch_shapes=[pltpu.VMEM((A_VARIANTS, heads_per_step, A_TQ, A_WIN), F32)],
        compiler_params=_params("parallel", "parallel", "arbitrary"),
        name="attn_band",
    )(proj, proj, proj, base)


B_TQ = 256


def _rope_tables(seq, dim):
    inv = 1.0 / (ROPE_THETA ** (jnp.arange(0, dim, 2, dtype=F32) / dim))
    ang = jnp.arange(seq, dtype=F32)[:, None] * inv[None, :]
    ang = jnp.concatenate([ang, ang], -1)
    sign = jnp.where(jnp.arange(dim) < dim // 2, -1.0, 1.0).astype(F32)
    return jnp.cos(ang), jnp.sin(ang) * sign[None, :]


def _rope(x, cos, sin_signed):
    return x * cos + pltpu.roll(x, x.shape[1] // 2, 1) * sin_signed


def _attn_b_kernel(q_ref, k_ref, v_ref, cos_ref, sin_ref, lam_ref, g_ref, o_ref, krot_ref,
                   *, dh, scale, lambda_init):
    qi = pl.program_id(2)
    seq = k_ref.shape[0]

    @pl.when(qi == 0)
    def _():
        for m in range(2):
            kf = k_ref[:, m * dh:(m + 1) * dh].astype(F32)
            krot_ref[m] = _rope(kf, cos_ref[...], sin_ref[...]).astype(BF16)

    lv = lam_ref[...]
    lam = (jnp.exp(jnp.sum(lv[0:1] * lv[1:2], axis=-1, keepdims=True))
           - jnp.exp(jnp.sum(lv[2:3] * lv[3:4], axis=-1, keepdims=True)) + lambda_init)

    def block(blk):
        row0 = blk * B_TQ
        kl = row0 + B_TQ
        cos_q = cos_ref[row0:kl, :]
        sin_q = sin_ref[row0:kl, :]
        q_chunk = (row0 + lax.broadcasted_iota(I32, (B_TQ, kl), 0)) // CHUNK
        k_chunk = lax.broadcasted_iota(I32, (B_TQ, kl), 1) // CHUNK
        allowed = k_chunk <= q_chunk
        probs = []
        for m in range(2):
            qf = q_ref[:, m * dh:(m + 1) * dh].astype(F32)
            qr = _rope(qf, cos_q, sin_q).astype(BF16)
            s = lax.dot_general(qr, krot_ref[m, :kl, :], NT_DIMS, preferred_element_type=F32) * scale
            s = jnp.where(allowed, s, MASK_VALUE)
            e = jnp.exp(s - jnp.max(s, axis=-1, keepdims=True))
            probs.append(e / jnp.sum(e, axis=-1, keepdims=True))
        w = (probs[0] - lam * probs[1]).astype(BF16)
        o = _dot(w, v_ref[:kl, :])
        ms = jnp.mean(o * o, axis=-1, keepdims=True)
        y = o * lax.rsqrt(ms + RMS_EPS) * g_ref[...] * (1.0 - lambda_init)
        o_ref[...] = y.astype(o_ref.dtype)

    for blk in range(seq // B_TQ):
        pl.when(qi == blk)(functools.partial(block, blk))


def _attention_b(proj, cos, sin_signed, lam_vecs, norm_g, *, batch, seq, heads, dh,
                 col_q, col_k, col_v, lambda_init):
    n = proj.shape[0]
    hw = 2 * dh
    n_qb = seq // B_TQ
    kern = functools.partial(_attn_b_kernel, dh=dh, scale=dh ** -0.5, lambda_init=lambda_init)
    return pl.pallas_call(
        kern,
        out_shape=jax.ShapeDtypeStruct((n, heads * hw), BF16),
        grid=(batch, heads, n_qb),
        in_specs=[
            pl.BlockSpec((B_TQ, hw), lambda b, h, i: (b * n_qb + i, col_q // hw + h)),
            pl.BlockSpec((seq, hw), lambda b, h, i: (b, col_k // hw + h)),
            pl.BlockSpec((seq, hw), lambda b, h, i: (b, col_v // hw + h)),
            pl.BlockSpec((seq, dh), lambda b, h, i: (0, 0)),
            pl.BlockSpec((seq, dh), lambda b, h, i: (0, 0)),
            pl.BlockSpec((4, dh), lambda b, h, i: (0, 0)),
            pl.BlockSpec((1, hw), lambda b, h, i: (0, 0)),
        ],
        out_specs=pl.BlockSpec((B_TQ, hw), lambda b, h, i: (b * n_qb + i, h)),
        scratch_shapes=[pltpu.VMEM((2, seq, dh), BF16)],
        compiler_params=_params("parallel", "parallel", "arbitrary"),
        name="attn_diff",
    )(proj, proj, proj, cos, sin_signed, lam_vecs, norm_g)


C_TQ = 512


def _attn_c_kernel(q_ref, k_ref, v_ref, o_ref, *, scale):
    s = lax.dot_general(q_ref[...], k_ref[...], NT_DIMS, preferred_element_type=F32) * scale
    e = jnp.exp(s - jnp.max(s, axis=-1, keepdims=True))
    p = (e / jnp.sum(e, axis=-1, keepdims=True)).astype(BF16)
    o_ref[...] = _dot(p, v_ref[...]).astype(o_ref.dtype)


def _attention_c(proj, ckv, *, batch, seq, mem_len, heads, dh, col_q):
    n = proj.shape[0]
    n_qb = seq // C_TQ
    kern = functools.partial(_attn_c_kernel, scale=dh ** -0.5)
    return pl.pallas_call(
        kern,
        out_shape=jax.ShapeDtypeStruct((n, heads * dh), BF16),
        grid=(batch, heads, n_qb),
        in_specs=[
            pl.BlockSpec((C_TQ, dh), lambda b, h, i: (b * n_qb + i, col_q // dh + h)),
            pl.BlockSpec((mem_len, dh), lambda b, h, i: (b, h)),
            pl.BlockSpec((mem_len, dh), lambda b, h, i: (b, heads + h)),
        ],
        out_specs=pl.BlockSpec((C_TQ, dh), lambda b, h, i: (b * n_qb + i, h)),
        compiler_params=_params("parallel", "parallel", "parallel"),
        name="attn_mem",
    )(proj, ckv, ckv)


def _merge_kernel(x_ref, ya_ref, yb_ref, yc_ref, wga_ref, wgb_ref, wgc_ref,
                  bga_ref, bgb_ref, bgc_ref, pa_ref, pb_ref, pc_ref, o_ref):
    x = x_ref[...]
    acc = None
    for wg, bg, y, p in ((wga_ref, bga_ref, ya_ref, pa_ref),
                         (wgb_ref, bgb_ref, yb_ref, pb_ref),
                         (wgc_ref, bgc_ref, yc_ref, pc_ref)):
        gate = jax.nn.sigmoid(_dot(x, wg[...].astype(BF16)) + bg[...])
        term = gate * _dot(y[...], p[...].astype(BF16))
        acc = term if acc is None else acc + term
    o_ref[...] = acc.astype(o_ref.dtype)


def _gated_merge(xb, ya, yb, yc, w_gates, b_gates, pa, pb, pc, *, tm, tn):
    n, d = xb.shape
    nj = d // tn
    row = lambda width: pl.BlockSpec((tm, width), lambda j, i: (i, 0))
    gate_w = lambda br: pl.BlockSpec((d, tn), lambda j, i, br=br: (0, br * nj + j))
    gate_b = lambda br: pl.BlockSpec((1, tn), lambda j, i, br=br: (0, br * nj + j))
    branch_w = lambda width: pl.BlockSpec((width, tn), lambda j, i: (0, j))
    return pl.pallas_call(
        _merge_kernel,
        out_shape=jax.ShapeDtypeStruct((n, d), BF16),
        grid=(nj, n // tm),
        in_specs=[row(d), row(ya.shape[1]), row(yb.shape[1]), row(yc.shape[1]),
                  gate_w(0), gate_w(1), gate_w(2), gate_b(0), gate_b(1), gate_b(2),
                  branch_w(pa.shape[0]), branch_w(pb.shape[0]), branch_w(pc.shape[0])],
        out_specs=pl.BlockSpec((tm, tn), lambda j, i: (i, j)),
        compiler_params=_params("parallel", "parallel"),
        name="gated_merge",
    )(xb, ya, yb, yc, w_gates, w_gates, w_gates, b_gates, b_gates, b_gates, pa, pb, pc)


def _layer_norm(z, g, b):
    mu = jnp.mean(z, axis=-1, keepdims=True)
    zc = z - mu
    var = jnp.mean(zc * zc, axis=-1, keepdims=True)
    return zc * lax.rsqrt(var + LN_EPS) * g + b


def _ln_router_kernel(x_ref, m_ref, g_ref, b_ref, wr_ref, br_ref,
                      h_ref, hp_ref, idx_ref, gate_ref, rank_ref, cnt_ref, carry_ref,
                      *, alpha, n_exp):
    @pl.when(pl.program_id(0) == 0)
    def _():
        carry_ref[...] = jnp.zeros_like(carry_ref)

    tm = x_ref.shape[0]
    h = _layer_norm(alpha * x_ref[...] + m_ref[...], g_ref[...], b_ref[...])
    h_ref[...] = h
    hp_ref[...] = _pack_halves(h)

    h_hi = h.astype(BF16)
    h_lo = (h - h_hi.astype(F32)).astype(BF16)
    w = wr_ref[...]
    r1 = _dot(h_hi, w)
    logits = r1[:, :LANES] + r1[:, LANES:] + _dot(h_lo, w[:, :LANES]) + br_ref[...]

    lane = lax.broadcasted_iota(I32, (tm, LANES), 1)
    lane_f = lane.astype(F32)
    cur = jnp.where(lane < n_exp, logits, -jnp.inf)
    vals, idxs = [], []
    for _ in range(TOP_K):
        mx = jnp.max(cur, axis=-1, keepdims=True)
        ix = jnp.min(jnp.where(cur == mx, lane_f, float(LANES)), axis=-1, keepdims=True).astype(I32)
        vals.append(mx)
        idxs.append(ix)
        cur = jnp.where(lane == ix, -jnp.inf, cur)
    exps = [jnp.exp(v - vals[0]) for v in vals]
    den = exps[0]
    for e in exps[1:]:
        den = den + e

    tri = (lax.broadcasted_iota(I32, (tm, tm), 0) > lax.broadcasted_iota(I32, (tm, tm), 1)).astype(BF16)
    carry = carry_ref[...]
    idx_out = jnp.zeros((tm, LANES), I32)
    gate_out = jnp.zeros((tm, LANES), F32)
    rank_out = jnp.zeros((tm, LANES), I32)
    for k in range(TOP_K):
        onehot = (lane == idxs[k]).astype(F32)
        before = _dot(tri, onehot.astype(BF16)) + carry
        rank = jnp.sum(onehot * before, axis=-1, keepdims=True)
        carry = carry + jnp.sum(onehot, axis=0, keepdims=True)
        idx_out = jnp.where(lane == k, idxs[k], idx_out)
        gate_out = jnp.where(lane == k, exps[k] / den, gate_out)
        rank_out = jnp.where(lane == k, rank.astype(I32), rank_out)
    carry_ref[...] = carry
    idx_ref[...] = idx_out
    gate_ref[...] = gate_out
    rank_ref[...] = rank_out
    cnt_ref[...] = carry


def _ln_router(x, m, g, b, wr_split, br_pad, *, alpha, n_exp, tm):
    n, d = x.shape
    row = pl.BlockSpec((tm, d), lambda i: (i, 0))
    vec = pl.BlockSpec((1, d), lambda i: (0, 0))
    small = pl.BlockSpec((tm, LANES), lambda i: (i, 0))
    kern = functools.partial(_ln_router_kernel, alpha=alpha, n_exp=n_exp)
    return pl.pallas_call(
        kern,
        out_shape=(jax.ShapeDtypeStruct((n, d), F32),
                   jax.ShapeDtypeStruct((n, d // 2), U32),
                   jax.ShapeDtypeStruct((n, LANES), I32),
                   jax.ShapeDtypeStruct((n, LANES), F32),
                   jax.ShapeDtypeStruct((n, LANES), I32),
                   jax.ShapeDtypeStruct((1, LANES), F32)),
        grid=(n // tm,),
        in_specs=[row, row, vec, vec,
                  pl.BlockSpec((d, 2 * LANES), lambda i: (0, 0)),
                  pl.BlockSpec((1, LANES), lambda i: (0, 0))],
        out_specs=(row, pl.BlockSpec((tm, d // 2), lambda i: (i, 0)), small, small, small,
                   pl.BlockSpec((1, LANES), lambda i: (0, 0))),
        scratch_shapes=[pltpu.VMEM((1, LANES), F32)],
        compiler_params=_params("arbitrary"),
        name="ln_router",
    )(x, m, g, b, wr_split, br_pad)


def _row_copy(src, src_row, dst, dst_row, sem):
    return pltpu.make_async_copy(src.at[pl.ds(src_row, 1)], dst.at[pl.ds(dst_row, 1)], sem)


def _dispatch_kernel(dest_ref, hp_ref, xg_ref, sem):
    tm = hp_ref.shape[0]

    def issue(t, carry):
        for k in range(TOP_K):
            _row_copy(hp_ref, t, xg_ref, dest_ref[0, t * TOP_K + k], sem).start()
        return carry

    lax.fori_loop(0, tm, issue, 0)

    def drain(t, carry):
        for k in range(TOP_K):
            _row_copy(hp_ref, 0, xg_ref, 0, sem).wait()
        return carry

    lax.fori_loop(0, tm, drain, 0)


def _dispatch(dest, hp, *, tm):
    n, w = hp.shape
    dest3 = dest.reshape(n // tm, 1, tm * TOP_K)
    return pl.pallas_call(
        _dispatch_kernel,
        out_shape=jax.ShapeDtypeStruct((n * TOP_K, w), U32),
        grid=(n // tm,),
        in_specs=[pl.BlockSpec((None, 1, tm * TOP_K), lambda i: (i, 0, 0), memory_space=pltpu.SMEM),
                  pl.BlockSpec((tm, w), lambda i: (i, 0))],
        out_specs=pl.BlockSpec(memory_space=pl.ANY),
        scratch_shapes=[pltpu.SemaphoreType.DMA(())],
        compiler_params=_params("arbitrary"),
        name="dispatch",
    )(dest3, hp)


def _cumsum_small(x):
    n = x.shape[0]
    keep = jnp.arange(n)[:, None] >= jnp.arange(n)[None, :]
    return jnp.sum(jnp.where(keep, x[None, :], 0), axis=1).astype(x.dtype)


def _lookup(table, idx):
    hit = idx[..., None] == jnp.arange(table.shape[0], dtype=idx.dtype)
    return jnp.sum(jnp.where(hit, table, 0), axis=-1).astype(table.dtype)


def _visit_schedule(counts, n_rows, tr):
    n_exp = counts.shape[0]
    n_tiles = n_rows // tr
    n_vis = n_tiles + n_exp
    gend = _cumsum_small(counts)
    gstart = gend - counts
    first_tile = gstart // tr
    last_tile = jnp.maximum(gend - 1, 0) // tr
    nvis = jnp.where(counts > 0, last_tile - first_tile + 1, 0)
    vend = _cumsum_small(nvis)
    vstart = vend - nvis
    total = vend[-1]
    v = jnp.arange(n_vis, dtype=I32)
    vc = jnp.minimum(v, total - 1)
    e_v = jnp.minimum(jnp.sum((vend[None, :] <= vc[:, None]).astype(I32), axis=1), n_exp - 1)
    tile_v = _lookup(first_tile, e_v) + (vc - _lookup(vstart, e_v))
    lo = jnp.clip(_lookup(gstart, e_v) - tile_v * tr, 0, tr)
    hi = jnp.clip(_lookup(gend, e_v) - tile_v * tr, 0, tr)
    live = v < total
    lo = jnp.where(live, lo, 0).astype(I32)
    hi = jnp.where(live, hi, 0).astype(I32)
    return tile_v, e_v, lo, hi


def _visit_state(vt, vlo, vhi):
    v = pl.program_id(1)
    lo = vlo[v]
    hi = vhi[v]
    first = jnp.logical_or(v == 0, vt[v] != vt[jnp.maximum(v - 1, 0)])
    return lo, hi, first


def _store_rows(o_ref, val, lo, hi, first):
    rows = lax.broadcasted_iota(I32, (o_ref.shape[0], 1), 0)
    mine = (rows >= lo) & (rows < hi)

    @pl.when(first)
    def _():
        o_ref[...] = jnp.where(mine, val, jnp.zeros_like(val))

    @pl.when(jnp.logical_not(first))
    def _():
        o_ref[...] = jnp.where(mine, val, o_ref[...])


SEL_W = 512


def _even_lane_selector():
    r = jnp.arange(SEL_W)[:, None]
    c = jnp.arange(SEL_W // 2)[None, :]
    return (r == 2 * c).astype(BF16)


def _up_kernel(vt, ve, vlo, vhi, xg_ref, w1_ref, b1_ref, sel_ref, o_ref):
    lo, hi, first = _visit_state(vt, vlo, vhi)

    @pl.when(hi > lo)
    def _():
        half = w1_ref.shape[0] // 2
        tn = w1_ref.shape[1]
        xa, xb = _unpack_halves(xg_ref[...])
        h = (_dot(xa.astype(BF16), w1_ref[:half, :].astype(BF16))
             + _dot(xb.astype(BF16), w1_ref[half:, :].astype(BF16)) + b1_ref[...])
        glu = jnp.minimum(h, SWIGLU_LIMIT)
        lin = jnp.clip(h, -SWIGLU_LIMIT, SWIGLU_LIMIT) + 1.0
        gact = glu * jax.nn.sigmoid(SWIGLU_ALPHA * glu)
        parts = []
        for c in range(tn // LANES):
            cs = slice(c * LANES, (c + 1) * LANES)
            parts.append(gact[:, cs] * pltpu.roll(lin[:, cs], LANES - 1, 1))
        inter = jnp.concatenate(parts, axis=1).astype(BF16)
        acts = [_dot(inter[:, s * SEL_W:(s + 1) * SEL_W], sel_ref[...]) for s in range(tn // SEL_W)]
        act = jnp.concatenate(acts, axis=1).astype(o_ref.dtype)
        _store_rows(o_ref, act, lo, hi, first)


def _expert_up(sched, xg, w1, b1, *, tr, tn):
    p_rows, w = xg.shape
    n_exp, d, f2 = w1.shape
    n_vis = sched[0].shape[0]
    grid_spec = pltpu.PrefetchScalarGridSpec(
        num_scalar_prefetch=4,
        grid=(f2 // tn, n_vis),
        in_specs=[
            pl.BlockSpec((tr, w), lambda c, v, vt, ve, vlo, vhi: (vt[v], 0)),
            pl.BlockSpec((None, d, tn), lambda c, v, vt, ve, vlo, vhi: (ve[v], 0, c)),
            pl.BlockSpec((None, 1, tn), lambda c, v, vt, ve, vlo, vhi: (ve[v], 0, c)),
            pl.BlockSpec((SEL_W, SEL_W // 2), lambda c, v, vt, ve, vlo, vhi: (0, 0)),
        ],
        out_specs=pl.BlockSpec((tr, tn // 2), lambda c, v, vt, ve, vlo, vhi: (vt[v], c)),
    )
    return pl.pallas_call(
        _up_kernel,
        out_shape=jax.ShapeDtypeStruct((p_rows, f2 // 2), BF16),
        grid_spec=grid_spec,
        compiler_params=_params("arbitrary", "arbitrary"),
        name="expert_up",
    )(*sched, xg, w1, b1.reshape(n_exp, 1, f2), _even_lane_selector())


def _down_kernel(vt, ve, vlo, vhi, act_ref, w2_ref, b2_ref, o_ref):
    lo, hi, first = _visit_state(vt, vlo, vhi)

    @pl.when(hi > lo)
    def _():
        y = _dot(act_ref[...], w2_ref[...].astype(BF16)) + b2_ref[...]
        _store_rows(o_ref, _pack_halves(y), lo, hi, first)


def _expert_down(sched, act, w2, b2, *, tr, tn):
    p_rows, f = act.shape
    n_exp, _, d = w2.shape
    n_vis = sched[0].shape[0]
    grid_spec = pltpu.PrefetchScalarGridSpec(
        num_scalar_prefetch=4,
        grid=(d // tn, n_vis),
        in_specs=[
            pl.BlockSpec((tr, f), lambda c, v, vt, ve, vlo, vhi: (vt[v], 0)),
            pl.BlockSpec((None, f, tn), lambda c, v, vt, ve, vlo, vhi: (ve[v], 0, c)),
            pl.BlockSpec((None, 1, tn), lambda c, v, vt, ve, vlo, vhi: (ve[v], 0, c)),
        ],
        out_specs=pl.BlockSpec((tr, tn // 2), lambda c, v, vt, ve, vlo, vhi: (vt[v], c)),
    )
    return pl.pallas_call(
        _down_kernel,
        out_shape=jax.ShapeDtypeStruct((p_rows, d // 2), U32),
        grid_spec=grid_spec,
        compiler_params=_params("arbitrary", "arbitrary"),
        name="expert_down",
    )(*sched, act, w2, b2.reshape(n_exp, 1, d))


def _combine_kernel(dest_ref, h_ref, gate_ref, g_ref, b_ref, y_ref, o_ref, buf_ref, sem,
                    *, alpha, chunk):
    tm = h_ref.shape[0]

    def issue(t, carry):
        for k in range(TOP_K):
            pltpu.make_async_copy(y_ref.at[pl.ds(dest_ref[0, t * TOP_K + k], 1)],
                                  buf_ref.at[k, pl.ds(t, 1)], sem).start()
        return carry

    lax.fori_loop(0, tm, issue, 0)

    def drain(t, carry):
        for k in range(TOP_K):
            pltpu.make_async_copy(y_ref.at[pl.ds(0, 1)], buf_ref.at[k, pl.ds(0, 1)], sem).wait()
        return carry

    lax.fori_loop(0, tm, drain, 0)

    gates = gate_ref[...]
    acc_hi = None
    acc_lo = None
    for k in range(TOP_K):
        hi, lo = _unpack_halves(buf_ref[k])
        gk = gates[:, k:k + 1]
        acc_hi = gk * hi if acc_hi is None else acc_hi + gk * hi
        acc_lo = gk * lo if acc_lo is None else acc_lo + gk * lo
    hw = chunk // 2
    pieces = []
    for c in range(acc_hi.shape[1] // hw):
        pieces.append(acc_hi[:, c * hw:(c + 1) * hw])
        pieces.append(acc_lo[:, c * hw:(c + 1) * hw])
    ffn = jnp.concatenate(pieces, axis=1)
    o_ref[...] = _layer_norm(alpha * h_ref[...] + ffn, g_ref[...], b_ref[...]).astype(o_ref.dtype)


def _combine(dest, h, gates, g, b, y, *, alpha, tm, chunk):
    n, d = h.shape
    dest3 = dest.reshape(n // tm, 1, tm * TOP_K)
    kern = functools.partial(_combine_kernel, alpha=alpha, chunk=chunk)
    return pl.pallas_call(
        kern,
        out_shape=jax.ShapeDtypeStruct((n, d), F32),
        grid=(n // tm,),
        in_specs=[pl.BlockSpec((None, 1, tm * TOP_K), lambda i: (i, 0, 0), memory_space=pltpu.SMEM),
                  pl.BlockSpec((tm, d), lambda i: (i, 0)),
                  pl.BlockSpec((tm, LANES), lambda i: (i, 0)),
                  pl.BlockSpec((1, d), lambda i: (0, 0)),
                  pl.BlockSpec((1, d), lambda i: (0, 0)),
                  pl.BlockSpec(memory_space=pl.ANY)],
        out_specs=pl.BlockSpec((tm, d), lambda i: (i, 0)),
        scratch_shapes=[pltpu.VMEM((TOP_K, tm, d // 2), U32), pltpu.SemaphoreType.DMA(())],
        compiler_params=_params("arbitrary"),
        name="combine_ln",
    )(dest3, h, gates, g, b, y)


def _tiles(n_tokens, d_model, d_expert):
    return dict(
        cast_tm=min(512, n_tokens),
        proj=dict(tm=min(1024, n_tokens), tn=512),
        memkv=dict(tm=512, tn=512),
        merge=dict(tm=min(256, n_tokens), tn=256),
        out=dict(tm=min(1024, n_tokens), tn=512),
        ln_tm=min(256, n_tokens),
        dispatch_tm=min(256, n_tokens),
        moe_tr=256,
        up_tn=min(1024, 2 * d_expert),
        down_tn=min(2048, d_model),
        combine_tm=min(128, n_tokens),
    )


def _layer(h, mem2, lw, *, batch, seq, mem_len, lambda_init, alpha):
    n, d = h.shape
    t = _tiles(n, d, lw["w_mlp2"].shape[1])
    a_width = A_HEADS * A_HEAD_DIM
    b_width = B_HEADS * 2 * B_HEAD_DIM
    col = dict(a_q=0, a_k=a_width, a_v=2 * a_width, b_q=3 * a_width, b_k=3 * a_width + b_width,
               b_v=3 * a_width + 2 * b_width, c_q=3 * a_width + 3 * b_width)

    xb = _cast_bf16(h, t["cast_tm"])
    proj = _matmul(xb, lw["w_in"], out_dtype=BF16, name="in_proj", **t["proj"])
    ckv = _matmul(mem2, lw["w_mem_kv"], out_dtype=BF16, name="mem_kv",
                  tm=min(t["memkv"]["tm"], mem2.shape[0]), tn=t["memkv"]["tn"])

    ya = _attention_a(proj, _band_bias_base(lw["rel_bias"]), batch=batch, seq=seq,
                      heads=A_HEADS, dh=A_HEAD_DIM, col_q=col["a_q"], col_k=col["a_k"], col_v=col["a_v"])
    cos, sin_signed = _rope_tables(seq, B_HEAD_DIM)
    lam_vecs = jnp.stack([lw["lambda_q1"], lw["lambda_k1"], lw["lambda_q2"], lw["lambda_k2"]]).astype(F32)
    yb = _attention_b(proj, cos, sin_signed, lam_vecs, lw["diff_norm_g"].reshape(1, -1),
                      batch=batch, seq=seq, heads=B_HEADS, dh=B_HEAD_DIM,
                      col_q=col["b_q"], col_k=col["b_k"], col_v=col["b_v"], lambda_init=lambda_init)
    yc = _attention_c(proj, ckv, batch=batch, seq=seq, mem_len=mem_len, heads=C_HEADS,
                      dh=C_HEAD_DIM, col_q=col["c_q"])

    merged = _gated_merge(xb, ya, yb, yc, lw["w_gates"], lw["b_gates"].reshape(1, -1),
                          lw["w_branch_a"], lw["w_branch_b"], lw["w_branch_c"], **t["merge"])
    mix = _matmul(merged, lw["w_o"], out_dtype=F32, name="out_proj", **t["out"])

    n_exp = lw["w_router"].shape[1]
    wr = jnp.pad(lw["w_router"], ((0, 0), (0, LANES - n_exp)))
    wr_hi = wr.astype(BF16)
    wr_lo = (wr - wr_hi.astype(F32)).astype(BF16)
    br = jnp.pad(lw["b_router"], (0, LANES - n_exp)).reshape(1, LANES)
    h1, h1_packed, top_idx, gates, rank, cnt = _ln_router(
        h, mix, lw["ln1_g"].reshape(1, -1), lw["ln1_b"].reshape(1, -1),
        jnp.concatenate([wr_hi, wr_lo], axis=1), br, alpha=alpha, n_exp=n_exp, tm=t["ln_tm"])

    counts = cnt[0, :n_exp].astype(I32)
    gstart = _cumsum_small(counts) - counts
    dest = _lookup(gstart, top_idx[:, :TOP_K]) + rank[:, :TOP_K]
    sched = _visit_schedule(counts, n * TOP_K, t["moe_tr"])

    xg = _dispatch(dest, h1_packed, tm=t["dispatch_tm"])
    act = _expert_up(sched, xg, lw["w_mlp1"], lw["b_mlp1"], tr=t["moe_tr"], tn=t["up_tn"])
    y = _expert_down(sched, act, lw["w_mlp2"], lw["b_mlp2"], tr=t["moe_tr"], tn=t["down_tn"])
    return _combine(dest, h1, gates, lw["ln2_g"].reshape(1, -1), lw["ln2_b"].reshape(1, -1), y,
                    alpha=alpha, tm=t["combine_tm"], chunk=t["down_tn"])


def kernel(x, mem, w_in, w_mem_kv, rel_bias, lambda_q1, lambda_k1, lambda_q2, lambda_k2, diff_norm_g,
           w_branch_a, w_branch_b, w_branch_c, w_gates, b_gates, w_o, ln1_g, ln1_b, w_router, b_router,
           w_mlp1, b_mlp1, w_mlp2, b_mlp2, ln2_g, ln2_b):
    batch, seq, d = x.shape
    mem_len = mem.shape[1]
    depth = w_in.shape[0]
    alpha = (2 * depth) ** 0.25
    stacked = dict(w_in=w_in, w_mem_kv=w_mem_kv, rel_bias=rel_bias, lambda_q1=lambda_q1,
                   lambda_k1=lambda_k1, lambda_q2=lambda_q2, lambda_k2=lambda_k2, diff_norm_g=diff_norm_g,
                   w_branch_a=w_branch_a, w_branch_b=w_branch_b, w_branch_c=w_branch_c, w_gates=w_gates,
                   b_gates=b_gates, w_o=w_o, ln1_g=ln1_g, ln1_b=ln1_b, w_router=w_router,
                   b_router=b_router, w_mlp1=w_mlp1, b_mlp1=b_mlp1, w_mlp2=w_mlp2, b_mlp2=b_mlp2,
                   ln2_g=ln2_g, ln2_b=ln2_b)
    h = x.reshape(batch * seq, d)
    mem2 = mem.reshape(batch * mem_len, d)
    for l in range(depth):
        lw = {name: w[l] for name, w in stacked.items()}
        lambda_init = 0.8 - 0.6 * math.exp(-0.3 * l)
        h = _layer(h, mem2, lw, batch=batch, seq=seq, mem_len=mem_len, lambda_init=lambda_init, alpha=alpha)
    return h.reshape(batch, seq, d)
```

```python
import functools
import math

import jax
import jax.numpy as jnp
from jax import lax
from jax.experimental import pallas as pl
from jax.experimental.pallas import tpu as pltpu

F32 = jnp.float32
BF16 = jnp.bfloat16
U32 = jnp.uint32
I32 = jnp.int32

CHUNK = 64
LEFT_CHUNKS = 8
MAX_REL = 128
A_HEADS = 16
A_HEAD_DIM = 128
B_HEADS = 4
B_HEAD_DIM = 128
C_HEADS = 4
C_HEAD_DIM = 256
N_BRANCHES = 3
ROPE_THETA = 10000.0
TOP_K = 4
SWIGLU_LIMIT = 7.0
SWIGLU_ALPHA = 1.702
LN_EPS = 1e-5
RMS_EPS = 1e-5
MASK_VALUE = -1e30

V7X_VMEM_BYTES = 64 * 1024 * 1024
V7X_VMEM_LIMIT = V7X_VMEM_BYTES - 8 * 1024 * 1024
LANES = 128

NT_DIMS = (((1,), (1,)), ((), ()))


def _params(*semantics):
    return pltpu.CompilerParams(dimension_semantics=semantics,
                                vmem_limit_bytes=V7X_VMEM_LIMIT)


def _dot(a, b):
    return jnp.dot(a, b, preferred_element_type=F32)


def _pack_halves(x):
    w = x.shape[1] // 2
    hi = lax.bitcast_convert_type(x[:, :w].astype(jnp.bfloat16).astype(F32), U32)
    lo = lax.bitcast_convert_type(x[:, w:].astype(jnp.bfloat16).astype(F32), U32)
    return hi | (lo >> 16)


def _unpack_halves(p):
    hi = lax.bitcast_convert_type(p & jnp.uint32(0xFFFF0000), F32)
    lo = lax.bitcast_convert_type(p << 16, F32)
    return hi, lo


def _cast_kernel(x_ref, o_ref):
    o_ref[...] = x_ref[...].astype(o_ref.dtype)


def _cast_bf16(x, tm):
    m, d = x.shape
    return pl.pallas_call(
        _cast_kernel,
        out_shape=jax.ShapeDtypeStruct((m, d), BF16),
        grid=(m // tm,),
        in_specs=[pl.BlockSpec((tm, d), lambda i: (i, 0))],
        out_specs=pl.BlockSpec((tm, d), lambda i: (i, 0)),
        compiler_params=_params("parallel"),
        name="cast_bf16",
    )(x)


def _mm_kernel(a_ref, w_ref, o_ref):
    a = a_ref[...].astype(BF16)
    o_ref[...] = _dot(a, w_ref[...].astype(BF16)).astype(o_ref.dtype)


def _matmul(a, w, *, tm, tn, out_dtype, name):
    m, k = a.shape
    n = w.shape[1]
    return pl.pallas_call(
        _mm_kernel,
        out_shape=jax.ShapeDtypeStruct((m, n), out_dtype),
        grid=(n // tn, m // tm),
        in_specs=[pl.BlockSpec((tm, k), lambda j, i: (i, 0)),
                  pl.BlockSpec((k, tn), lambda j, i: (0, j))],
        out_specs=pl.BlockSpec((tm, tn), lambda j, i: (i, j)),
        compiler_params=_params("parallel", "parallel"),
        name=name,
    )(a, w)


A_TQ = 2 * CHUNK
A_WIN = (LEFT_CHUNKS + 2) * CHUNK
A_VARIANTS = LEFT_CHUNKS * CHUNK // A_TQ + 1


A_BASE_W = A_WIN + A_TQ


def _band_bias_base(rel_bias):
    reach = A_WIN
    ext =jnp.pad(rel_bias.astype(F32), ((0, 0), (reach - MAX_REL, reach - MAX_REL)), mode="edge")
    rev = ext[:, ::-1]
    rows = []
    for v in range(A_VARIANTS):
        c = rev[:, reach - A_TQ * v - A_TQ: reach - A_TQ * v + A_WIN]
        rows.append(jnp.concatenate([c[:, A_TQ:], c[:, :A_TQ]], axis=1))
    return jnp.stack(rows)[:, :, None, :]


def _attn_a_kernel(q_ref, k_ref, v_ref, base_ref, o_ref, tb_ref, *, heads, dh, scale):
    i = pl.program_id(2)

    @pl.when(i == 0)
    def _():
        r = lax.broadcasted_iota(I32, (A_TQ, A_WIN), 0)
        j = lax.broadcasted_iota(I32, (A_TQ, A_WIN), 1)
        for var in range(A_VARIANTS):
            cdiff = (A_TQ * var + r) // CHUNK - j // CHUNK
            valid = (cdiff >= 0) & (cdiff <= LEFT_CHUNKS)
            for h in range(heads):
                rows = jnp.broadcast_to(base_ref[var, h], (A_TQ, A_BASE_W))
                toeplitz = pltpu.roll(rows, 0, 1, stride=1, stride_axis=0)[:, :A_WIN]
                tb_ref[var, h] = jnp.where(valid, toeplitz, MASK_VALUE)

    var = jnp.minimum(i, A_VARIANTS - 1)
    start = pl.multiple_of(jnp.maximum(i - (A_VARIANTS - 1), 0) * A_TQ, A_TQ)
    for h in range(heads):
        cs = slice(h * dh, (h + 1) * dh)
        q = q_ref[:, cs]
        k = k_ref[pl.ds(start, A_WIN), cs]
        v = v_ref[pl.ds(start, A_WIN), cs]
        s = lax.dot_general(q, k, NT_DIMS, preferred_element_type=F32) * scale + tb_ref[var, h]
        m = jnp.max(s, axis=-1, keepdims=True)
        p = jnp.exp(s - m)
        l = jnp.sum(p, axis=-1, keepdims=True)
        o = _dot(p.astype(BF16), v)
        o_ref[:, cs] = (o / l).astype(o_ref.dtype)


def _attention_a(proj, base, *, batch, seq, heads, dh, col_q, col_k, col_v, heads_per_step=8):
    n = proj.shape[0]
    gw = heads_per_step * dh
    n_groups = heads // heads_per_step
    n_qb = seq // A_TQ
    kern = functools.partial(_attn_a_kernel, heads=heads_per_step, dh=dh, scale=dh ** -0.5)
    return pl.pallas_call(
        kern,
        out_shape=jax.ShapeDtypeStruct((n, heads * dh), BF16),
        grid=(batch, n_groups, n_qb),
        in_specs=[
            pl.BlockSpec((A_TQ, gw), lambda b, g, i: (b * n_qb + i, col_q // gw + g)),
            pl.BlockSpec((seq, gw), lambda b, g, i: (b, col_k // gw + g)),
            pl.BlockSpec((seq, gw), lambda b, g, i: (b, col_v // gw + g)),
            pl.BlockSpec((A_VARIANTS, heads_per_step, 1, A_BASE_W), lambda b, g, i: (0, g, 0, 0)),
        ],
        out_specs=pl.BlockSpec((A_TQ, gw), lambda b, g, i: (b * n_qb + i, g)),
        scratch_shapes=[pltpu.VMEM((A_VARIANTS, heads_per_step, A_TQ, A_WIN), F32)],
        compiler_params=_params("parallel", "parallel", "arbitrary"),
        name="attn_band",
    )(proj, proj, proj, base)


B_TQ = 256


def _rope_tables(seq, dim):
    inv = 1.0 / (ROPE_THETA ** (jnp.arange(0, dim, 2, dtype=F32) / dim))
    ang = jnp.arange(seq, dtype=F32)[:, None] * inv[None, :]
    ang = jnp.concatenate([ang, ang], -1)
    sign = jnp.where(jnp.arange(dim) < dim // 2, -1.0, 1.0).astype(F32)
    return jnp.cos(ang), jnp.sin(ang) * sign[None, :]


def _rope(x, cos, sin_signed):
    return x * cos + pltpu.roll(x, x.shape[1] // 2, 1) * sin_signed


def _attn_b_kernel(q_ref, k_ref, v_ref, cos_ref, sin_ref, lam_ref, g_ref, o_ref, krot_ref,
                   *, dh, scale, lambda_init):
    qi = pl.program_id(2)
    seq = k_ref.shape[0]

    @pl.when(qi == 0)
    def _():
        for m in range(2):
            kf = k_ref[:, m * dh:(m + 1) * dh].astype(F32)
            krot_ref[m] = _rope(kf, cos_ref[...], sin_ref[...]).astype(BF16)

    lv = lam_ref[...]
    lam = (jnp.exp(jnp.sum(lv[0:1] * lv[1:2], axis=-1, keepdims=True))
           - jnp.exp(jnp.sum(lv[2:3] * lv[3:4], axis=-1, keepdims=True)) + lambda_init)

    def block(blk):
        row0 = blk * B_TQ
        kl = row0 + B_TQ
        cos_q = cos_ref[row0:kl, :]
        sin_q = sin_ref[row0:kl, :]
        q_chunk = (row0 + lax.broadcasted_iota(I32, (B_TQ, kl), 0)) // CHUNK
        k_chunk = lax.broadcasted_iota(I32, (B_TQ, kl), 1) // CHUNK
        allowed = k_chunk <= q_chunk
        probs = []
        for m in range(2):
            qf = q_ref[:, m * dh:(m + 1) * dh].astype(F32)
            qr = _rope(qf, cos_q, sin_q).astype(BF16)
            s = lax.dot_general(qr, krot_ref[m, :kl, :], NT_DIMS, preferred_element_type=F32) * scale
            s = jnp.where(allowed, s, MASK_VALUE)
            e = jnp.exp(s - jnp.max(s, axis=-1, keepdims=True))
            probs.append(e / jnp.sum(e, axis=-1, keepdims=True))
        w = (probs[0] - lam * probs[1]).astype(BF16)
        o = _dot(w, v_ref[:kl, :])
        ms = jnp.mean(o * o, axis=-1, keepdims=True)
        y = o * lax.rsqrt(ms + RMS_EPS) * g_ref[...] * (1.0 - lambda_init)
        o_ref[...] = y.astype(o_ref.dtype)

    for blk in range(seq // B_TQ):
        pl.when(qi == blk)(functools.partial(block, blk))


def _attention_b(proj, cos, sin_signed, lam_vecs, norm_g, *, batch, seq, heads, dh,
                 col_q, col_k, col_v, lambda_init):
    n = proj.shape[0]
    hw = 2 * dh
    n_qb = seq // B_TQ
    kern = functools.partial(_attn_b_kernel, dh=dh, scale=dh ** -0.5, lambda_init=lambda_init)
    return pl.pallas_call(
        kern,
        out_shape=jax.ShapeDtypeStruct((n, heads * hw), BF16),
        grid=(batch, heads, n_qb),
        in_specs=[
            pl.BlockSpec((B_TQ, hw), lambda b, h, i: (b * n_qb + i, col_q // hw + h)),
            pl.BlockSpec((seq, hw), lambda b, h, i: (b, col_k // hw + h)),
            pl.BlockSpec((seq, hw), lambda b, h, i: (b, col_v // hw + h)),
            pl.BlockSpec((seq, dh), lambda b, h, i: (0, 0)),
            pl.BlockSpec((seq, dh), lambda b, h, i: (0, 0)),
            pl.BlockSpec((4, dh), lambda b, h, i: (0, 0)),
            pl.BlockSpec((1, hw), lambda b, h, i: (0, 0)),
        ],
        out_specs=pl.BlockSpec((B_TQ, hw), lambda b, h, i: (b * n_qb + i, h)),
        scratch_shapes=[pltpu.VMEM((2, seq, dh), BF16)],
        compiler_params=_params("parallel", "parallel", "arbitrary"),
        name="attn_diff",
    )(proj, proj, proj, cos, sin_signed, lam_vecs, norm_g)


C_TQ = 512


def _attn_c_kernel(q_ref, k_ref, v_ref, o_ref, *, scale):
    s = lax.dot_general(q_ref[...], k_ref[...], NT_DIMS, preferred_element_type=F32) * scale
    e = jnp.exp(s - jnp.max(s, axis=-1, keepdims=True))
    p = (e / jnp.sum(e, axis=-1, keepdims=True)).astype(BF16)
    o_ref[...] = _dot(p, v_ref[...]).astype(o_ref.dtype)


def _attention_c(proj, ckv, *, batch, seq, mem_len, heads, dh, col_q):
    n = proj.shape[0]
    n_qb = seq // C_TQ
    kern = functools.partial(_attn_c_kernel, scale=dh ** -0.5)
    return pl.pallas_call(
        kern,
        out_shape=jax.ShapeDtypeStruct((n, heads * dh), BF16),
        grid=(batch, heads, n_qb),
        in_specs=[
            pl.BlockSpec((C_TQ, dh), lambda b, h, i: (b * n_qb + i, col_q // dh + h)),
            pl.BlockSpec((mem_len, dh), lambda b, h, i: (b, h)),
            pl.BlockSpec((mem_len, dh), lambda b, h, i: (b, heads + h)),
        ],
        out_specs=pl.BlockSpec((C_TQ, dh), lambda b, h, i: (b * n_qb + i, h)),
        compiler_params=_params("parallel", "parallel", "parallel"),
        name="attn_mem",
    )(proj, ckv, ckv)


def _merge_kernel(x_ref, ya_ref, yb_ref, yc_ref, wga_ref, wgb_ref, wgc_ref,
                  bga_ref, bgb_ref, bgc_ref, pa_ref, pb_ref, pc_ref, o_ref):
    x = x_ref[...]
    acc = None
    for wg, bg, y, p in ((wga_ref, bga_ref, ya_ref, pa_ref),
                         (wgb_ref, bgb_ref, yb_ref, pb_ref),
                         (wgc_ref, bgc_ref, yc_ref, pc_ref)):
        gate = jax.nn.sigmoid(_dot(x, wg[...].astype(BF16)) + bg[...])
        term = gate * _dot(y[...], p[...].astype(BF16))
        acc = term if acc is None else acc + term
    o_ref[...] = acc.astype(o_ref.dtype)


def _gated_merge(xb, ya, yb, yc, w_gates, b_gates, pa, pb, pc, *, tm, tn):
    n, d = xb.shape
    nj = d // tn
    row = lambda width: pl.BlockSpec((tm, width), lambda j, i: (i, 0))
    gate_w = lambda br: pl.BlockSpec((d, tn), lambda j, i, br=br: (0, br * nj + j))
    gate_b = lambda br: pl.BlockSpec((1, tn), lambda j, i, br=br: (0, br * nj + j))
    branch_w = lambda width: pl.BlockSpec((width, tn), lambda j, i: (0, j))
    return pl.pallas_call(
        _merge_kernel,
        out_shape=jax.ShapeDtypeStruct((n, d), BF16),
        grid=(nj, n // tm),
        in_specs=[row(d), row(ya.shape[1]), row(yb.shape[1]), row(yc.shape[1]),
                  gate_w(0), gate_w(1), gate_w(2), gate_b(0), gate_b(1), gate_b(2),
                  branch_w(pa.shape[0]), branch_w(pb.shape[0]), branch_w(pc.shape[0])],
        out_specs=pl.BlockSpec((tm, tn), lambda j, i: (i, j)),
        compiler_params=_params("parallel", "parallel"),
        name="gated_merge",
    )(xb, ya, yb, yc, w_gates, w_gates, w_gates, b_gates, b_gates, b_gates, pa, pb, pc)


def _layer_norm(z, g, b):
    mu = jnp.mean(z, axis=-1, keepdims=True)
    zc = z - mu
    var = jnp.mean(zc * zc, axis=-1, keepdims=True)
    return zc * lax.rsqrt(var + LN_EPS) * g + b


def _ln_router_kernel(x_ref, m_ref, g_ref, b_ref, wr_ref, br_ref,
                      h_ref, hp_ref, idx_ref, gate_ref, rank_ref, cnt_ref, carry_ref,
                      *, alpha, n_exp):
    @pl.when(pl.program_id(0) == 0)
    def _():
        carry_ref[...] = jnp.zeros_like(carry_ref)

    tm = x_ref.shape[0]
    h = _layer_norm(alpha * x_ref[...] + m_ref[...], g_ref[...], b_ref[...])
    h_ref[...] = h
    hp_ref[...] = _pack_halves(h)

    h_hi = h.astype(BF16)
    h_lo = (h - h_hi.astype(F32)).astype(BF16)
    w = wr_ref[...]
    r1 = _dot(h_hi, w)
    logits = r1[:, :LANES] + r1[:, LANES:] + _dot(h_lo, w[:, :LANES]) + br_ref[...]

    lane = lax.broadcasted_iota(I32, (tm, LANES), 1)
    lane_f = lane.astype(F32)
    cur = jnp.where(lane < n_exp, logits, -jnp.inf)
    vals, idxs = [], []
    for _ in range(TOP_K):
        mx = jnp.max(cur, axis=-1, keepdims=True)
        ix = jnp.min(jnp.where(cur == mx, lane_f, float(LANES)), axis=-1, keepdims=True).astype(I32)
        vals.append(mx)
        idxs.append(ix)
        cur = jnp.where(lane == ix, -jnp.inf, cur)
    exps = [jnp.exp(v - vals[0]) for v in vals]
    den = exps[0]
    for e in exps[1:]:
        den = den + e

    tri = (lax.broadcasted_iota(I32, (tm, tm), 0) > lax.broadcasted_iota(I32, (tm, tm), 1)).astype(BF16)
    carry = carry_ref[...]
    idx_out = jnp.zeros((tm, LANES), I32)
    gate_out = jnp.zeros((tm, LANES), F32)
    rank_out = jnp.zeros((tm, LANES), I32)
    for k in range(TOP_K):
        onehot = (lane == idxs[k]).astype(F32)
        before = _dot(tri, onehot.astype(BF16)) + carry
        rank = jnp.sum(onehot * before, axis=-1, keepdims=True)
        carry = carry + jnp.sum(onehot, axis=0, keepdims=True)
        idx_out = jnp.where(lane == k, idxs[k], idx_out)
        gate_out = jnp.where(lane == k, exps[k] / den, gate_out)
        rank_out = jnp.where(lane == k, rank.astype(I32), rank_out)
    carry_ref[...] = carry
    idx_ref[...] = idx_out
    gate_ref[...] = gate_out
    rank_ref[...] = rank_out
    cnt_ref[...] = carry


def _ln_router(x, m, g, b, wr_split, br_pad, *, alpha, n_exp, tm):
    n, d = x.shape
    row = pl.BlockSpec((tm, d), lambda i: (i, 0))
    vec = pl.BlockSpec((1, d), lambda i: (0, 0))
    small = pl.BlockSpec((tm, LANES), lambda i: (i, 0))
    kern = functools.partial(_ln_router_kernel, alpha=alpha, n_exp=n_exp)
    return pl.pallas_call(
        kern,
        out_shape=(jax.ShapeDtypeStruct((n, d), F32),
                   jax.ShapeDtypeStruct((n, d // 2), U32),
                   jax.ShapeDtypeStruct((n, LANES), I32),
                   jax.ShapeDtypeStruct((n, LANES), F32),
                   jax.ShapeDtypeStruct((n, LANES), I32),
                   jax.ShapeDtypeStruct((1, LANES), F32)),
        grid=(n // tm,),
        in_specs=[row, row, vec, vec,
                  pl.BlockSpec((d, 2 * LANES), lambda i: (0, 0)),
                  pl.BlockSpec((1, LANES), lambda i: (0, 0))],
        out_specs=(row, pl.BlockSpec((tm, d // 2), lambda i: (i, 0)), small, small, small,
                   pl.BlockSpec((1, LANES), lambda i: (0, 0))),
        scratch_shapes=[pltpu.VMEM((1, LANES), F32)],
        compiler_params=_params("arbitrary"),
        name="ln_router",
    )(x, m, g, b, wr_split, br_pad)


MOE_RB = 256


def _cumsum_small(x):
    n = x.shape[0]
    keep = jnp.arange(n)[:, None] >= jnp.arange(n)[None, :]
    return jnp.sum(jnp.where(keep, x[None, :], 0), axis=1).astype(x.dtype)


def _lookup(table, idx):
    hit = idx[..., None] == jnp.arange(table.shape[0], dtype=idx.dtype)
    return jnp.sum(jnp.where(hit, table, 0), axis=-1).astype(table.dtype)


def _group_layout(counts, cap):
    padded = (counts + MOE_RB - 1) // MOE_RB * MOE_RB
    pend = _cumsum_small(padded)
    pstart = pend - padded
    tail = jnp.stack([pend[-1], (cap - pend[-1]) // MOE_RB])
    return pstart, padded // MOE_RB, pstart + counts, padded - counts, tail


def _tail_rows(tail, b):
    return pl.ds(pl.multiple_of(tail[0] + b * MOE_RB, MOE_RB), MOE_RB)


def _zero_tail(tail, zero_view, dst_view, sem):
    def start(b, carry):
        pltpu.make_async_copy(zero_view, dst_view(_tail_rows(tail, b)), sem).start()
        return carry

    lax.fori_loop(0, tail[1], start, 0)

    def done(b, carry):
        pltpu.make_async_copy(zero_view, dst_view(_tail_rows(tail, 0)), sem).wait()
        return carry

    lax.fori_loop(0, tail[1], done, 0)


def _row_copy(src, src_row, dst, dst_row, sem):
    return pltpu.make_async_copy(src.at[pl.ds(src_row, 1)], dst.at[pl.ds(dst_row, 1)], sem)


def _dispatch_kernel(pad_lo, pad_n, tail, dest_ref, hp_ref, xg_ref, zero_ref, sem, zsem):
    tm = hp_ref.shape[0]
    n_exp = pad_lo.shape[0]

    @pl.when(pl.program_id(0) == 0)
    def _():
        zero_ref[...] = jnp.zeros_like(zero_ref)
        _zero_tail(tail, zero_ref, lambda rows: xg_ref.at[rows], zsem)

        def fill(e, carry):
            def one(r, c):
                _row_copy(zero_ref, 0, xg_ref, pad_lo[e] + r, zsem).start()
                return c
            return lax.fori_loop(0, pad_n[e], one, carry)

        lax.fori_loop(0, n_exp, fill, 0)

        def fill_done(e, carry):
            def one(r, c):
                _row_copy(zero_ref, 0, xg_ref, 0, zsem).wait()
                return c
            return lax.fori_loop(0, pad_n[e], one, carry)

        lax.fori_loop(0, n_exp, fill_done, 0)

    def issue(t, carry):
        for k in range(TOP_K):
            _row_copy(hp_ref, t, xg_ref, dest_ref[0, t * TOP_K + k], sem).start()
        return carry

    lax.fori_loop(0, tm, issue, 0)

    def drain(t, carry):
        for k in range(TOP_K):
            _row_copy(hp_ref, 0, xg_ref, 0, sem).wait()
        return carry

    lax.fori_loop(0, tm, drain, 0)


def _dispatch(dest, hp, pad_lo, pad_n, tail, *, tm, cap):
    n, w = hp.shape
    dest3 = dest.reshape(n // tm, 1, tm * TOP_K)
    grid_spec = pltpu.PrefetchScalarGridSpec(
        num_scalar_prefetch=3,
        grid=(n // tm,),
        in_specs=[pl.BlockSpec((None, 1, tm * TOP_K), lambda i, *_: (i, 0, 0), memory_space=pltpu.SMEM),
                  pl.BlockSpec((tm, w), lambda i, *_: (i, 0))],
        out_specs=pl.BlockSpec(memory_space=pl.ANY),
        scratch_shapes=[pltpu.VMEM((MOE_RB, w), U32), pltpu.SemaphoreType.DMA(()), pltpu.SemaphoreType.DMA(())],
    )
    return pl.pallas_call(
        _dispatch_kernel,
        out_shape=jax.ShapeDtypeStruct((cap, w), U32),
        grid_spec=grid_spec,
        compiler_params=_params("arbitrary"),
        name="dispatch",
    )(pad_lo, pad_n, tail, dest3, hp)


SEL_W = 512


def _even_lane_selector():
    r = jnp.arange(SEL_W)[:, None]
    c = jnp.arange(SEL_W // 2)[None, :]
    return (r == 2 * c).astype(BF16)


def _swiglu_even_lanes(h, sel):
    tn = h.shape[1]
    glu = jnp.minimum(h, SWIGLU_LIMIT)
    lin = jnp.clip(h, -SWIGLU_LIMIT, SWIGLU_LIMIT) + 1.0
    gact = glu * jax.nn.sigmoid(SWIGLU_ALPHA * glu)
    parts = []
    for c in range(tn // LANES):
        cs = slice(c * LANES, (c + 1) * LANES)
        parts.append(gact[:, cs] * pltpu.roll(lin[:, cs], LANES - 1, 1))
    inter = jnp.concatenate(parts, axis=1).astype(BF16)
    acts = [_dot(inter[:, s * SEL_W:(s + 1) * SEL_W], sel) for s in range(tn // SEL_W)]
    return jnp.concatenate(acts, axis=1)


def _up_kernel(pstart, nblk, tail, xg_ref, w1_ref, b1_ref, sel_ref, act_ref,
               xbuf, hbuf, obuf, wbf, sem_in, sem_out, *, n_chunks):
    c = pl.program_id(0)
    e = pl.program_id(1)
    n = nblk[e]
    base = pstart[e]
    rb = xbuf.shape[1]
    half = w1_ref.shape[0] // 2
    tno = obuf.shape[2]

    def rows(blk):
        return pl.ds(pl.multiple_of(base + blk * rb, rb), rb)

    def in_copy(blk, slot):
        return pltpu.make_async_copy(xg_ref.at[rows(blk)], xbuf.at[slot], sem_in.at[slot])

    def out_start(blk, slot):
        for cc in range(n_chunks):
            @pl.when(c == cc)
            def _(cc=cc):
                pltpu.make_async_copy(obuf.at[slot], act_ref.at[rows(blk), pl.ds(cc * tno, tno)],
                                      sem_out.at[slot]).start()

    def out_wait(slot):
        pltpu.make_async_copy(obuf.at[slot], act_ref.at[pl.ds(0, rb), pl.ds(0, tno)], sem_out.at[slot]).wait()

    def matmul(slot):
        xa, xb = _unpack_halves(xbuf[slot])
        hbuf[slot] = (_dot(xa.astype(BF16), wbf[:half, :]) + _dot(xb.astype(BF16), wbf[half:, :])
                      + b1_ref[...])

    def epilogue(slot):
        obuf[slot] = _swiglu_even_lanes(hbuf[slot], sel_ref[...]).astype(obuf.dtype)

    @pl.when(n > 0)
    def _():
        in_copy(0, 0).start()
        wbf[...] = w1_ref[...].astype(BF16)
        in_copy(0, 0).wait()

        @pl.when(n > 1)
        def _():
            in_copy(1, 1).start()

        matmul(0)

        def body(blk, carry):
            slot = lax.rem(blk, 2)
            prev = 1 - slot
            in_copy(blk, slot).wait()

            @pl.when(blk + 1 < n)
            def _():
                in_copy(blk + 1, prev).start()

            @pl.when(blk >= 3)
            def _():
                out_wait(prev)

            epilogue(prev)
            matmul(slot)
            out_start(blk - 1, prev)
            return carry

        lax.fori_loop(1, n, body, 0)

        last = n - 1
        ls = lax.rem(last, 2)

        @pl.when(last >= 2)
        def _():
            out_wait(ls)

        epilogue(ls)
        out_start(last, ls)

        @pl.when(last >= 1)
        def _():
            out_wait(1 - ls)

        out_wait(ls)

    @pl.when(e == pl.num_programs(1) - 1)
    def _():
        obuf[0] = jnp.zeros(obuf.shape[1:], obuf.dtype)
        for cc in range(n_chunks):
            @pl.when(c == cc)
            def _(cc=cc):
                _zero_tail(tail, obuf.at[0], lambda r: act_ref.at[r, pl.ds(cc * tno, tno)], sem_out.at[0])


def _expert_up(pstart, nblk, tail, xg, w1, b1, *, tn):
    cap, w = xg.shape
    n_exp, d, f2 = w1.shape
    n_chunks = f2 // tn
    grid_spec = pltpu.PrefetchScalarGridSpec(
        num_scalar_prefetch=3,
        grid=(n_chunks, n_exp),
        in_specs=[
            pl.BlockSpec(memory_space=pl.ANY),
            pl.BlockSpec((None, d, tn), lambda c, e, *_: (e, 0, c)),
            pl.BlockSpec((None, 1, tn), lambda c, e, *_: (e, 0, c)),
            pl.BlockSpec((SEL_W, SEL_W // 2), lambda c, e, *_: (0, 0)),
        ],
        out_specs=pl.BlockSpec(memory_space=pl.ANY),
        scratch_shapes=[pltpu.VMEM((2, MOE_RB, w), U32),
                        pltpu.VMEM((2, MOE_RB, tn), F32),
                        pltpu.VMEM((2, MOE_RB, tn // 2), BF16),
                        pltpu.VMEM((d, tn), BF16),
                        pltpu.SemaphoreType.DMA((2,)),
                        pltpu.SemaphoreType.DMA((2,))],
    )
    return pl.pallas_call(
        functools.partial(_up_kernel, n_chunks=n_chunks),
        out_shape=jax.ShapeDtypeStruct((cap, f2 // 2), BF16),
        grid_spec=grid_spec,
        compiler_params=_params("arbitrary", "arbitrary"),
        name="expert_up",
    )(pstart, nblk, tail, xg, w1, b1.reshape(n_exp, 1, f2), _even_lane_selector())


def _down_kernel(pstart, nblk, tail, act_ref, w2_ref, b2_ref, y_ref, abuf, obuf, wbf, sem_in, sem_out,
                 *, n_chunks):
    c = pl.program_id(0)
    e = pl.program_id(1)
    n = nblk[e]
    base = pstart[e]
    rb = abuf.shape[1]
    tno = obuf.shape[2]

    def rows(blk):
        return pl.ds(pl.multiple_of(base + blk * rb, rb), rb)

    def in_copy(blk, slot):
        return pltpu.make_async_copy(act_ref.at[rows(blk)], abuf.at[slot], sem_in.at[slot])

    def out_start(blk, slot):
        for cc in range(n_chunks):
            @pl.when(c == cc)
            def _(cc=cc):
                pltpu.make_async_copy(obuf.at[slot], y_ref.at[rows(blk), pl.ds(cc * tno, tno)],
                                      sem_out.at[slot]).start()

    def out_wait(slot):
        pltpu.make_async_copy(obuf.at[slot], y_ref.at[pl.ds(0, rb), pl.ds(0, tno)], sem_out.at[slot]).wait()

    @pl.when(n > 0)
    def _():
        in_copy(0, 0).start()
        wbf[...] = w2_ref[...].astype(BF16)

        def body(blk, carry):
            slot = lax.rem(blk, 2)
            in_copy(blk, slot).wait()

            @pl.when(blk + 1 < n)
            def _():
                in_copy(blk + 1, 1 - slot).start()

            @pl.when(blk >= 2)
            def _():
                out_wait(slot)

            obuf[slot] = _pack_halves(_dot(abuf[slot], wbf[...]) + b2_ref[...])
            out_start(blk, slot)
            return carry

        lax.fori_loop(0, n, body, 0)

        last = n - 1

        @pl.when(last >= 1)
        def _():
            out_wait(lax.rem(last + 1, 2))

        out_wait(lax.rem(last, 2))

    @pl.when(e == pl.num_programs(1) - 1)
    def _():
        obuf[0] = jnp.zeros(obuf.shape[1:], obuf.dtype)
        for cc in range(n_chunks):
            @pl.when(c == cc)
            def _(cc=cc):
                _zero_tail(tail, obuf.at[0], lambda r: y_ref.at[r, pl.ds(cc * tno, tno)], sem_out.at[0])


def _expert_down(pstart, nblk, tail, act, w2, b2, *, tn):
    cap, f = act.shape
    n_exp, _, d = w2.shape
    n_chunks = d // tn
    grid_spec = pltpu.PrefetchScalarGridSpec(
        num_scalar_prefetch=3,
        grid=(n_chunks, n_exp),
        in_specs=[
            pl.BlockSpec(memory_space=pl.ANY),
            pl.BlockSpec((None, f, tn), lambda c, e, *_: (e, 0, c)),
            pl.BlockSpec((None, 1, tn), lambda c, e, *_: (e, 0, c)),
        ],
        out_specs=pl.BlockSpec(memory_space=pl.ANY),
        scratch_shapes=[pltpu.VMEM((2, MOE_RB, f), BF16),
                        pltpu.VMEM((2, MOE_RB, tn // 2), U32),
                        pltpu.VMEM((f, tn), BF16),
                        pltpu.SemaphoreType.DMA((2,)),
                        pltpu.SemaphoreType.DMA((2,))],
    )
    return pl.pallas_call(
        functools.partial(_down_kernel, n_chunks=n_chunks),
        out_shape=jax.ShapeDtypeStruct((cap, d // 2), U32),
        grid_spec=grid_spec,
        compiler_params=_params("arbitrary", "arbitrary"),
        name="expert_down",
    )(pstart, nblk, tail, act, w2, b2.reshape(n_exp, 1, d))


def _combine_kernel(dest_ref, next_ref, h_ref, gate_ref, g_ref, b_ref, y_ref, o_ref, buf_ref, sem,
                    *, alpha, chunk):
    tm = h_ref.shape[0]
    i = pl.program_id(0)
    slot = lax.rem(i, 2)

    def gather(idx_ref, s):
        def issue(t, carry):
            for k in range(TOP_K):
                pltpu.make_async_copy(y_ref.at[pl.ds(idx_ref[0, t * TOP_K + k], 1)],
                                      buf_ref.at[s, k, pl.ds(t, 1)], sem.at[s]).start()
            return carry
        lax.fori_loop(0, tm, issue, 0)

    @pl.when(i == 0)
    def _():
        gather(dest_ref, 0)

    @pl.when(i + 1 < pl.num_programs(0))
    def _():
        gather(next_ref, 1 - slot)

    def drain(t, carry):
        for k in range(TOP_K):
            pltpu.make_async_copy(y_ref.at[pl.ds(0, 1)], buf_ref.at[slot, k, pl.ds(0, 1)], sem.at[slot]).wait()
        return carry

    lax.fori_loop(0, tm, drain, 0)

    gates = gate_ref[...]
    acc_hi = None
    acc_lo = None
    for k in range(TOP_K):
        hi, lo = _unpack_halves(buf_ref[slot, k])
        gk = gates[:, k:k + 1]
        acc_hi = gk * hi if acc_hi is None else acc_hi + gk * hi
        acc_lo = gk * lo if acc_lo is None else acc_lo + gk * lo
    hw = chunk // 2
    pieces = []
    for c in range(acc_hi.shape[1] // hw):
        pieces.append(acc_hi[:, c * hw:(c + 1) * hw])
        pieces.append(acc_lo[:, c * hw:(c + 1) * hw])
    ffn = jnp.concatenate(pieces, axis=1)
    o_ref[...] = _layer_norm(alpha * h_ref[...] + ffn, g_ref[...], b_ref[...]).astype(o_ref.dtype)


def _combine(dest, h, gates, g, b, y, *, alpha, tm, chunk):
    n, d = h.shape
    n_steps = n // tm
    dest3 = dest.reshape(n_steps, 1, tm * TOP_K)
    kern = functools.partial(_combine_kernel, alpha=alpha, chunk=chunk)
    return pl.pallas_call(
        kern,
        out_shape=jax.ShapeDtypeStruct((n, d), F32),
        grid=(n_steps,),
        in_specs=[pl.BlockSpec((None, 1, tm * TOP_K), lambda i: (i, 0, 0), memory_space=pltpu.SMEM),
                  pl.BlockSpec((None, 1, tm * TOP_K), lambda i: (jnp.minimum(i + 1, n_steps - 1), 0, 0),
                               memory_space=pltpu.SMEM),
                  pl.BlockSpec((tm, d), lambda i: (i, 0)),
                  pl.BlockSpec((tm, LANES), lambda i: (i, 0)),
                  pl.BlockSpec((1, d), lambda i: (0, 0)),
                  pl.BlockSpec((1, d), lambda i: (0, 0)),
                  pl.BlockSpec(memory_space=pl.ANY)],
        out_specs=pl.BlockSpec((tm, d), lambda i: (i, 0)),
        scratch_shapes=[pltpu.VMEM((2, TOP_K, tm, d // 2), U32), pltpu.SemaphoreType.DMA((2,))],
        compiler_params=_params("arbitrary"),
        name="combine_ln",
    )(dest3, dest3, h, gates, g, b, y)


def _tiles(n_tokens, d_model, d_expert):
    return dict(
        cast_tm=min(512, n_tokens),
        proj=dict(tm=min(1024, n_tokens), tn=512),
        memkv=dict(tm=512, tn=512),
        merge=dict(tm=min(512, n_tokens), tn=256),
        out=dict(tm=min(1024, n_tokens), tn=512),
        ln_tm=min(256, n_tokens),
        dispatch_tm=min(256, n_tokens),
        up_tn=min(1024, 2 * d_expert),
        down_tn=min(2048, d_model),
        combine_tm=min(128, n_tokens),
    )


def _layer(h, mem2, lw, *, batch, seq, mem_len, lambda_init, alpha):
    n, d = h.shape
    t = _tiles(n, d, lw["w_mlp2"].shape[1])
    a_width = A_HEADS * A_HEAD_DIM
    b_width = B_HEADS * 2 * B_HEAD_DIM
    col = dict(a_q=0, a_k=a_width, a_v=2 * a_width, b_q=3 * a_width, b_k=3 * a_width + b_width,
               b_v=3 * a_width + 2 * b_width, c_q=3 * a_width + 3 * b_width)

    xb = _cast_bf16(h, t["cast_tm"])
    proj = _matmul(xb, lw["w_in"], out_dtype=BF16, name="in_proj", **t["proj"])
    ckv = _matmul(mem2, lw["w_mem_kv"], out_dtype=BF16, name="mem_kv",
                  tm=min(t["memkv"]["tm"], mem2.shape[0]), tn=t["memkv"]["tn"])

    ya = _attention_a(proj, _band_bias_base(lw["rel_bias"]), batch=batch, seq=seq,
                      heads=A_HEADS, dh=A_HEAD_DIM, col_q=col["a_q"], col_k=col["a_k"], col_v=col["a_v"])
    cos, sin_signed = _rope_tables(seq, B_HEAD_DIM)
    lam_vecs = jnp.stack([lw["lambda_q1"], lw["lambda_k1"], lw["lambda_q2"], lw["lambda_k2"]]).astype(F32)
    yb = _attention_b(proj, cos, sin_signed, lam_vecs, lw["diff_norm_g"].reshape(1, -1),
                      batch=batch, seq=seq, heads=B_HEADS, dh=B_HEAD_DIM,
                      col_q=col["b_q"], col_k=col["b_k"], col_v=col["b_v"], lambda_init=lambda_init)
    yc = _attention_c(proj, ckv, batch=batch, seq=seq, mem_len=mem_len, heads=C_HEADS,
                      dh=C_HEAD_DIM, col_q=col["c_q"])

    merged = _gated_merge(xb, ya, yb, yc, lw["w_gates"], lw["b_gates"].reshape(1, -1),
                          lw["w_branch_a"], lw["w_branch_b"], lw["w_branch_c"], **t["merge"])
    mix = _matmul(merged, lw["w_o"], out_dtype=F32, name="out_proj", **t["out"])

    n_exp = lw["w_router"].shape[1]
    wr = jnp.pad(lw["w_router"], ((0, 0), (0, LANES - n_exp)))
    wr_hi = wr.astype(BF16)
    wr_lo = (wr - wr_hi.astype(F32)).astype(BF16)
    br = jnp.pad(lw["b_router"], (0, LANES - n_exp)).reshape(1, LANES)
    h1, h1_packed, top_idx, gates, rank, cnt = _ln_router(
        h, mix, lw["ln1_g"].reshape(1, -1), lw["ln1_b"].reshape(1, -1),
        jnp.concatenate([wr_hi, wr_lo], axis=1), br, alpha=alpha, n_exp=n_exp, tm=t["ln_tm"])

    counts = cnt[0, :n_exp].astype(I32)
    cap = n * TOP_K + n_exp * MOE_RB
    pstart, nblk, pad_lo, pad_n, tail = _group_layout(counts, cap)
    dest = _lookup(pstart, top_idx[:, :TOP_K]) + rank[:, :TOP_K]

    xg = _dispatch(dest, h1_packed, pad_lo, pad_n, tail, tm=t["dispatch_tm"], cap=cap)
    act = _expert_up(pstart, nblk, tail, xg, lw["w_mlp1"], lw["b_mlp1"], tn=t["up_tn"])
    y = _expert_down(pstart, nblk, tail, act, lw["w_mlp2"], lw["b_mlp2"], tn=t["down_tn"])
    return _combine(dest, h1, gates, lw["ln2_g"].reshape(1, -1), lw["ln2_b"].reshape(1, -1), y,
                    alpha=alpha, tm=t["combine_tm"], chunk=t["down_tn"])


def kernel(x, mem, w_in, w_mem_kv, rel_bias, lambda_q1, lambda_k1, lambda_q2, lambda_k2, diff_norm_g,
           w_branch_a, w_branch_b, w_branch_c, w_gates, b_gates, w_o, ln1_g, ln1_b, w_router, b_router,
           w_mlp1, b_mlp1, w_mlp2, b_mlp2, ln2_g, ln2_b):
    batch, seq, d = x.shape
    mem_len = mem.shape[1]
    depth = w_in.shape[0]
    alpha = (2 * depth) ** 0.25
    stacked = dict(w_in=w_in, w_mem_kv=w_mem_kv, rel_bias=rel_bias, lambda_q1=lambda_q1,
                   lambda_k1=lambda_k1, lambda_q2=lambda_q2, lambda_k2=lambda_k2, diff_norm_g=diff_norm_g,
                   w_branch_a=w_branch_a, w_branch_b=w_branch_b, w_branch_c=w_branch_c, w_gates=w_gates,
                   b_gates=b_gates, w_o=w_o, ln1_g=ln1_g, ln1_b=ln1_b, w_router=w_router,
                   b_router=b_router, w_mlp1=w_mlp1, b_mlp1=b_mlp1, w_mlp2=w_mlp2, b_mlp2=b_mlp2,
                   ln2_g=ln2_g, ln2_b=ln2_b)
    h = x.reshape(batch * seq, d)
    mem2 = mem.reshape(batch * mem_len, d)
    for l in range(depth):
        lw = {name: w[l] for name, w in stacked.items()}
        lambda_init = 0.8 - 0.6 * math.exp(-0.3 * l)
        h = _layer(h, mem2, lw, batch=batch, seq=seq, mem_len=mem_len, lambda_init=lambda_init, alpha=alpha)
    return h.reshape(batch, seq, d)
```

```python
import functools
import math

import jax
import jax.numpy as jnp
from jax import lax
from jax.experimental import pallas as pl
from jax.experimental.pallas import tpu as pltpu

F32 = jnp.float32
BF16 = jnp.bfloat16
U32 = jnp.uint32
I32 = jnp.int32

CHUNK = 64
LEFT_CHUNKS = 8
MAX_REL = 128
A_HEADS = 16
A_HEAD_DIM = 128
B_HEADS = 4
B_HEAD_DIM = 128
C_HEADS = 4
C_HEAD_DIM = 256
N_BRANCHES = 3
ROPE_THETA = 10000.0
TOP_K = 4
SWIGLU_LIMIT = 7.0
SWIGLU_ALPHA = 1.702
LN_EPS = 1e-5
RMS_EPS = 1e-5
MASK_VALUE = -1e30

V7X_VMEM_BYTES = 64 * 1024 * 1024
V7X_VMEM_LIMIT = V7X_VMEM_BYTES - 8 * 1024 * 1024
LANES = 128

NT_DIMS = (((1,), (1,)), ((), ()))


def _params(*semantics):
    return pltpu.CompilerParams(dimension_semantics=semantics,
                                vmem_limit_bytes=V7X_VMEM_LIMIT)


def _dot(a, b):
    return jnp.dot(a, b, preferred_element_type=F32)


def _pack_halves(x):
    w = x.shape[1] // 2
    hi = lax.bitcast_convert_type(x[:, :w].astype(jnp.bfloat16).astype(F32), U32)
    lo = lax.bitcast_convert_type(x[:, w:].astype(jnp.bfloat16).astype(F32), U32)
    return hi | (lo >> 16)


def _unpack_halves(p):
    hi = lax.bitcast_convert_type(p & jnp.uint32(0xFFFF0000), F32)
    lo = lax.bitcast_convert_type(p << 16, F32)
    return hi, lo


def _cast_kernel(x_ref, o_ref):
    o_ref[...] = x_ref[...].astype(o_ref.dtype)


def _cast_bf16(x, tm):
    m, d = x.shape
    return pl.pallas_call(
        _cast_kernel,
        out_shape=jax.ShapeDtypeStruct((m, d), BF16),
        grid=(m // tm,),
        in_specs=[pl.BlockSpec((tm, d), lambda i: (i, 0))],
        out_specs=pl.BlockSpec((tm, d), lambda i: (i, 0)),
        compiler_params=_params("parallel"),
        name="cast_bf16",
    )(x)


def _mm_kernel(a_ref, w_ref, o_ref):
    a = a_ref[...].astype(BF16)
    o_ref[...] = _dot(a, w_ref[...].astype(BF16)).astype(o_ref.dtype)


def _matmul(a, w, *, tm, tn, out_dtype, name):
    m, k = a.shape
    n = w.shape[1]
    return pl.pallas_call(
        _mm_kernel,
        out_shape=jax.ShapeDtypeStruct((m, n), out_dtype),
        grid=(n // tn, m // tm),
        in_specs=[pl.BlockSpec((tm, k), lambda j, i: (i, 0)),
                  pl.BlockSpec((k, tn), lambda j, i: (0, j))],
        out_specs=pl.BlockSpec((tm, tn), lambda j, i: (i, j)),
        compiler_params=_params("parallel", "parallel"),
        name=name,
    )(a, w)


A_TQ = 2 * CHUNK
A_WIN = (LEFT_CHUNKS + 2) * CHUNK
A_VARIANTS = LEFT_CHUNKS * CHUNK // A_TQ + 1


A_BASE_W = A_WIN + A_TQ


def _band_bias_base(rel_bias):
    reach = A_WIN
    ext =jnp.pad(rel_bias.astype(F32), ((0, 0), (reach - MAX_REL, reach - MAX_REL)), mode="edge")
    rev = ext[:, ::-1]
    rows = []
    for v in range(A_VARIANTS):
        c = rev[:, reach - A_TQ * v - A_TQ: reach - A_TQ * v + A_WIN]
        rows.append(jnp.concatenate([c[:, A_TQ:], c[:, :A_TQ]], axis=1))
    return jnp.stack(rows)[:, :, None, :]


def _attn_a_kernel(q_ref, k_ref, v_ref, base_ref, o_ref, tb_ref, *, heads, dh, scale):
    i = pl.program_id(2)

    @pl.when(i == 0)
    def _():
        r = lax.broadcasted_iota(I32, (A_TQ, A_WIN), 0)
        j = lax.broadcasted_iota(I32, (A_TQ, A_WIN), 1)
        for var in range(A_VARIANTS):
            cdiff = (A_TQ * var + r) // CHUNK - j // CHUNK
            valid = (cdiff >= 0) & (cdiff <= LEFT_CHUNKS)
            for h in range(heads):
                rows = jnp.broadcast_to(base_ref[var, h], (A_TQ, A_BASE_W))
                toeplitz = pltpu.roll(rows, 0, 1, stride=1, stride_axis=0)[:, :A_WIN]
                tb_ref[var, h] = jnp.where(valid, toeplitz, MASK_VALUE)

    var = jnp.minimum(i, A_VARIANTS - 1)
    start = pl.multiple_of(jnp.maximum(i - (A_VARIANTS - 1), 0) * A_TQ, A_TQ)
    for h in range(heads):
        cs = slice(h * dh, (h + 1) * dh)
        q = q_ref[:, cs]
        k = k_ref[pl.ds(start, A_WIN), cs]
        v = v_ref[pl.ds(start, A_WIN), cs]
        s = lax.dot_general(q, k, NT_DIMS, preferred_element_type=F32) * scale + tb_ref[var, h]
        m = jnp.max(s, axis=-1, keepdims=True)
        p = jnp.exp(s - m)
        l = jnp.sum(p, axis=-1, keepdims=True)
        o = _dot(p.astype(BF16), v)
        o_ref[:, cs] = (o / l).astype(o_ref.dtype)


def _attention_a(proj, base, *, batch, seq, heads, dh, col_q, col_k, col_v, heads_per_step=8):
    n = proj.shape[0]
    gw = heads_per_step * dh
    n_groups = heads // heads_per_step
    n_qb = seq // A_TQ
    kern = functools.partial(_attn_a_kernel, heads=heads_per_step, dh=dh, scale=dh ** -0.5)
    return pl.pallas_call(
        kern,
        out_shape=jax.ShapeDtypeStruct((n, heads * dh), BF16),
        grid=(batch, n_groups, n_qb),
        in_specs=[
            pl.BlockSpec((A_TQ, gw), lambda b, g, i: (b * n_qb + i, col_q // gw + g)),
            pl.BlockSpec((seq, gw), lambda b, g, i: (b, col_k // gw + g)),
            pl.BlockSpec((seq, gw), lambda b, g, i: (b, col_v // gw + g)),
            pl.BlockSpec((A_VARIANTS, heads_per_step, 1, A_BASE_W), lambda b, g, i: (0, g, 0, 0)),
        ],
        out_specs=pl.BlockSpec((A_TQ, gw), lambda b, g, i: (b * n_qb + i, g)),
        scratch_shapes=[pltpu.VMEM((A_VARIANTS, heads_per_step, A_TQ, A_WIN), F32)],
        compiler_params=_params("parallel", "parallel", "arbitrary"),
        name="attn_band",
    )(proj, proj, proj, base)


B_TQ = 256


def _rope_tables(seq, dim):
    inv = 1.0 / (ROPE_THETA ** (jnp.arange(0, dim, 2, dtype=F32) / dim))
    ang = jnp.arange(seq, dtype=F32)[:, None] * inv[None, :]
    ang = jnp.concatenate([ang, ang], -1)
    sign = jnp.where(jnp.arange(dim) < dim // 2, -1.0, 1.0).astype(F32)
    return jnp.cos(ang), jnp.sin(ang) * sign[None, :]


def _rope(x, cos, sin_signed):
    return x * cos + pltpu.roll(x, x.shape[1] // 2, 1) * sin_signed


def _attn_b_kernel(q_ref, k_ref, v_ref, cos_ref, sin_ref, lam_ref, g_ref, o_ref, krot_ref,
                   *, dh, scale, lambda_init):
    qi = pl.program_id(2)
    seq = k_ref.shape[0]

    @pl.when(qi == 0)
    def _():
        for m in range(2):
            kf = k_ref[:, m * dh:(m + 1) * dh].astype(F32)
            krot_ref[m] = _rope(kf, cos_ref[...], sin_ref[...]).astype(BF16)

    lv = lam_ref[...]
    lam = (jnp.exp(jnp.sum(lv[0:1] * lv[1:2], axis=-1, keepdims=True))
           - jnp.exp(jnp.sum(lv[2:3] * lv[3:4], axis=-1, keepdims=True)) + lambda_init)

    def block(blk):
        row0 = blk * B_TQ
        kl = row0 + B_TQ
        cos_q = cos_ref[row0:kl, :]
        sin_q = sin_ref[row0:kl, :]
        q_chunk = (row0 + lax.broadcasted_iota(I32, (B_TQ, kl), 0)) // CHUNK
        k_chunk = lax.broadcasted_iota(I32, (B_TQ, kl), 1) // CHUNK
        allowed = k_chunk <= q_chunk
        probs = []
        for m in range(2):
            qf = q_ref[:, m * dh:(m + 1) * dh].astype(F32)
            qr = _rope(qf, cos_q, sin_q).astype(BF16)
            s = lax.dot_general(qr, krot_ref[m, :kl, :], NT_DIMS, preferred_element_type=F32) * scale
            s = jnp.where(allowed, s, MASK_VALUE)
            e = jnp.exp(s - jnp.max(s, axis=-1, keepdims=True))
            probs.append(e / jnp.sum(e, axis=-1, keepdims=True))
        w = (probs[0] - lam * probs[1]).astype(BF16)
        o = _dot(w, v_ref[:kl, :])
        ms = jnp.mean(o * o, axis=-1, keepdims=True)
        y = o * lax.rsqrt(ms + RMS_EPS) * g_ref[...] * (1.0 - lambda_init)
        o_ref[...] = y.astype(o_ref.dtype)

    for blk in range(seq // B_TQ):
        pl.when(qi == blk)(functools.partial(block, blk))


def _attention_b(proj, cos, sin_signed, lam_vecs, norm_g, *, batch, seq, heads, dh,
                 col_q, col_k, col_v, lambda_init):
    n = proj.shape[0]
    hw = 2 * dh
    n_qb = seq // B_TQ
    kern = functools.partial(_attn_b_kernel, dh=dh, scale=dh ** -0.5, lambda_init=lambda_init)
    return pl.pallas_call(
        kern,
        out_shape=jax.ShapeDtypeStruct((n, heads * hw), BF16),
        grid=(batch, heads, n_qb),
        in_specs=[
            pl.BlockSpec((B_TQ, hw), lambda b, h, i: (b * n_qb + i, col_q // hw + h)),
            pl.BlockSpec((seq, hw), lambda b, h, i: (b, col_k // hw + h)),
            pl.BlockSpec((seq, hw), lambda b, h, i: (b, col_v // hw + h)),
            pl.BlockSpec((seq, dh), lambda b, h, i: (0, 0)),
            pl.BlockSpec((seq, dh), lambda b, h, i: (0, 0)),
            pl.BlockSpec((4, dh), lambda b, h, i: (0, 0)),
            pl.BlockSpec((1, hw), lambda b, h, i: (0, 0)),
        ],
        out_specs=pl.BlockSpec((B_TQ, hw), lambda b, h, i: (b * n_qb + i, h)),
        scratch_shapes=[pltpu.VMEM((2, seq, dh), BF16)],
        compiler_params=_params("parallel", "parallel", "arbitrary"),
        name="attn_diff",
    )(proj, proj, proj, cos, sin_signed, lam_vecs, norm_g)


C_TQ = 512


def _attn_c_kernel(q_ref, k_ref, v_ref, o_ref, *, scale):
    s = lax.dot_general(q_ref[...], k_ref[...], NT_DIMS, preferred_element_type=F32) * scale
    e = jnp.exp(s - jnp.max(s, axis=-1, keepdims=True))
    p = (e / jnp.sum(e, axis=-1, keepdims=True)).astype(BF16)
    o_ref[...] = _dot(p, v_ref[...]).astype(o_ref.dtype)


def _attention_c(proj, ckv, *, batch, seq, mem_len, heads, dh, col_q):
    n = proj.shape[0]
    n_qb = seq // C_TQ
    kern = functools.partial(_attn_c_kernel, scale=dh ** -0.5)
    return pl.pallas_call(
        kern,
        out_shape=jax.ShapeDtypeStruct((n, heads * dh), BF16),
        grid=(batch, heads, n_qb),
        in_specs=[
            pl.BlockSpec((C_TQ, dh), lambda b, h, i: (b * n_qb + i, col_q // dh + h)),
            pl.BlockSpec((mem_len, dh), lambda b, h, i: (b, h)),
            pl.BlockSpec((mem_len, dh), lambda b, h, i: (b, heads + h)),
        ],
        out_specs=pl.BlockSpec((C_TQ, dh), lambda b, h, i: (b * n_qb + i, h)),
        compiler_params=_params("parallel", "parallel", "parallel"),
        name="attn_mem",
    )(proj, ckv, ckv)


def _merge_kernel(x_ref, ya_ref, yb_ref, yc_ref, wga_ref, wgb_ref, wgc_ref,
                  bga_ref, bgb_ref, bgc_ref, pa_ref, pb_ref, pc_ref, o_ref):
    x = x_ref[...]
    acc = None
    for wg, bg, y, p in ((wga_ref, bga_ref, ya_ref, pa_ref),
                         (wgb_ref, bgb_ref, yb_ref, pb_ref),
                         (wgc_ref, bgc_ref, yc_ref, pc_ref)):
        gate = jax.nn.sigmoid(_dot(x, wg[...].astype(BF16)) + bg[...])
        term = gate * _dot(y[...], p[...].astype(BF16))
        acc = term if acc is None else acc + term
    o_ref[...] = acc.astype(o_ref.dtype)


def _gated_merge(xb, ya, yb, yc, w_gates, b_gates, pa, pb, pc, *, tm, tn):
    n, d = xb.shape
    nj = d // tn
    row = lambda width: pl.BlockSpec((tm, width), lambda j, i: (i, 0))
    gate_w = lambda br: pl.BlockSpec((d, tn), lambda j, i, br=br: (0, br * nj + j))
    gate_b = lambda br: pl.BlockSpec((1, tn), lambda j, i, br=br: (0, br * nj + j))
    branch_w = lambda width: pl.BlockSpec((width, tn), lambda j, i: (0, j))
    return pl.pallas_call(
        _merge_kernel,
        out_shape=jax.ShapeDtypeStruct((n, d), BF16),
        grid=(nj, n // tm),
        in_specs=[row(d), row(ya.shape[1]), row(yb.shape[1]), row(yc.shape[1]),
                  gate_w(0), gate_w(1), gate_w(2), gate_b(0), gate_b(1), gate_b(2),
                  branch_w(pa.shape[0]), branch_w(pb.shape[0]), branch_w(pc.shape[0])],
        out_specs=pl.BlockSpec((tm, tn), lambda j, i: (i, j)),
        compiler_params=_params("parallel", "parallel"),
        name="gated_merge",
    )(xb, ya, yb, yc, w_gates, w_gates, w_gates, b_gates, b_gates, b_gates, pa, pb, pc)


def _layer_norm(z, g, b):
    mu = jnp.mean(z, axis=-1, keepdims=True)
    zc = z - mu
    var = jnp.mean(zc * zc, axis=-1, keepdims=True)
    return zc * lax.rsqrt(var + LN_EPS) * g + b


def _ln_router_kernel(x_ref, m_ref, g_ref, b_ref, wr_ref, br_ref,
                      h_ref, hp_ref, idx_ref, gate_ref, rank_ref, cnt_ref, carry_ref,
                      *, alpha, n_exp):
    @pl.when(pl.program_id(0) == 0)
    def _():
        carry_ref[...] = jnp.zeros_like(carry_ref)

    tm = x_ref.shape[0]
    h = _layer_norm(alpha * x_ref[...] + m_ref[...], g_ref[...], b_ref[...])
    h_ref[...] = h
    hp_ref[...] = _pack_halves(h)

    h_hi = h.astype(BF16)
    h_lo = (h - h_hi.astype(F32)).astype(BF16)
    w = wr_ref[...]
    r1 = _dot(h_hi, w)
    logits = r1[:, :LANES] + r1[:, LANES:] + _dot(h_lo, w[:, :LANES]) + br_ref[...]

    lane = lax.broadcasted_iota(I32, (tm, LANES), 1)
    lane_f = lane.astype(F32)
    cur = jnp.where(lane < n_exp, logits, -jnp.inf)
    vals, idxs = [], []
    for _ in range(TOP_K):
        mx = jnp.max(cur, axis=-1, keepdims=True)
        ix = jnp.min(jnp.where(cur == mx, lane_f, float(LANES)), axis=-1, keepdims=True).astype(I32)
        vals.append(mx)
        idxs.append(ix)
        cur = jnp.where(lane == ix, -jnp.inf, cur)
    exps = [jnp.exp(v - vals[0]) for v in vals]
    den = exps[0]
    for e in exps[1:]:
        den = den + e

    tri = (lax.broadcasted_iota(I32, (tm, tm), 0) > lax.broadcasted_iota(I32, (tm, tm), 1)).astype(BF16)
    carry = carry_ref[...]
    idx_out = jnp.zeros((tm, LANES), I32)
    gate_out = jnp.zeros((tm, LANES), F32)
    rank_out = jnp.zeros((tm, LANES), I32)
    for k in range(TOP_K):
        onehot = (lane == idxs[k]).astype(F32)
        before = _dot(tri, onehot.astype(BF16)) + carry
        rank = jnp.sum(onehot * before, axis=-1, keepdims=True)
        carry = carry + jnp.sum(onehot, axis=0, keepdims=True)
        idx_out = jnp.where(lane == k, idxs[k], idx_out)
        gate_out = jnp.where(lane == k, exps[k] / den, gate_out)
        rank_out = jnp.where(lane == k, rank.astype(I32), rank_out)
    carry_ref[...] = carry
    idx_ref[...] = idx_out
    gate_ref[...] = gate_out
    rank_ref[...] = rank_out
    cnt_ref[...] = carry


def _ln_router(x, m, g, b, wr_split, br_pad, *, alpha, n_exp, tm):
    n, d = x.shape
    row = pl.BlockSpec((tm, d), lambda i: (i, 0))
    vec = pl.BlockSpec((1, d), lambda i: (0, 0))
    small = pl.BlockSpec((tm, LANES), lambda i: (i, 0))
    kern = functools.partial(_ln_router_kernel, alpha=alpha, n_exp=n_exp)
    return pl.pallas_call(
        kern,
        out_shape=(jax.ShapeDtypeStruct((n, d), F32),
                   jax.ShapeDtypeStruct((n, d // 2), U32),
                   jax.ShapeDtypeStruct((n, LANES), I32),
                   jax.ShapeDtypeStruct((n, LANES), F32),
                   jax.ShapeDtypeStruct((n, LANES), I32),
                   jax.ShapeDtypeStruct((1, LANES), F32)),
        grid=(n // tm,),
        in_specs=[row, row, vec, vec,
                  pl.BlockSpec((d, 2 * LANES), lambda i: (0, 0)),
                  pl.BlockSpec((1, LANES), lambda i: (0, 0))],
        out_specs=(row, pl.BlockSpec((tm, d // 2), lambda i: (i, 0)), small, small, small,
                   pl.BlockSpec((1, LANES), lambda i: (0, 0))),
        scratch_shapes=[pltpu.VMEM((1, LANES), F32)],
        compiler_params=_params("arbitrary"),
        name="ln_router",
    )(x, m, g, b, wr_split, br_pad)


MOE_RB = 256
ROW_DMA_PRIORITY = 1
CAST_ROWS = 512


def _cumsum_small(x):
    n = x.shape[0]
    keep = jnp.arange(n)[:, None] >= jnp.arange(n)[None, :]
    return jnp.sum(jnp.where(keep, x[None, :], 0), axis=1).astype(x.dtype)


def _lookup(table, idx):
    hit = idx[..., None] == jnp.arange(table.shape[0], dtype=idx.dtype)
    return jnp.sum(jnp.where(hit, table, 0), axis=-1).astype(table.dtype)


def _group_layout(counts, cap):
    padded = (counts + MOE_RB - 1) // MOE_RB * MOE_RB
    pend = _cumsum_small(padded)
    pstart = pend - padded
    tail = jnp.stack([pend[-1], (cap - pend[-1]) // MOE_RB])
    return pstart, padded // MOE_RB, pstart + counts, padded - counts, tail


def _tail_rows(tail, b):
    return pl.ds(pl.multiple_of(tail[0] + b * MOE_RB, MOE_RB), MOE_RB)


def _zero_tail(tail, zero_view, dst_view, sem):
    def start(b, carry):
        pltpu.make_async_copy(zero_view, dst_view(_tail_rows(tail, b)), sem).start()
        return carry

    lax.fori_loop(0, tail[1], start, 0)

    def done(b, carry):
        pltpu.make_async_copy(zero_view, dst_view(_tail_rows(tail, 0)), sem).wait()
        return carry

    lax.fori_loop(0, tail[1], done, 0)


def _row_copy(src, src_row, dst, dst_row, sem):
    return pltpu.make_async_copy(src.at[pl.ds(src_row, 1)], dst.at[pl.ds(dst_row, 1)], sem)


def _dispatch_kernel(pad_lo, pad_n, tail, dest_ref, hp_ref, xg_ref, zero_ref, sem, zsem):
    tm = hp_ref.shape[0]
    n_exp = pad_lo.shape[0]

    @pl.when(pl.program_id(0) == 0)
    def _():
        zero_ref[...] = jnp.zeros_like(zero_ref)
        _zero_tail(tail, zero_ref, lambda rows: xg_ref.at[rows], zsem)

        def fill(e, carry):
            def one(r, c):
                _row_copy(zero_ref, 0, xg_ref, pad_lo[e] + r, zsem).start()
                return c
            return lax.fori_loop(0, pad_n[e], one, carry)

        lax.fori_loop(0, n_exp, fill, 0)

        def fill_done(e, carry):
            def one(r, c):
                _row_copy(zero_ref, 0, xg_ref, 0, zsem).wait()
                return c
            return lax.fori_loop(0, pad_n[e], one, carry)

        lax.fori_loop(0, n_exp, fill_done, 0)

    def issue(t, carry):
        for k in range(TOP_K):
            _row_copy(hp_ref, t, xg_ref, dest_ref[0, t * TOP_K + k], sem).start(priority=k % 2)
        return carry

    lax.fori_loop(0, tm, issue, 0)

    def drain(t, carry):
        for k in range(TOP_K):
            _row_copy(hp_ref, 0, xg_ref, 0, sem).wait()
        return carry

    lax.fori_loop(0, tm, drain, 0)


def _dispatch(dest, hp, pad_lo, pad_n, tail, *, tm, cap):
    n, w = hp.shape
    dest3 = dest.reshape(n // tm, 1, tm * TOP_K)
    grid_spec = pltpu.PrefetchScalarGridSpec(
        num_scalar_prefetch=3,
        grid=(n // tm,),
        in_specs=[pl.BlockSpec((None, 1, tm * TOP_K), lambda i, *_: (i, 0, 0), memory_space=pltpu.SMEM),
                  pl.BlockSpec((tm, w), lambda i, *_: (i, 0))],
        out_specs=pl.BlockSpec(memory_space=pl.ANY),
        scratch_shapes=[pltpu.VMEM((MOE_RB, w), U32), pltpu.SemaphoreType.DMA(()), pltpu.SemaphoreType.DMA(())],
    )
    return pl.pallas_call(
        _dispatch_kernel,
        out_shape=jax.ShapeDtypeStruct((cap, w), U32),
        grid_spec=grid_spec,
        compiler_params=_params("arbitrary"),
        name="dispatch",
    )(pad_lo, pad_n, tail, dest3, hp)


SEL_W = 512


def _even_lane_selector():
    r = jnp.arange(SEL_W)[:, None]
    c = jnp.arange(SEL_W // 2)[None, :]
    return (r == 2 * c).astype(BF16)


def _swiglu_even_lanes(h, sel):
    tn = h.shape[1]
    glu = jnp.minimum(h, SWIGLU_LIMIT)
    lin = jnp.clip(h, -SWIGLU_LIMIT, SWIGLU_LIMIT) + 1.0
    gact = glu * jax.nn.sigmoid(SWIGLU_ALPHA * glu)
    parts = []
    for c in range(tn // LANES):
        cs = slice(c * LANES, (c + 1) * LANES)
        parts.append(gact[:, cs] * pltpu.roll(lin[:, cs], LANES - 1, 1))
    inter = jnp.concatenate(parts, axis=1).astype(BF16)
    acts = [_dot(inter[:, s * SEL_W:(s + 1) * SEL_W], sel) for s in range(tn // SEL_W)]
    return jnp.concatenate(acts, axis=1)


def _up_kernel(pstart, nblk, tail, xg_ref, w1_ref, b1_ref, sel_ref, act_ref,
               xbuf, hbuf, obuf, wbf, sem_in, sem_out, *, n_chunks):
    c = pl.program_id(0)
    e = pl.program_id(1)
    n = nblk[e]
    base = pstart[e]
    rb = xbuf.shape[1]
    half = w1_ref.shape[0] // 2
    tno = obuf.shape[2]

    def rows(blk):
        return pl.ds(pl.multiple_of(base + blk * rb, rb), rb)

    def in_copy(blk, slot):
        return pltpu.make_async_copy(xg_ref.at[rows(blk)], xbuf.at[slot], sem_in.at[slot])

    def out_start(blk, slot):
        for cc in range(n_chunks):
            @pl.when(c == cc)
            def _(cc=cc):
                pltpu.make_async_copy(obuf.at[slot], act_ref.at[rows(blk), pl.ds(cc * tno, tno)],
                                      sem_out.at[slot]).start()

    def out_wait(slot):
        pltpu.make_async_copy(obuf.at[slot], act_ref.at[pl.ds(0, rb), pl.ds(0, tno)], sem_out.at[slot]).wait()

    def matmul(slot):
        xa, xb = _unpack_halves(xbuf[slot])
        hbuf[slot] = (_dot(xa.astype(BF16), wbf[:half, :]) + _dot(xb.astype(BF16), wbf[half:, :])
                      + b1_ref[...])

    def matmul_first():
        xs = [x.astype(BF16) for x in _unpack_halves(xbuf[0])]
        acc = jnp.broadcast_to(b1_ref[...], hbuf.shape[1:])
        for part in range(2):
            for k in range(half // CAST_ROWS):
                r0 = part * half + k * CAST_ROWS
                wk = w1_ref[r0:r0 + CAST_ROWS, :].astype(BF16)
                wbf[r0:r0 + CAST_ROWS, :] = wk
                acc = acc + _dot(xs[part][:, k * CAST_ROWS:(k + 1) * CAST_ROWS], wk)
        hbuf[0] = acc

    def epilogue(slot):
        obuf[slot] = _swiglu_even_lanes(hbuf[slot], sel_ref[...]).astype(obuf.dtype)

    @pl.when(n > 0)
    def _():
        in_copy(0, 0).start(priority=ROW_DMA_PRIORITY)
        in_copy(0, 0).wait()

        @pl.when(n > 1)
        def _():
            in_copy(1, 1).start(priority=ROW_DMA_PRIORITY)

        matmul_first()

        def body(blk, carry):
            slot = lax.rem(blk, 2)
            prev = 1 - slot
            in_copy(blk, slot).wait()

            @pl.when(blk + 1 < n)
            def _():
                in_copy(blk + 1, prev).start(priority=ROW_DMA_PRIORITY)

            @pl.when(blk >= 3)
            def _():
                out_wait(prev)

            epilogue(prev)
            matmul(slot)
            out_start(blk - 1, prev)
            return carry

        lax.fori_loop(1, n, body, 0)

        last = n - 1
        ls = lax.rem(last, 2)

        @pl.when(last >= 2)
        def _():
            out_wait(ls)

        epilogue(ls)
        out_start(last, ls)

        @pl.when(last >= 1)
        def _():
            out_wait(1 - ls)

        out_wait(ls)

    @pl.when(e == pl.num_programs(1) - 1)
    def _():
        obuf[0] = jnp.zeros(obuf.shape[1:], obuf.dtype)
        for cc in range(n_chunks):
            @pl.when(c == cc)
            def _(cc=cc):
                _zero_tail(tail, obuf.at[0], lambda r: act_ref.at[r, pl.ds(cc * tno, tno)], sem_out.at[0])


def _expert_up(pstart, nblk, tail, xg, w1, b1, *, tn):
    cap, w = xg.shape
    n_exp, d, f2 = w1.shape
    n_chunks = f2 // tn
    grid_spec = pltpu.PrefetchScalarGridSpec(
        num_scalar_prefetch=3,
        grid=(n_chunks, n_exp),
        in_specs=[
            pl.BlockSpec(memory_space=pl.ANY),
            pl.BlockSpec((None, d, tn), lambda c, e, *_: (e, 0, c)),
            pl.BlockSpec((None, 1, tn), lambda c, e, *_: (e, 0, c)),
            pl.BlockSpec((SEL_W, SEL_W // 2), lambda c, e, *_: (0, 0)),
        ],
        out_specs=pl.BlockSpec(memory_space=pl.ANY),
        scratch_shapes=[pltpu.VMEM((2, MOE_RB, w), U32),
                        pltpu.VMEM((2, MOE_RB, tn), F32),
                        pltpu.VMEM((2, MOE_RB, tn // 2), BF16),
                        pltpu.VMEM((d, tn), BF16),
                        pltpu.SemaphoreType.DMA((2,)),
                        pltpu.SemaphoreType.DMA((2,))],
    )
    return pl.pallas_call(
        functools.partial(_up_kernel, n_chunks=n_chunks),
        out_shape=jax.ShapeDtypeStruct((cap, f2 // 2), BF16),
        grid_spec=grid_spec,
        compiler_params=_params("arbitrary", "arbitrary"),
        name="expert_up",
    )(pstart, nblk, tail, xg, w1, b1.reshape(n_exp, 1, f2), _even_lane_selector())


def _down_kernel(pstart, nblk, tail, act_ref, w2_ref, b2_ref, y_ref, abuf, obuf, wbf, sem_in, sem_out,
                 *, n_chunks):
    c = pl.program_id(0)
    e = pl.program_id(1)
    n = nblk[e]
    base = pstart[e]
    rb = abuf.shape[1]
    tno = obuf.shape[2]

    def rows(blk):
        return pl.ds(pl.multiple_of(base + blk * rb, rb), rb)

    def in_copy(blk, slot):
        return pltpu.make_async_copy(act_ref.at[rows(blk)], abuf.at[slot], sem_in.at[slot])

    def out_start(blk, slot):
        for cc in range(n_chunks):
            @pl.when(c == cc)
            def _(cc=cc):
                pltpu.make_async_copy(obuf.at[slot], y_ref.at[rows(blk), pl.ds(cc * tno, tno)],
                                      sem_out.at[slot]).start()

    def out_wait(slot):
        pltpu.make_async_copy(obuf.at[slot], y_ref.at[pl.ds(0, rb), pl.ds(0, tno)], sem_out.at[slot]).wait()

    @pl.when(n > 0)
    def _():
        in_copy(0, 0).start(priority=ROW_DMA_PRIORITY)
        in_copy(0, 0).wait()

        @pl.when(n > 1)
        def _():
            in_copy(1, 1).start(priority=ROW_DMA_PRIORITY)

        a0 = abuf[0]
        acc = jnp.broadcast_to(b2_ref[...], (rb, w2_ref.shape[1]))
        for k in range(w2_ref.shape[0] // CAST_ROWS):
            ks = slice(k * CAST_ROWS, (k + 1) * CAST_ROWS)
            wk = w2_ref[ks, :].astype(BF16)
            wbf[ks, :] = wk
            acc = acc + _dot(a0[:, ks], wk)
        obuf[0] = _pack_halves(acc)
        out_start(0, 0)

        def body(blk, carry):
            slot = lax.rem(blk, 2)
            in_copy(blk, slot).wait()

            @pl.when(blk + 1 < n)
            def _():
                in_copy(blk + 1, 1 - slot).start(priority=ROW_DMA_PRIORITY)

            @pl.when(blk >= 2)
            def _():
                out_wait(slot)

            obuf[slot] = _pack_halves(_dot(abuf[slot], wbf[...]) + b2_ref[...])
            out_start(blk, slot)
            return carry

        lax.fori_loop(1, n, body, 0)

        last = n - 1

        @pl.when(last >= 1)
        def _():
            out_wait(lax.rem(last + 1, 2))

        out_wait(lax.rem(last, 2))

    @pl.when(e == pl.num_programs(1) - 1)
    def _():
        obuf[0] = jnp.zeros(obuf.shape[1:], obuf.dtype)
        for cc in range(n_chunks):
            @pl.when(c == cc)
            def _(cc=cc):
                _zero_tail(tail, obuf.at[0], lambda r: y_ref.at[r, pl.ds(cc * tno, tno)], sem_out.at[0])


def _expert_down(pstart, nblk, tail, act, w2, b2, *, tn):
    cap, f = act.shape
    n_exp, _, d = w2.shape
    n_chunks = d // tn
    grid_spec = pltpu.PrefetchScalarGridSpec(
        num_scalar_prefetch=3,
        grid=(n_chunks, n_exp),
        in_specs=[
            pl.BlockSpec(memory_space=pl.ANY),
            pl.BlockSpec((None, f, tn), lambda c, e, *_: (e, 0, c)),
            pl.BlockSpec((None, 1, tn), lambda c, e, *_: (e, 0, c)),
        ],
        out_specs=pl.BlockSpec(memory_space=pl.ANY),
        scratch_shapes=[pltpu.VMEM((2, MOE_RB, f), BF16),
                        pltpu.VMEM((2, MOE_RB, tn // 2), U32),
                        pltpu.VMEM((f, tn), BF16),
                        pltpu.SemaphoreType.DMA((2,)),
                        pltpu.SemaphoreType.DMA((2,))],
    )
    return pl.pallas_call(
        functools.partial(_down_kernel, n_chunks=n_chunks),
        out_shape=jax.ShapeDtypeStruct((cap, d // 2), U32),
        grid_spec=grid_spec,
        compiler_params=_params("arbitrary", "arbitrary"),
        name="expert_down",
    )(pstart, nblk, tail, act, w2, b2.reshape(n_exp, 1, d))


def _combine_kernel(dest_ref, next_ref, h_ref, gate_ref, g_ref, b_ref, y_ref, o_ref, buf_ref, sem,
                    *, alpha, chunk):
    tm = h_ref.shape[0]
    i = pl.program_id(0)
    slot = lax.rem(i, 2)

    def gather(idx_ref, s):
        def issue(t, carry):
            for k in range(TOP_K):
                pltpu.make_async_copy(y_ref.at[pl.ds(idx_ref[0, t * TOP_K + k], 1)],
                                      buf_ref.at[s, k, pl.ds(t, 1)], sem.at[s]).start(priority=k % 2)
            return carry
        lax.fori_loop(0, tm, issue, 0)

    @pl.when(i == 0)
    def _():
        gather(dest_ref, 0)

    @pl.when(i + 1 < pl.num_programs(0))
    def _():
        gather(next_ref, 1 - slot)

    def drain(t, carry):
        for k in range(TOP_K):
            pltpu.make_async_copy(y_ref.at[pl.ds(0, 1)], buf_ref.at[slot, k, pl.ds(0, 1)], sem.at[slot]).wait()
        return carry

    lax.fori_loop(0, tm, drain, 0)

    gates = gate_ref[...]
    acc_hi = None
    acc_lo = None
    for k in range(TOP_K):
        hi, lo = _unpack_halves(buf_ref[slot, k])
        gk = gates[:, k:k + 1]
        acc_hi = gk * hi if acc_hi is None else acc_hi + gk * hi
        acc_lo = gk * lo if acc_lo is None else acc_lo + gk * lo
    hw = chunk // 2
    pieces = []
    for c in range(acc_hi.shape[1] // hw):
        pieces.append(acc_hi[:, c * hw:(c + 1) * hw])
        pieces.append(acc_lo[:, c * hw:(c + 1) * hw])
    ffn = jnp.concatenate(pieces, axis=1)
    o_ref[...] = _layer_norm(alpha * h_ref[...] + ffn, g_ref[...], b_ref[...]).astype(o_ref.dtype)


def _combine(dest, h, gates, g, b, y, *, alpha, tm, chunk):
    n, d = h.shape
    n_steps = n // tm
    dest3 = dest.reshape(n_steps, 1, tm * TOP_K)
    kern = functools.partial(_combine_kernel, alpha=alpha, chunk=chunk)
    return pl.pallas_call(
        kern,
        out_shape=jax.ShapeDtypeStruct((n, d), F32),
        grid=(n_steps,),
        in_specs=[pl.BlockSpec((None, 1, tm * TOP_K), lambda i: (i, 0, 0), memory_space=pltpu.SMEM),
                  pl.BlockSpec((None, 1, tm * TOP_K), lambda i: (jnp.minimum(i + 1, n_steps - 1), 0, 0),
                               memory_space=pltpu.SMEM),
                  pl.BlockSpec((tm, d), lambda i: (i, 0)),
                  pl.BlockSpec((tm, LANES), lambda i: (i, 0)),
                  pl.BlockSpec((1, d), lambda i: (0, 0)),
                  pl.BlockSpec((1, d), lambda i: (0, 0)),
                  pl.BlockSpec(memory_space=pl.ANY)],
        out_specs=pl.BlockSpec((tm, d), lambda i: (i, 0)),
        scratch_shapes=[pltpu.VMEM((2, TOP_K, tm, d // 2), U32), pltpu.SemaphoreType.DMA((2,))],
        compiler_params=_params("arbitrary"),
        name="combine_ln",
    )(dest3, dest3, h, gates, g, b, y)


def _tiles(n_tokens, d_model, d_expert):
    return dict(
        cast_tm=min(512, n_tokens),
        proj=dict(tm=min(1024, n_tokens), tn=512),
        memkv=dict(tm=512, tn=512),
        merge=dict(tm=min(512, n_tokens), tn=256),
        out=dict(tm=min(1024, n_tokens), tn=512),
        ln_tm=min(256, n_tokens),
        dispatch_tm=min(256, n_tokens),
        up_tn=min(1024, 2 * d_expert),
        down_tn=min(2048, d_model),
        combine_tm=min(128, n_tokens),
    )


def _layer(h, mem2, lw, *, batch, seq, mem_len, lambda_init, alpha):
    n, d = h.shape
    t = _tiles(n, d, lw["w_mlp2"].shape[1])
    a_width = A_HEADS * A_HEAD_DIM
    b_width = B_HEADS * 2 * B_HEAD_DIM
    col = dict(a_q=0, a_k=a_width, a_v=2 * a_width, b_q=3 * a_width, b_k=3 * a_width + b_width,
               b_v=3 * a_width + 2 * b_width, c_q=3 * a_width + 3 * b_width)

    xb = _cast_bf16(h, t["cast_tm"])
    proj = _matmul(xb, lw["w_in"], out_dtype=BF16, name="in_proj", **t["proj"])
    ckv = _matmul(mem2, lw["w_mem_kv"], out_dtype=BF16, name="mem_kv",
                  tm=min(t["memkv"]["tm"], mem2.shape[0]), tn=t["memkv"]["tn"])

    ya = _attention_a(proj, _band_bias_base(lw["rel_bias"]), batch=batch, seq=seq,
                      heads=A_HEADS, dh=A_HEAD_DIM, col_q=col["a_q"], col_k=col["a_k"], col_v=col["a_v"])
    cos, sin_signed = _rope_tables(seq, B_HEAD_DIM)
    lam_vecs = jnp.stack([lw["lambda_q1"], lw["lambda_k1"], lw["lambda_q2"], lw["lambda_k2"]]).astype(F32)
    yb = _attention_b(proj, cos, sin_signed, lam_vecs, lw["diff_norm_g"].reshape(1, -1),
                      batch=batch, seq=seq, heads=B_HEADS, dh=B_HEAD_DIM,
                      col_q=col["b_q"], col_k=col["b_k"], col_v=col["b_v"], lambda_init=lambda_init)
    yc = _attention_c(proj, ckv, batch=batch, seq=seq, mem_len=mem_len, heads=C_HEADS,
                      dh=C_HEAD_DIM, col_q=col["c_q"])

    merged = _gated_merge(xb, ya, yb, yc, lw["w_gates"], lw["b_gates"].reshape(1, -1),
                          lw["w_branch_a"], lw["w_branch_b"], lw["w_branch_c"], **t["merge"])
    mix = _matmul(merged, lw["w_o"], out_dtype=F32, name="out_proj", **t["out"])

    n_exp = lw["w_router"].shape[1]
    wr = jnp.pad(lw["w_router"], ((0, 0), (0, LANES - n_exp)))
    wr_hi = wr.astype(BF16)
    wr_lo = (wr - wr_hi.astype(F32)).astype(BF16)
    br = jnp.pad(lw["b_router"], (0, LANES - n_exp)).reshape(1, LANES)
    h1, h1_packed, top_idx, gates, rank, cnt = _ln_router(
        h, mix, lw["ln1_g"].reshape(1, -1), lw["ln1_b"].reshape(1, -1),
        jnp.concatenate([wr_hi, wr_lo], axis=1), br, alpha=alpha, n_exp=n_exp, tm=t["ln_tm"])

    counts = cnt[0, :n_exp].astype(I32)
    cap = n * TOP_K + n_exp * MOE_RB
    pstart, nblk, pad_lo, pad_n, tail = _group_layout(counts, cap)
    dest = _lookup(pstart, top_idx[:, :TOP_K]) + rank[:, :TOP_K]

    xg = _dispatch(dest, h1_packed, pad_lo, pad_n, tail, tm=t["dispatch_tm"], cap=cap)
    act = _expert_up(pstart, nblk, tail, xg, lw["w_mlp1"], lw["b_mlp1"], tn=t["up_tn"])
    y = _expert_down(pstart, nblk, tail, act, lw["w_mlp2"], lw["b_mlp2"], tn=t["down_tn"])
    return _combine(dest, h1, gates, lw["ln2_g"].reshape(1, -1), lw["ln2_b"].reshape(1, -1), y,
                    alpha=alpha, tm=t["combine_tm"], chunk=t["down_tn"])


def kernel(x, mem, w_in, w_mem_kv, rel_bias, lambda_q1, lambda_k1, lambda_q2, lambda_k2, diff_norm_g,
           w_branch_a, w_branch_b, w_branch_c, w_gates, b_gates, w_o, ln1_g, ln1_b, w_router, b_router,
           w_mlp1, b_mlp1, w_mlp2, b_mlp2, ln2_g, ln2_b):
    batch, seq, d = x.shape
    mem_len = mem.shape[1]
    depth = w_in.shape[0]
    alpha = (2 * depth) ** 0.25
    stacked = dict(w_in=w_in, w_mem_kv=w_mem_kv, rel_bias=rel_bias, lambda_q1=lambda_q1,
                   lambda_k1=lambda_k1, lambda_q2=lambda_q2, lambda_k2=lambda_k2, diff_norm_g=diff_norm_g,
                   w_branch_a=w_branch_a, w_branch_b=w_branch_b, w_branch_c=w_branch_c, w_gates=w_gates,
                   b_gates=b_gates, w_o=w_o, ln1_g=ln1_g, ln1_b=ln1_b, w_router=w_router,
                   b_router=b_router, w_mlp1=w_mlp1, b_mlp1=b_mlp1, w_mlp2=w_mlp2, b_mlp2=b_mlp2,
                   ln2_g=ln2_g, ln2_b=ln2_b)
    h = x.reshape(batch * seq, d)
    mem2 = mem.reshape(batch * mem_len, d)
    for l in range(depth):
        lw = {name: w[l] for name, w in stacked.items()}
        lambda_init = 0.8 - 0.6 * math.exp(-0.3 * l)
        h = _layer(h, mem2, lw, batch=batch, seq=seq, mem_len=mem_len, lambda_init=lambda_init, alpha=alpha)
    return h.reshape(batch, seq, d)
```

```python
import functools
import math

import jax
import jax.numpy as jnp
from jax import lax
from jax.experimental import pallas as pl
from jax.experimental.pallas import tpu as pltpu

F32 = jnp.float32
BF16 = jnp.bfloat16
U32 = jnp.uint32
I32 = jnp.int32

CHUNK = 64
LEFT_CHUNKS = 8
MAX_REL = 128
A_HEADS = 16
A_HEAD_DIM = 128
B_HEADS = 4
B_HEAD_DIM = 128
C_HEADS = 4
C_HEAD_DIM = 256
N_BRANCHES = 3
ROPE_THETA = 10000.0
TOP_K = 4
SWIGLU_LIMIT = 7.0
SWIGLU_ALPHA = 1.702
LN_EPS = 1e-5
RMS_EPS = 1e-5
MASK_VALUE = -1e30

V7X_VMEM_BYTES = 64 * 1024 * 1024
V7X_VMEM_LIMIT = V7X_VMEM_BYTES - 8 * 1024 * 1024
LANES = 128

NT_DIMS = (((1,), (1,)), ((), ()))


def _params(*semantics):
    return pltpu.CompilerParams(dimension_semantics=semantics,
                                vmem_limit_bytes=V7X_VMEM_LIMIT)


def _dot(a, b):
    return jnp.dot(a, b, preferred_element_type=F32)


def _pack_halves(x):
    w = x.shape[1] // 2
    hi = lax.bitcast_convert_type(x[:, :w].astype(jnp.bfloat16).astype(F32), U32)
    lo = lax.bitcast_convert_type(x[:, w:].astype(jnp.bfloat16).astype(F32), U32)
    return hi | (lo >> 16)


def _unpack_halves(p):
    hi = lax.bitcast_convert_type(p & jnp.uint32(0xFFFF0000), F32)
    lo = lax.bitcast_convert_type(p << 16, F32)
    return hi, lo


def _cast_kernel(x_ref, o_ref):
    o_ref[...] = x_ref[...].astype(o_ref.dtype)


def _cast_bf16(x, tm):
    m, d = x.shape
    return pl.pallas_call(
        _cast_kernel,
        out_shape=jax.ShapeDtypeStruct((m, d), BF16),
        grid=(m // tm,),
        in_specs=[pl.BlockSpec((tm, d), lambda i: (i, 0))],
        out_specs=pl.BlockSpec((tm, d), lambda i: (i, 0)),
        compiler_params=_params("parallel"),
        name="cast_bf16",
    )(x)


def _mm_kernel(a_ref, w_ref, o_ref):
    a = a_ref[...].astype(BF16)
    o_ref[...] = _dot(a, w_ref[...].astype(BF16)).astype(o_ref.dtype)


def _matmul(a, w, *, tm, tn, out_dtype, name):
    m, k = a.shape
    n = w.shape[1]
    return pl.pallas_call(
        _mm_kernel,
        out_shape=jax.ShapeDtypeStruct((m, n), out_dtype),
        grid=(n // tn, m // tm),
        in_specs=[pl.BlockSpec((tm, k), lambda j, i: (i, 0)),
                  pl.BlockSpec((k, tn), lambda j, i: (0, j))],
        out_specs=pl.BlockSpec((tm, tn), lambda j, i: (i, j)),
        compiler_params=_params("parallel", "parallel"),
        name=name,
    )(a, w)


A_TQ = 2 * CHUNK
A_WIN = (LEFT_CHUNKS + 2) * CHUNK
A_VARIANTS = LEFT_CHUNKS * CHUNK // A_TQ + 1


A_BASE_W = A_WIN + A_TQ


def _band_bias_base(rel_bias):
    reach = A_WIN
    ext =jnp.pad(rel_bias.astype(F32), ((0, 0), (reach - MAX_REL, reach - MAX_REL)), mode="edge")
    rev = ext[:, ::-1]
    rows = []
    for v in range(A_VARIANTS):
        c = rev[:, reach - A_TQ * v - A_TQ: reach - A_TQ * v + A_WIN]
        rows.append(jnp.concatenate([c[:, A_TQ:], c[:, :A_TQ]], axis=1))
    return jnp.stack(rows)[:, :, None, :]


def _attn_a_kernel(q_ref, k_ref, v_ref, base_ref, o_ref, tb_ref, *, heads, dh, scale):
    i = pl.program_id(2)

    @pl.when(i == 0)
    def _():
        r = lax.broadcasted_iota(I32, (A_TQ, A_WIN), 0)
        j = lax.broadcasted_iota(I32, (A_TQ, A_WIN), 1)
        for var in range(A_VARIANTS):
            cdiff = (A_TQ * var + r) // CHUNK - j // CHUNK
            valid = (cdiff >= 0) & (cdiff <= LEFT_CHUNKS)
            for h in range(heads):
                rows = jnp.broadcast_to(base_ref[var, h], (A_TQ, A_BASE_W))
                toeplitz = pltpu.roll(rows, 0, 1, stride=1, stride_axis=0)[:, :A_WIN]
                tb_ref[var, h] = jnp.where(valid, toeplitz, MASK_VALUE)

    var = jnp.minimum(i, A_VARIANTS - 1)
    start = pl.multiple_of(jnp.maximum(i - (A_VARIANTS - 1), 0) * A_TQ, A_TQ)
    for h in range(heads):
        cs = slice(h * dh, (h + 1) * dh)
        q = q_ref[:, cs]
        k = k_ref[pl.ds(start, A_WIN), cs]
        v = v_ref[pl.ds(start, A_WIN), cs]
        s = lax.dot_general(q, k, NT_DIMS, preferred_element_type=F32) * scale + tb_ref[var, h]
        m = jnp.max(s, axis=-1, keepdims=True)
        p = jnp.exp(s - m)
        l = jnp.sum(p, axis=-1, keepdims=True)
        o = _dot(p.astype(BF16), v)
        o_ref[:, cs] = (o / l).astype(o_ref.dtype)


def _attention_a(proj, base, *, batch, seq, heads, dh, col_q, col_k, col_v, heads_per_step=8):
    n = proj.shape[0]
    gw = heads_per_step * dh
    n_groups = heads // heads_per_step
    n_qb = seq // A_TQ
    kern = functools.partial(_attn_a_kernel, heads=heads_per_step, dh=dh, scale=dh ** -0.5)
    return pl.pallas_call(
        kern,
        out_shape=jax.ShapeDtypeStruct((n, heads * dh), BF16),
        grid=(batch, n_groups, n_qb),
        in_specs=[
            pl.BlockSpec((A_TQ, gw), lambda b, g, i: (b * n_qb + i, col_q // gw + g)),
            pl.BlockSpec((seq, gw), lambda b, g, i: (b, col_k // gw + g)),
            pl.BlockSpec((seq, gw), lambda b, g, i: (b, col_v // gw + g)),
            pl.BlockSpec((A_VARIANTS, heads_per_step, 1, A_BASE_W), lambda b, g, i: (0, g, 0, 0)),
        ],
        out_specs=pl.BlockSpec((A_TQ, gw), lambda b, g, i: (b * n_qb + i, g)),
        scratch_shapes=[pltpu.VMEM((A_VARIANTS, heads_per_step, A_TQ, A_WIN), F32)],
        compiler_params=_params("parallel", "parallel", "arbitrary"),
        name="attn_band",
    )(proj, proj, proj, base)


B_TQ = 256


def _rope_tables(seq, dim):
    inv = 1.0 / (ROPE_THETA ** (jnp.arange(0, dim, 2, dtype=F32) / dim))
    ang = jnp.arange(seq, dtype=F32)[:, None] * inv[None, :]
    ang = jnp.concatenate([ang, ang], -1)
    sign = jnp.where(jnp.arange(dim) < dim // 2, -1.0, 1.0).astype(F32)
    return jnp.cos(ang), jnp.sin(ang) * sign[None, :]


def _rope(x, cos, sin_signed):
    return x * cos + pltpu.roll(x, x.shape[1] // 2, 1) * sin_signed


def _attn_b_kernel(q_ref, k_ref, v_ref, cos_ref, sin_ref, lam_ref, g_ref, o_ref, krot_ref,
                   *, dh, scale, lambda_init):
    qi = pl.program_id(2)
    seq = k_ref.shape[0]

    @pl.when(qi == 0)
    def _():
        for m in range(2):
            kf = k_ref[:, m * dh:(m + 1) * dh].astype(F32)
            krot_ref[m] = _rope(kf, cos_ref[...], sin_ref[...]).astype(BF16)

    lv = lam_ref[...]
    lam = (jnp.exp(jnp.sum(lv[0:1] * lv[1:2], axis=-1, keepdims=True))
           - jnp.exp(jnp.sum(lv[2:3] * lv[3:4], axis=-1, keepdims=True)) + lambda_init)

    def block(blk):
        row0 = blk * B_TQ
        kl = row0 + B_TQ
        cos_q = cos_ref[row0:kl, :]
        sin_q = sin_ref[row0:kl, :]
        q_chunk = (row0 + lax.broadcasted_iota(I32, (B_TQ, kl), 0)) // CHUNK
        k_chunk = lax.broadcasted_iota(I32, (B_TQ, kl), 1) // CHUNK
        allowed = k_chunk <= q_chunk
        probs = []
        for m in range(2):
            qf = q_ref[:, m * dh:(m + 1) * dh].astype(F32)
            qr = _rope(qf, cos_q, sin_q).astype(BF16)
            s = lax.dot_general(qr, krot_ref[m, :kl, :], NT_DIMS, preferred_element_type=F32) * scale
            s = jnp.where(allowed, s, MASK_VALUE)
            e = jnp.exp(s - jnp.max(s, axis=-1, keepdims=True))
            probs.append(e / jnp.sum(e, axis=-1, keepdims=True))
        w = (probs[0] - lam * probs[1]).astype(BF16)
        o = _dot(w, v_ref[:kl, :])
        ms = jnp.mean(o * o, axis=-1, keepdims=True)
        y = o * lax.rsqrt(ms + RMS_EPS) * g_ref[...] * (1.0 - lambda_init)
        o_ref[...] = y.astype(o_ref.dtype)

    for blk in range(seq // B_TQ):
        pl.when(qi == blk)(functools.partial(block, blk))


def _attention_b(proj, cos, sin_signed, lam_vecs, norm_g, *, batch, seq, heads, dh,
                 col_q, col_k, col_v, lambda_init):
    n = proj.shape[0]
    hw = 2 * dh
    n_qb = seq // B_TQ
    kern = functools.partial(_attn_b_kernel, dh=dh, scale=dh ** -0.5, lambda_init=lambda_init)
    return pl.pallas_call(
        kern,
        out_shape=jax.ShapeDtypeStruct((n, heads * hw), BF16),
        grid=(batch, heads, n_qb),
        in_specs=[
            pl.BlockSpec((B_TQ, hw), lambda b, h, i: (b * n_qb + i, col_q // hw + h)),
            pl.BlockSpec((seq, hw), lambda b, h, i: (b, col_k // hw + h)),
            pl.BlockSpec((seq, hw), lambda b, h, i: (b, col_v // hw + h)),
            pl.BlockSpec((seq, dh), lambda b, h, i: (0, 0)),
            pl.BlockSpec((seq, dh), lambda b, h, i: (0, 0)),
            pl.BlockSpec((4, dh), lambda b, h, i: (0, 0)),
            pl.BlockSpec((1, hw), lambda b, h, i: (0, 0)),
        ],
        out_specs=pl.BlockSpec((B_TQ, hw), lambda b, h, i: (b * n_qb + i, h)),
        scratch_shapes=[pltpu.VMEM((2, seq, dh), BF16)],
        compiler_params=_params("parallel", "parallel", "arbitrary"),
        name="attn_diff",
    )(proj, proj, proj, cos, sin_signed, lam_vecs, norm_g)


C_TQ = 512


def _attn_c_kernel(q_ref, k_ref, v_ref, o_ref, *, scale):
    s = lax.dot_general(q_ref[...], k_ref[...], NT_DIMS, preferred_element_type=F32) * scale
    e = jnp.exp(s - jnp.max(s, axis=-1, keepdims=True))
    p = (e / jnp.sum(e, axis=-1, keepdims=True)).astype(BF16)
    o_ref[...] = _dot(p, v_ref[...]).astype(o_ref.dtype)


def _attention_c(proj, ckv, *, batch, seq, mem_len, heads, dh, col_q):
    n = proj.shape[0]
    n_qb = seq // C_TQ
    kern = functools.partial(_attn_c_kernel, scale=dh ** -0.5)
    return pl.pallas_call(
        kern,
        out_shape=jax.ShapeDtypeStruct((n, heads * dh), BF16),
        grid=(batch, heads, n_qb),
        in_specs=[
            pl.BlockSpec((C_TQ, dh), lambda b, h, i: (b * n_qb + i, col_q // dh + h)),
            pl.BlockSpec((mem_len, dh), lambda b, h, i: (b, h)),
            pl.BlockSpec((mem_len, dh), lambda b, h, i: (b, heads + h)),
        ],
        out_specs=pl.BlockSpec((C_TQ, dh), lambda b, h, i: (b * n_qb + i, h)),
        compiler_params=_params("parallel", "parallel", "parallel"),
        name="attn_mem",
    )(proj, ckv, ckv)


def _merge_kernel(x_ref, ya_ref, yb_ref, yc_ref, wga_ref, wgb_ref, wgc_ref,
                  bga_ref, bgb_ref, bgc_ref, pa_ref, pb_ref, pc_ref, o_ref):
    x = x_ref[...]
    acc = None
    for wg, bg, y, p in ((wga_ref, bga_ref, ya_ref, pa_ref),
                         (wgb_ref, bgb_ref, yb_ref, pb_ref),
                         (wgc_ref, bgc_ref, yc_ref, pc_ref)):
        gate = jax.nn.sigmoid(_dot(x, wg[...].astype(BF16)) + bg[...])
        term = gate * _dot(y[...], p[...].astype(BF16))
        acc = term if acc is None else acc + term
    o_ref[...] = acc.astype(o_ref.dtype)


def _gated_merge(xb, ya, yb, yc, w_gates, b_gates, pa, pb, pc, *, tm, tn):
    n, d = xb.shape
    nj = d // tn
    row = lambda width: pl.BlockSpec((tm, width), lambda j, i: (i, 0))
    gate_w = lambda br: pl.BlockSpec((d, tn), lambda j, i, br=br: (0, br * nj + j))
    gate_b = lambda br: pl.BlockSpec((1, tn), lambda j, i, br=br: (0, br * nj + j))
    branch_w = lambda width: pl.BlockSpec((width, tn), lambda j, i: (0, j))
    return pl.pallas_call(
        _merge_kernel,
        out_shape=jax.ShapeDtypeStruct((n, d), BF16),
        grid=(nj, n // tm),
        in_specs=[row(d), row(ya.shape[1]), row(yb.shape[1]), row(yc.shape[1]),
                  gate_w(0), gate_w(1), gate_w(2), gate_b(0), gate_b(1), gate_b(2),
                  branch_w(pa.shape[0]), branch_w(pb.shape[0]), branch_w(pc.shape[0])],
        out_specs=pl.BlockSpec((tm, tn), lambda j, i: (i, j)),
        compiler_params=_params("parallel", "parallel"),
        name="gated_merge",
    )(xb, ya, yb, yc, w_gates, w_gates, w_gates, b_gates, b_gates, b_gates, pa, pb, pc)


def _layer_norm(z, g, b):
    mu = jnp.mean(z, axis=-1, keepdims=True)
    zc = z - mu
    var = jnp.mean(zc * zc, axis=-1, keepdims=True)
    return zc * lax.rsqrt(var + LN_EPS) * g + b


def _ln_router_kernel(x_ref, m_ref, g_ref, b_ref, wr_ref, br_ref,
                      h_ref, hp_ref, idx_ref, gate_ref, rank_ref, cnt_ref, carry_ref,
                      *, alpha, n_exp):
    @pl.when(pl.program_id(0) == 0)
    def _():
        carry_ref[...] = jnp.zeros_like(carry_ref)

    tm = x_ref.shape[0]
    h = _layer_norm(alpha * x_ref[...] + m_ref[...], g_ref[...], b_ref[...])
    h_ref[...] = h
    hp_ref[...] = _pack_halves(h)

    h_hi = h.astype(BF16)
    h_lo = (h - h_hi.astype(F32)).astype(BF16)
    w = wr_ref[...]
    r1 = _dot(h_hi, w)
    logits = r1[:, :LANES] + r1[:, LANES:] + _dot(h_lo, w[:, :LANES]) + br_ref[...]

    lane = lax.broadcasted_iota(I32, (tm, LANES), 1)
    lane_f = lane.astype(F32)
    cur = jnp.where(lane < n_exp, logits, -jnp.inf)
    vals, idxs = [], []
    for _ in range(TOP_K):
        mx = jnp.max(cur, axis=-1, keepdims=True)
        ix = jnp.min(jnp.where(cur == mx, lane_f, float(LANES)), axis=-1, keepdims=True).astype(I32)
        vals.append(mx)
        idxs.append(ix)
        cur = jnp.where(lane == ix, -jnp.inf, cur)
    exps = [jnp.exp(v - vals[0]) for v in vals]
    den = exps[0]
    for e in exps[1:]:
        den = den + e

    tri = (lax.broadcasted_iota(I32, (tm, tm), 0) > lax.broadcasted_iota(I32, (tm, tm), 1)).astype(BF16)
    carry = carry_ref[...]
    idx_out = jnp.zeros((tm, LANES), I32)
    gate_out = jnp.zeros((tm, LANES), F32)
    rank_out = jnp.zeros((tm, LANES), I32)
    for k in range(TOP_K):
        onehot = (lane == idxs[k]).astype(F32)
        before = _dot(tri, onehot.astype(BF16)) + carry
        rank = jnp.sum(onehot * before, axis=-1, keepdims=True)
        carry = carry + jnp.sum(onehot, axis=0, keepdims=True)
        idx_out = jnp.where(lane == k, idxs[k], idx_out)
        gate_out = jnp.where(lane == k, exps[k] / den, gate_out)
        rank_out = jnp.where(lane == k, rank.astype(I32), rank_out)
    carry_ref[...] = carry
    idx_ref[...] = idx_out
    gate_ref[...] = gate_out
    rank_ref[...] = rank_out
    cnt_ref[...] = carry


def _ln_router(x, m, g, b, wr_split, br_pad, *, alpha, n_exp, tm):
    n, d = x.shape
    row = pl.BlockSpec((tm, d), lambda i: (i, 0))
    vec = pl.BlockSpec((1, d), lambda i: (0, 0))
    small = pl.BlockSpec((tm, LANES), lambda i: (i, 0))
    kern = functools.partial(_ln_router_kernel, alpha=alpha, n_exp=n_exp)
    return pl.pallas_call(
        kern,
        out_shape=(jax.ShapeDtypeStruct((n, d), F32),
                   jax.ShapeDtypeStruct((n, d // 2), U32),
                   jax.ShapeDtypeStruct((n, LANES), I32),
                   jax.ShapeDtypeStruct((n, LANES), F32),
                   jax.ShapeDtypeStruct((n, LANES), I32),
                   jax.ShapeDtypeStruct((1, LANES), F32)),
        grid=(n // tm,),
        in_specs=[row, row, vec, vec,
                  pl.BlockSpec((d, 2 * LANES), lambda i: (0, 0)),
                  pl.BlockSpec((1, LANES), lambda i: (0, 0))],
        out_specs=(row, pl.BlockSpec((tm, d // 2), lambda i: (i, 0)), small, small, small,
                   pl.BlockSpec((1, LANES), lambda i: (0, 0))),
        scratch_shapes=[pltpu.VMEM((1, LANES), F32)],
        compiler_params=_params("arbitrary"),
        name="ln_router",
    )(x, m, g, b, wr_split, br_pad)


MOE_RB = 256
ROW_DMA_PRIORITY = 1
CAST_ROWS = 512
W_SPLIT = 4


def _cumsum_small(x):
    n = x.shape[0]
    keep = jnp.arange(n)[:, None] >= jnp.arange(n)[None, :]
    return jnp.sum(jnp.where(keep, x[None, :], 0), axis=1).astype(x.dtype)


def _lookup(table, idx):
    hit = idx[..., None] == jnp.arange(table.shape[0], dtype=idx.dtype)
    return jnp.sum(jnp.where(hit, table, 0), axis=-1).astype(table.dtype)


def _group_layout(counts, cap):
    padded = (counts + MOE_RB - 1) // MOE_RB * MOE_RB
    pend = _cumsum_small(padded)
    pstart = pend - padded
    tail = jnp.stack([pend[-1], (cap - pend[-1]) // MOE_RB])
    return pstart, padded // MOE_RB, pstart + counts, padded - counts, tail


def _tail_rows(tail, b):
    return pl.ds(pl.multiple_of(tail[0] + b * MOE_RB, MOE_RB), MOE_RB)


def _zero_tail(tail, zero_view, dst_view, sem):
    def start(b, carry):
        pltpu.make_async_copy(zero_view, dst_view(_tail_rows(tail, b)), sem).start()
        return carry

    lax.fori_loop(0, tail[1], start, 0)

    def done(b, carry):
        pltpu.make_async_copy(zero_view, dst_view(_tail_rows(tail, 0)), sem).wait()
        return carry

    lax.fori_loop(0, tail[1], done, 0)


def _row_copy(src, src_row, dst, dst_row, sem):
    return pltpu.make_async_copy(src.at[pl.ds(src_row, 1)], dst.at[pl.ds(dst_row, 1)], sem)


def _dispatch_kernel(pad_lo, pad_n, tail, dest_ref, hp_ref, xg_ref, zero_ref, sem, zsem):
    tm = hp_ref.shape[0]
    n_exp = pad_lo.shape[0]

    @pl.when(pl.program_id(0) == 0)
    def _():
        zero_ref[...] = jnp.zeros_like(zero_ref)
        _zero_tail(tail, zero_ref, lambda rows: xg_ref.at[rows], zsem)

        def fill(e, carry):
            def one(r, c):
                _row_copy(zero_ref, 0, xg_ref, pad_lo[e] + r, zsem).start()
                return c
            return lax.fori_loop(0, pad_n[e], one, carry)

        lax.fori_loop(0, n_exp, fill, 0)

        def fill_done(e, carry):
            def one(r, c):
                _row_copy(zero_ref, 0, xg_ref, 0, zsem).wait()
                return c
            return lax.fori_loop(0, pad_n[e], one, carry)

        lax.fori_loop(0, n_exp, fill_done, 0)

    def issue(t, carry):
        for k in range(TOP_K):
            _row_copy(hp_ref, t, xg_ref, dest_ref[0, t * TOP_K + k], sem).start(priority=k % 2)
        return carry

    lax.fori_loop(0, tm, issue, 0)

    def drain(t, carry):
        for k in range(TOP_K):
            _row_copy(hp_ref, 0, xg_ref, 0, sem).wait()
        return carry

    lax.fori_loop(0, tm, drain, 0)


def _dispatch(dest, hp, pad_lo, pad_n, tail, *, tm, cap):
    n, w = hp.shape
    dest3 = dest.reshape(n // tm, 1, tm * TOP_K)
    grid_spec = pltpu.PrefetchScalarGridSpec(
        num_scalar_prefetch=3,
        grid=(n // tm,),
        in_specs=[pl.BlockSpec((None, 1, tm * TOP_K), lambda i, *_: (i, 0, 0), memory_space=pltpu.SMEM),
                  pl.BlockSpec((tm, w), lambda i, *_: (i, 0))],
        out_specs=pl.BlockSpec(memory_space=pl.ANY),
        scratch_shapes=[pltpu.VMEM((MOE_RB, w), U32), pltpu.SemaphoreType.DMA(()), pltpu.SemaphoreType.DMA(())],
    )
    return pl.pallas_call(
        _dispatch_kernel,
        out_shape=jax.ShapeDtypeStruct((cap, w), U32),
        grid_spec=grid_spec,
        compiler_params=_params("arbitrary"),
        name="dispatch",
    )(pad_lo, pad_n, tail, dest3, hp)


SEL_W = 512


def _even_lane_selector():
    r = jnp.arange(SEL_W)[:, None]
    c = jnp.arange(SEL_W // 2)[None, :]
    return (r == 2 * c).astype(BF16)


def _swiglu_even_lanes(h, sel):
    tn = h.shape[1]
    glu = jnp.minimum(h, SWIGLU_LIMIT)
    lin = jnp.clip(h, -SWIGLU_LIMIT, SWIGLU_LIMIT) + 1.0
    gact = glu * jax.nn.sigmoid(SWIGLU_ALPHA * glu)
    parts = []
    for c in range(tn // LANES):
        cs = slice(c * LANES, (c + 1) * LANES)
        parts.append(gact[:, cs] * pltpu.roll(lin[:, cs], LANES - 1, 1))
    inter = jnp.concatenate(parts, axis=1).astype(BF16)
    acts = [_dot(inter[:, s * SEL_W:(s + 1) * SEL_W], sel) for s in range(tn // SEL_W)]
    return jnp.concatenate(acts, axis=1)


def _up_kernel(pstart, nblk, tail, xg_ref, *rest, n_chunks):
    w1_refs = rest[:W_SPLIT]
    b1_ref, sel_ref, act_ref, xbuf, hbuf, obuf, wbf, sem_in, sem_out = rest[W_SPLIT:]
    c = pl.program_id(0)
    e = pl.program_id(1)
    n = nblk[e]
    base = pstart[e]
    rb = xbuf.shape[1]
    slab = w1_refs[0].shape[0]
    half = slab * W_SPLIT // 2
    tno = obuf.shape[2]

    def rows(blk):
        return pl.ds(pl.multiple_of(base + blk * rb, rb), rb)

    def in_copy(blk, slot):
        return pltpu.make_async_copy(xg_ref.at[rows(blk)], xbuf.at[slot], sem_in.at[slot])

    def out_start(blk, slot):
        for cc in range(n_chunks):
            @pl.when(c == cc)
            def _(cc=cc):
                pltpu.make_async_copy(obuf.at[slot], act_ref.at[rows(blk), pl.ds(cc * tno, tno)],
                                      sem_out.at[slot]).start()

    def out_wait(slot):
        pltpu.make_async_copy(obuf.at[slot], act_ref.at[pl.ds(0, rb), pl.ds(0, tno)], sem_out.at[slot]).wait()

    def matmul(slot):
        xa, xb = _unpack_halves(xbuf[slot])
        hbuf[slot] = (_dot(xa.astype(BF16), wbf[:half, :]) + _dot(xb.astype(BF16), wbf[half:, :])
                      + b1_ref[...])

    def matmul_first():
        xs = [x.astype(BF16) for x in _unpack_halves(xbuf[0])]
        acc = jnp.broadcast_to(b1_ref[...], hbuf.shape[1:])
        for part in range(2):
            for k in range(half // CAST_ROWS):
                r0 = part * half + k * CAST_ROWS
                wk = w1_refs[r0 // slab][r0 % slab:r0 % slab + CAST_ROWS, :].astype(BF16)
                wbf[r0:r0 + CAST_ROWS, :] = wk
                acc = acc + _dot(xs[part][:, k * CAST_ROWS:(k + 1) * CAST_ROWS], wk)
        hbuf[0] = acc

    def epilogue(slot):
        obuf[slot] = _swiglu_even_lanes(hbuf[slot], sel_ref[...]).astype(obuf.dtype)

    @pl.when(n > 0)
    def _():
        in_copy(0, 0).start(priority=ROW_DMA_PRIORITY)
        in_copy(0, 0).wait()

        @pl.when(n > 1)
        def _():
            in_copy(1, 1).start(priority=ROW_DMA_PRIORITY)

        matmul_first()

        def body(blk, carry):
            slot = lax.rem(blk, 2)
            prev = 1 - slot
            in_copy(blk, slot).wait()

            @pl.when(blk + 1 < n)
            def _():
                in_copy(blk + 1, prev).start(priority=ROW_DMA_PRIORITY)

            @pl.when(blk >= 3)
            def _():
                out_wait(prev)

            epilogue(prev)
            matmul(slot)
            out_start(blk - 1, prev)
            return carry

        lax.fori_loop(1, n, body, 0)

        last = n - 1
        ls = lax.rem(last, 2)

        @pl.when(last >= 2)
        def _():
            out_wait(ls)

        epilogue(ls)
        out_start(last, ls)

        @pl.when(last >= 1)
        def _():
            out_wait(1 - ls)

        out_wait(ls)

    @pl.when(e == pl.num_programs(1) - 1)
    def _():
        obuf[0] = jnp.zeros(obuf.shape[1:], obuf.dtype)
        for cc in range(n_chunks):
            @pl.when(c == cc)
            def _(cc=cc):
                _zero_tail(tail, obuf.at[0], lambda r: act_ref.at[r, pl.ds(cc * tno, tno)], sem_out.at[0])


def _expert_up(pstart, nblk, tail, xg, w1, b1, *, tn):
    cap, w = xg.shape
    n_exp, d, f2 = w1.shape
    n_chunks = f2 // tn
    grid_spec = pltpu.PrefetchScalarGridSpec(
        num_scalar_prefetch=3,
        grid=(n_chunks, n_exp),
        in_specs=[pl.BlockSpec(memory_space=pl.ANY)]
        + [pl.BlockSpec((None, d // W_SPLIT, tn), lambda c, e, *_, r=r: (e, r, c)) for r in range(W_SPLIT)]
        + [pl.BlockSpec((None, 1, tn), lambda c, e, *_: (e, 0, c)),
           pl.BlockSpec((SEL_W, SEL_W // 2), lambda c, e, *_: (0, 0))],
        out_specs=pl.BlockSpec(memory_space=pl.ANY),
        scratch_shapes=[pltpu.VMEM((2, MOE_RB, w), U32),
                        pltpu.VMEM((2, MOE_RB, tn), F32),
                        pltpu.VMEM((2, MOE_RB, tn // 2), BF16),
                        pltpu.VMEM((d, tn), BF16),
                        pltpu.SemaphoreType.DMA((2,)),
                        pltpu.SemaphoreType.DMA((2,))],
    )
    return pl.pallas_call(
        functools.partial(_up_kernel, n_chunks=n_chunks),
        out_shape=jax.ShapeDtypeStruct((cap, f2 // 2), BF16),
        grid_spec=grid_spec,
        compiler_params=_params("arbitrary", "arbitrary"),
        name="expert_up",
    )(pstart, nblk, tail, xg, *([w1] * W_SPLIT), b1.reshape(n_exp, 1, f2), _even_lane_selector())


def _down_kernel(pstart, nblk, tail, act_ref, *rest, n_chunks):
    w2_refs = rest[:W_SPLIT]
    b2_ref, y_ref, abuf, obuf, wbf, sem_in, sem_out = rest[W_SPLIT:]
    c = pl.program_id(0)
    e = pl.program_id(1)
    n = nblk[e]
    base = pstart[e]
    rb = abuf.shape[1]
    tno = obuf.shape[2]

    def rows(blk):
        return pl.ds(pl.multiple_of(base + blk * rb, rb), rb)

    def in_copy(blk, slot):
        return pltpu.make_async_copy(act_ref.at[rows(blk)], abuf.at[slot], sem_in.at[slot])

    def out_start(blk, slot):
        for cc in range(n_chunks):
            @pl.when(c == cc)
            def _(cc=cc):
                pltpu.make_async_copy(obuf.at[slot], y_ref.at[rows(blk), pl.ds(cc * tno, tno)],
                                      sem_out.at[slot]).start()

    def out_wait(slot):
        pltpu.make_async_copy(obuf.at[slot], y_ref.at[pl.ds(0, rb), pl.ds(0, tno)], sem_out.at[slot]).wait()

    @pl.when(n > 0)
    def _():
        in_copy(0, 0).start(priority=ROW_DMA_PRIORITY)
        in_copy(0, 0).wait()

        @pl.when(n > 1)
        def _():
            in_copy(1, 1).start(priority=ROW_DMA_PRIORITY)

        a0 = abuf[0]
        slab = w2_refs[0].shape[0]
        acc = jnp.broadcast_to(b2_ref[...], (rb, wbf.shape[1]))
        for r, w2_ref in enumerate(w2_refs):
            ks = slice(r * slab, (r + 1) * slab)
            wk = w2_ref[...].astype(BF16)
            wbf[ks, :] = wk
            acc = acc + _dot(a0[:, ks], wk)
        obuf[0] = _pack_halves(acc)
        out_start(0, 0)

        def body(blk, carry):
            slot = lax.rem(blk, 2)
            in_copy(blk, slot).wait()

            @pl.when(blk + 1 < n)
            def _():
                in_copy(blk + 1, 1 - slot).start(priority=ROW_DMA_PRIORITY)

            @pl.when(blk >= 2)
            def _():
                out_wait(slot)

            obuf[slot] = _pack_halves(_dot(abuf[slot], wbf[...]) + b2_ref[...])
            out_start(blk, slot)
            return carry

        lax.fori_loop(1, n, body, 0)

        last = n - 1

        @pl.when(last >= 1)
        def _():
            out_wait(lax.rem(last + 1, 2))

        out_wait(lax.rem(last, 2))

    @pl.when(e == pl.num_programs(1) - 1)
    def _():
        obuf[0] = jnp.zeros(obuf.shape[1:], obuf.dtype)
        for cc in range(n_chunks):
            @pl.when(c == cc)
            def _(cc=cc):
                _zero_tail(tail, obuf.at[0], lambda r: y_ref.at[r, pl.ds(cc * tno, tno)], sem_out.at[0])


def _expert_down(pstart, nblk, tail, act, w2, b2, *, tn):
    cap, f = act.shape
    n_exp, _, d = w2.shape
    n_chunks = d // tn
    grid_spec = pltpu.PrefetchScalarGridSpec(
        num_scalar_prefetch=3,
        grid=(n_chunks, n_exp),
        in_specs=[pl.BlockSpec(memory_space=pl.ANY)]
        + [pl.BlockSpec((None, f // W_SPLIT, tn), lambda c, e, *_, r=r: (e, r, c)) for r in range(W_SPLIT)]
        + [pl.BlockSpec((None, 1, tn), lambda c, e, *_: (e, 0, c))],
        out_specs=pl.BlockSpec(memory_space=pl.ANY),
        scratch_shapes=[pltpu.VMEM((2, MOE_RB, f), BF16),
                        pltpu.VMEM((2, MOE_RB, tn // 2), U32),
                        pltpu.VMEM((f, tn), BF16),
                        pltpu.SemaphoreType.DMA((2,)),
                        pltpu.SemaphoreType.DMA((2,))],
    )
    return pl.pallas_call(
        functools.partial(_down_kernel, n_chunks=n_chunks),
        out_shape=jax.ShapeDtypeStruct((cap, d // 2), U32),
        grid_spec=grid_spec,
        compiler_params=_params("arbitrary", "arbitrary"),
        name="expert_down",
    )(pstart, nblk, tail, act, *([w2] * W_SPLIT), b2.reshape(n_exp, 1, d))


def _combine_kernel(dest_ref, next_ref, h_ref, gate_ref, g_ref, b_ref, y_ref, o_ref, buf_ref, sem,
                    *, alpha, chunk):
    tm = h_ref.shape[0]
    i = pl.program_id(0)
    slot = lax.rem(i, 2)

    def gather(idx_ref, s):
        def issue(t, carry):
            for k in range(TOP_K):
                pltpu.make_async_copy(y_ref.at[pl.ds(idx_ref[0, t * TOP_K + k], 1)],
                                      buf_ref.at[s, k, pl.ds(t, 1)], sem.at[s]).start(priority=k % 2)
            return carry
        lax.fori_loop(0, tm, issue, 0)

    @pl.when(i == 0)
    def _():
        gather(dest_ref, 0)

    @pl.when(i + 1 < pl.num_programs(0))
    def _():
        gather(next_ref, 1 - slot)

    def drain(t, carry):
        for k in range(TOP_K):
            pltpu.make_async_copy(y_ref.at[pl.ds(0, 1)], buf_ref.at[slot, k, pl.ds(0, 1)], sem.at[slot]).wait()
        return carry

    lax.fori_loop(0, tm, drain, 0)

    gates = gate_ref[...]
    acc_hi = None
    acc_lo = None
    for k in range(TOP_K):
        hi, lo = _unpack_halves(buf_ref[slot, k])
        gk = gates[:, k:k + 1]
        acc_hi = gk * hi if acc_hi is None else acc_hi + gk * hi
        acc_lo = gk * lo if acc_lo is None else acc_lo + gk * lo
    hw = chunk // 2
    pieces = []
    for c in range(acc_hi.shape[1] // hw):
        pieces.append(acc_hi[:, c * hw:(c + 1) * hw])
        pieces.append(acc_lo[:, c * hw:(c + 1) * hw])
    ffn = jnp.concatenate(pieces, axis=1)
    o_ref[...] = _layer_norm(alpha * h_ref[...] + ffn, g_ref[...], b_ref[...]).astype(o_ref.dtype)


def _combine(dest, h, gates, g, b, y, *, alpha, tm, chunk):
    n, d = h.shape
    n_steps = n // tm
    dest3 = dest.reshape(n_steps, 1, tm * TOP_K)
    kern = functools.partial(_combine_kernel, alpha=alpha, chunk=chunk)
    return pl.pallas_call(
        kern,
        out_shape=jax.ShapeDtypeStruct((n, d), F32),
        grid=(n_steps,),
        in_specs=[pl.BlockSpec((None, 1, tm * TOP_K), lambda i: (i, 0, 0), memory_space=pltpu.SMEM),
                  pl.BlockSpec((None, 1, tm * TOP_K), lambda i: (jnp.minimum(i + 1, n_steps - 1), 0, 0),
                               memory_space=pltpu.SMEM),
                  pl.BlockSpec((tm, d), lambda i: (i, 0)),
                  pl.BlockSpec((tm, LANES), lambda i: (i, 0)),
                  pl.BlockSpec((1, d), lambda i: (0, 0)),
                  pl.BlockSpec((1, d), lambda i: (0, 0)),
                  pl.BlockSpec(memory_space=pl.ANY)],
        out_specs=pl.BlockSpec((tm, d), lambda i: (i, 0)),
        scratch_shapes=[pltpu.VMEM((2, TOP_K, tm, d // 2), U32), pltpu.SemaphoreType.DMA((2,))],
        compiler_params=_params("arbitrary"),
        name="combine_ln",
    )(dest3, dest3, h, gates, g, b, y)


def _tiles(n_tokens, d_model, d_expert):
    return dict(
        cast_tm=min(512, n_tokens),
        proj=dict(tm=min(1024, n_tokens), tn=512),
        memkv=dict(tm=512, tn=512),
        merge=dict(tm=min(512, n_tokens), tn=256),
        out=dict(tm=min(1024, n_tokens), tn=512),
        ln_tm=min(256, n_tokens),
        dispatch_tm=min(256, n_tokens),
        up_tn=min(1024, 2 * d_expert),
        down_tn=min(2048, d_model),
        combine_tm=min(128, n_tokens),
    )


def _layer(h, mem2, lw, *, batch, seq, mem_len, lambda_init, alpha):
    n, d = h.shape
    t = _tiles(n, d, lw["w_mlp2"].shape[1])
    a_width = A_HEADS * A_HEAD_DIM
    b_width = B_HEADS * 2 * B_HEAD_DIM
    col = dict(a_q=0, a_k=a_width, a_v=2 * a_width, b_q=3 * a_width, b_k=3 * a_width + b_width,
               b_v=3 * a_width + 2 * b_width, c_q=3 * a_width + 3 * b_width)

    xb = _cast_bf16(h, t["cast_tm"])
    proj = _matmul(xb, lw["w_in"], out_dtype=BF16, name="in_proj", **t["proj"])
    ckv = _matmul(mem2, lw["w_mem_kv"], out_dtype=BF16, name="mem_kv",
                  tm=min(t["memkv"]["tm"], mem2.shape[0]), tn=t["memkv"]["tn"])

    ya = _attention_a(proj, _band_bias_base(lw["rel_bias"]), batch=batch, seq=seq,
                      heads=A_HEADS, dh=A_HEAD_DIM, col_q=col["a_q"], col_k=col["a_k"], col_v=col["a_v"])
    cos, sin_signed = _rope_tables(seq, B_HEAD_DIM)
    lam_vecs = jnp.stack([lw["lambda_q1"], lw["lambda_k1"], lw["lambda_q2"], lw["lambda_k2"]]).astype(F32)
    yb = _attention_b(proj, cos, sin_signed, lam_vecs, lw["diff_norm_g"].reshape(1, -1),
                      batch=batch, seq=seq, heads=B_HEADS, dh=B_HEAD_DIM,
                      col_q=col["b_q"], col_k=col["b_k"], col_v=col["b_v"], lambda_init=lambda_init)
    yc = _attention_c(proj, ckv, batch=batch, seq=seq, mem_len=mem_len, heads=C_HEADS,
                      dh=C_HEAD_DIM, col_q=col["c_q"])

    merged = _gated_merge(xb, ya, yb, yc, lw["w_gates"], lw["b_gates"].reshape(1, -1),
                          lw["w_branch_a"], lw["w_branch_b"], lw["w_branch_c"], **t["merge"])
    mix = _matmul(merged, lw["w_o"], out_dtype=F32, name="out_proj", **t["out"])

    n_exp = lw["w_router"].shape[1]
    wr = jnp.pad(lw["w_router"], ((0, 0), (0, LANES - n_exp)))
    wr_hi = wr.astype(BF16)
    wr_lo = (wr - wr_hi.astype(F32)).astype(BF16)
    br = jnp.pad(lw["b_router"], (0, LANES - n_exp)).reshape(1, LANES)
    h1, h1_packed, top_idx, gates, rank, cnt = _ln_router(
        h, mix, lw["ln1_g"].reshape(1, -1), lw["ln1_b"].reshape(1, -1),
        jnp.concatenate([wr_hi, wr_lo], axis=1), br, alpha=alpha, n_exp=n_exp, tm=t["ln_tm"])

    counts = cnt[0, :n_exp].astype(I32)
    cap = n * TOP_K + n_exp * MOE_RB
    pstart, nblk, pad_lo, pad_n, tail = _group_layout(counts, cap)
    dest = _lookup(pstart, top_idx[:, :TOP_K]) + rank[:, :TOP_K]

    xg = _dispatch(dest, h1_packed, pad_lo, pad_n, tail, tm=t["dispatch_tm"], cap=cap)
    act = _expert_up(pstart, nblk, tail, xg, lw["w_mlp1"], lw["b_mlp1"], tn=t["up_tn"])
    y = _expert_down(pstart, nblk, tail, act, lw["w_mlp2"], lw["b_mlp2"], tn=t["down_tn"])
    return _combine(dest, h1, gates, lw["ln2_g"].reshape(1, -1), lw["ln2_b"].reshape(1, -1), y,
                    alpha=alpha, tm=t["combine_tm"], chunk=t["down_tn"])


def kernel(x, mem, w_in, w_mem_kv, rel_bias, lambda_q1, lambda_k1, lambda_q2, lambda_k2, diff_norm_g,
           w_branch_a, w_branch_b, w_branch_c, w_gates, b_gates, w_o, ln1_g, ln1_b, w_router, b_router,
           w_mlp1, b_mlp1, w_mlp2, b_mlp2, ln2_g, ln2_b):
    batch, seq, d = x.shape
    mem_len = mem.shape[1]
    depth = w_in.shape[0]
    alpha = (2 * depth) ** 0.25
    stacked = dict(w_in=w_in, w_mem_kv=w_mem_kv, rel_bias=rel_bias, lambda_q1=lambda_q1,
                   lambda_k1=lambda_k1, lambda_q2=lambda_q2, lambda_k2=lambda_k2, diff_norm_g=diff_norm_g,
                   w_branch_a=w_branch_a, w_branch_b=w_branch_b, w_branch_c=w_branch_c, w_gates=w_gates,
                   b_gates=b_gates, w_o=w_o, ln1_g=ln1_g, ln1_b=ln1_b, w_router=w_router,
                   b_router=b_router, w_mlp1=w_mlp1, b_mlp1=b_mlp1, w_mlp2=w_mlp2, b_mlp2=b_mlp2,
                   ln2_g=ln2_g, ln2_b=ln2_b)
    h = x.reshape(batch * seq, d)
    mem2 = mem.reshape(batch * mem_len, d)
    for l in range(depth):
        lw = {name: w[l] for name, w in stacked.items()}
        lambda_init = 0.8 - 0.6 * math.exp(-0.3 * l)
        h = _layer(h, mem2, lw, batch=batch, seq=seq, mem_len=mem_len, lambda_init=lambda_init, alpha=alpha)
    return h.reshape(batch, seq, d)
```

```python
import functools
import math

import jax
import jax.numpy as jnp
from jax import lax
from jax.experimental import pallas as pl
from jax.experimental.pallas import tpu as pltpu

F32 = jnp.float32
BF16 = jnp.bfloat16
U32 = jnp.uint32
I32 = jnp.int32

CHUNK = 64
LEFT_CHUNKS = 8
MAX_REL = 128
A_HEADS = 16
A_HEAD_DIM = 128
B_HEADS = 4
B_HEAD_DIM = 128
C_HEADS = 4
C_HEAD_DIM = 256
N_BRANCHES = 3
ROPE_THETA = 10000.0
TOP_K = 4
SWIGLU_LIMIT = 7.0
SWIGLU_ALPHA = 1.702
LN_EPS = 1e-5
RMS_EPS = 1e-5
MASK_VALUE = -1e30

V7X_VMEM_BYTES = 64 * 1024 * 1024
V7X_VMEM_LIMIT = V7X_VMEM_BYTES - 8 * 1024 * 1024
LANES = 128

NT_DIMS = (((1,), (1,)), ((), ()))


def _params(*semantics):
    return pltpu.CompilerParams(dimension_semantics=semantics,
                                vmem_limit_bytes=V7X_VMEM_LIMIT)


def _dot(a, b):
    return jnp.dot(a, b, preferred_element_type=F32)


def _pack_halves(x):
    w = x.shape[1] // 2
    hi = lax.bitcast_convert_type(x[:, :w].astype(jnp.bfloat16).astype(F32), U32)
    lo = lax.bitcast_convert_type(x[:, w:].astype(jnp.bfloat16).astype(F32), U32)
    return hi | (lo >> 16)


def _unpack_halves(p):
    hi = lax.bitcast_convert_type(p & jnp.uint32(0xFFFF0000), F32)
    lo = lax.bitcast_convert_type(p << 16, F32)
    return hi, lo


def _cast_kernel(x_ref, o_ref):
    o_ref[...] = x_ref[...].astype(o_ref.dtype)


def _cast_bf16(x, tm):
    m, d = x.shape
    return pl.pallas_call(
        _cast_kernel,
        out_shape=jax.ShapeDtypeStruct((m, d), BF16),
        grid=(m // tm,),
        in_specs=[pl.BlockSpec((tm, d), lambda i: (i, 0))],
        out_specs=pl.BlockSpec((tm, d), lambda i: (i, 0)),
        compiler_params=_params("parallel"),
        name="cast_bf16",
    )(x)


def _mm_kernel(a_ref, w_ref, o_ref):
    a = a_ref[...].astype(BF16)
    o_ref[...] = _dot(a, w_ref[...].astype(BF16)).astype(o_ref.dtype)


def _matmul(a, w, *, tm, tn, out_dtype, name):
    m, k = a.shape
    n = w.shape[1]
    return pl.pallas_call(
        _mm_kernel,
        out_shape=jax.ShapeDtypeStruct((m, n), out_dtype),
        grid=(n // tn, m // tm),
        in_specs=[pl.BlockSpec((tm, k), lambda j, i: (i, 0)),
                  pl.BlockSpec((k, tn), lambda j, i: (0, j))],
        out_specs=pl.BlockSpec((tm, tn), lambda j, i: (i, j)),
        compiler_params=_params("parallel", "parallel"),
        name=name,
    )(a, w)


A_TQ = 2 * CHUNK
A_WIN = (LEFT_CHUNKS + 2) * CHUNK
A_VARIANTS = LEFT_CHUNKS * CHUNK // A_TQ + 1


A_BASE_W = A_WIN + A_TQ


def _band_bias_base(rel_bias):
    reach = A_WIN
    ext =jnp.pad(rel_bias.astype(F32), ((0, 0), (reach - MAX_REL, reach - MAX_REL)), mode="edge")
    rev = ext[:, ::-1]
    rows = []
    for v in range(A_VARIANTS):
        c = rev[:, reach - A_TQ * v - A_TQ: reach - A_TQ * v + A_WIN]
        rows.append(jnp.concatenate([c[:, A_TQ:], c[:, :A_TQ]], axis=1))
    return jnp.stack(rows)[:, :, None, :]


def _attn_a_kernel(q_ref, k_ref, v_ref, base_ref, o_ref, tb_ref, *, heads, dh, scale):
    i = pl.program_id(2)

    @pl.when(i == 0)
    def _():
        r = lax.broadcasted_iota(I32, (A_TQ, A_WIN), 0)
        j = lax.broadcasted_iota(I32, (A_TQ, A_WIN), 1)
        for var in range(A_VARIANTS):
            cdiff = (A_TQ * var + r) // CHUNK - j // CHUNK
            valid = (cdiff >= 0) & (cdiff <= LEFT_CHUNKS)
            for h in range(heads):
                rows = jnp.broadcast_to(base_ref[var, h], (A_TQ, A_BASE_W))
                toeplitz = pltpu.roll(rows, 0, 1, stride=1, stride_axis=0)[:, :A_WIN]
                tb_ref[var, h] = jnp.where(valid, toeplitz, MASK_VALUE)

    var = jnp.minimum(i, A_VARIANTS - 1)
    start = pl.multiple_of(jnp.maximum(i - (A_VARIANTS - 1), 0) * A_TQ, A_TQ)
    for h in range(heads):
        cs = slice(h * dh, (h + 1) * dh)
        q = q_ref[:, cs]
        k = k_ref[pl.ds(start, A_WIN), cs]
        v = v_ref[pl.ds(start, A_WIN), cs]
        s = lax.dot_general(q, k, NT_DIMS, preferred_element_type=F32) * scale + tb_ref[var, h]
        m = jnp.max(s, axis=-1, keepdims=True)
        p = jnp.exp(s - m)
        l = jnp.sum(p, axis=-1, keepdims=True)
        o = _dot(p.astype(BF16), v)
        o_ref[:, cs] = (o / l).astype(o_ref.dtype)


def _attention_a(proj, base, *, batch, seq, heads, dh, col_q, col_k, col_v, heads_per_step=8):
    n = proj.shape[0]
    gw = heads_per_step * dh
    n_groups = heads // heads_per_step
    n_qb = seq // A_TQ
    kern = functools.partial(_attn_a_kernel, heads=heads_per_step, dh=dh, scale=dh ** -0.5)
    return pl.pallas_call(
        kern,
        out_shape=jax.ShapeDtypeStruct((n, heads * dh), BF16),
        grid=(batch, n_groups, n_qb),
        in_specs=[
            pl.BlockSpec((A_TQ, gw), lambda b, g, i: (b * n_qb + i, col_q // gw + g)),
            pl.BlockSpec((seq, gw), lambda b, g, i: (b, col_k // gw + g)),
            pl.BlockSpec((seq, gw), lambda b, g, i: (b, col_v // gw + g)),
            pl.BlockSpec((A_VARIANTS, heads_per_step, 1, A_BASE_W), lambda b, g, i: (0, g, 0, 0)),
        ],
        out_specs=pl.BlockSpec((A_TQ, gw), lambda b, g, i: (b * n_qb + i, g)),
        scratch_shapes=[pltpu.VMEM((A_VARIANTS, heads_per_step, A_TQ, A_WIN), F32)],
        compiler_params=_params("parallel", "parallel", "arbitrary"),
        name="attn_band",
    )(proj, proj, proj, base)


B_TQ = 256


def _rope_tables(seq, dim):
    inv = 1.0 / (ROPE_THETA ** (jnp.arange(0, dim, 2, dtype=F32) / dim))
    ang = jnp.arange(seq, dtype=F32)[:, None] * inv[None, :]
    ang = jnp.concatenate([ang, ang], -1)
    sign = jnp.where(jnp.arange(dim) < dim // 2, -1.0, 1.0).astype(F32)
    return jnp.cos(ang), jnp.sin(ang) * sign[None, :]


def _rope(x, cos, sin_signed):
    return x * cos + pltpu.roll(x, x.shape[1] // 2, 1) * sin_signed


def _attn_b_kernel(q_ref, k_ref, v_ref, cos_ref, sin_ref, lam_ref, g_ref, o_ref, krot_ref,
                   *, dh, scale, lambda_init):
    qi = pl.program_id(2)
    seq = k_ref.shape[0]

    @pl.when(qi == 0)
    def _():
        for m in range(2):
            kf = k_ref[:, m * dh:(m + 1) * dh].astype(F32)
            krot_ref[m] = _rope(kf, cos_ref[...], sin_ref[...]).astype(BF16)

    lv = lam_ref[...]
    lam = (jnp.exp(jnp.sum(lv[0:1] * lv[1:2], axis=-1, keepdims=True))
           - jnp.exp(jnp.sum(lv[2:3] * lv[3:4], axis=-1, keepdims=True)) + lambda_init)

    def block(blk):
        row0 = blk * B_TQ
        kl = row0 + B_TQ
        cos_q = cos_ref[row0:kl, :]
        sin_q = sin_ref[row0:kl, :]
        q_chunk = (row0 + lax.broadcasted_iota(I32, (B_TQ, kl), 0)) // CHUNK
        k_chunk = lax.broadcasted_iota(I32, (B_TQ, kl), 1) // CHUNK
        allowed = k_chunk <= q_chunk
        probs = []
        for m in range(2):
            qf = q_ref[:, m * dh:(m + 1) * dh].astype(F32)
            qr = _rope(qf, cos_q, sin_q).astype(BF16)
            s = lax.dot_general(qr, krot_ref[m, :kl, :], NT_DIMS, preferred_element_type=F32) * scale
            s = jnp.where(allowed, s, MASK_VALUE)
            e = jnp.exp(s - jnp.max(s, axis=-1, keepdims=True))
            probs.append(e / jnp.sum(e, axis=-1, keepdims=True))
        w = (probs[0] - lam * probs[1]).astype(BF16)
        o = _dot(w, v_ref[:kl, :])
        ms = jnp.mean(o * o, axis=-1, keepdims=True)
        y = o * lax.rsqrt(ms + RMS_EPS) * g_ref[...] * (1.0 - lambda_init)
        o_ref[...] = y.astype(o_ref.dtype)

    for blk in range(seq // B_TQ):
        pl.when(qi == blk)(functools.partial(block, blk))


def _attention_b(proj, cos, sin_signed, lam_vecs, norm_g, *, batch, seq, heads, dh,
                 col_q, col_k, col_v, lambda_init):
    n = proj.shape[0]
    hw = 2 * dh
    n_qb = seq // B_TQ
    kern = functools.partial(_attn_b_kernel, dh=dh, scale=dh ** -0.5, lambda_init=lambda_init)
    return pl.pallas_call(
        kern,
        out_shape=jax.ShapeDtypeStruct((n, heads * hw), BF16),
        grid=(batch, heads, n_qb),
        in_specs=[
            pl.BlockSpec((B_TQ, hw), lambda b, h, i: (b * n_qb + i, col_q // hw + h)),
            pl.BlockSpec((seq, hw), lambda b, h, i: (b, col_k // hw + h)),
            pl.BlockSpec((seq, hw), lambda b, h, i: (b, col_v // hw + h)),
            pl.BlockSpec((seq, dh), lambda b, h, i: (0, 0)),
            pl.BlockSpec((seq, dh), lambda b, h, i: (0, 0)),
            pl.BlockSpec((4, dh), lambda b, h, i: (0, 0)),
            pl.BlockSpec((1, hw), lambda b, h, i: (0, 0)),
        ],
        out_specs=pl.BlockSpec((B_TQ, hw), lambda b, h, i: (b * n_qb + i, h)),
        scratch_shapes=[pltpu.VMEM((2, seq, dh), BF16)],
        compiler_params=_params("parallel", "parallel", "arbitrary"),
        name="attn_diff",
    )(proj, proj, proj, cos, sin_signed, lam_vecs, norm_g)


C_TQ = 512


def _attn_c_kernel(q_ref, k_ref, v_ref, o_ref, *, scale):
    s = lax.dot_general(q_ref[...], k_ref[...], NT_DIMS, preferred_element_type=F32) * scale
    e = jnp.exp(s - jnp.max(s, axis=-1, keepdims=True))
    p = (e / jnp.sum(e, axis=-1, keepdims=True)).astype(BF16)
    o_ref[...] = _dot(p, v_ref[...]).astype(o_ref.dtype)


def _attention_c(proj, ckv, *, batch, seq, mem_len, heads, dh, col_q):
    n = proj.shape[0]
    n_qb = seq // C_TQ
    kern = functools.partial(_attn_c_kernel, scale=dh ** -0.5)
    return pl.pallas_call(
        kern,
        out_shape=jax.ShapeDtypeStruct((n, heads * dh), BF16),
        grid=(batch, heads, n_qb),
        in_specs=[
            pl.BlockSpec((C_TQ, dh), lambda b, h, i: (b * n_qb + i, col_q // dh + h)),
            pl.BlockSpec((mem_len, dh), lambda b, h, i: (b, h)),
            pl.BlockSpec((mem_len, dh), lambda b, h, i: (b, heads + h)),
        ],
        out_specs=pl.BlockSpec((C_TQ, dh), lambda b, h, i: (b * n_qb + i, h)),
        compiler_params=_params("parallel", "parallel", "parallel"),
        name="attn_mem",
    )(proj, ckv, ckv)


def _merge_kernel(x_ref, ya_ref, yb_ref, yc_ref, wga_ref, wgb_ref, wgc_ref,
                  bga_ref, bgb_ref, bgc_ref, pa_ref, pb_ref, pc_ref, o_ref):
    x = x_ref[...]
    acc = None
    for wg, bg, y, p in ((wga_ref, bga_ref, ya_ref, pa_ref),
                         (wgb_ref, bgb_ref, yb_ref, pb_ref),
                         (wgc_ref, bgc_ref, yc_ref, pc_ref)):
        gate = jax.nn.sigmoid(_dot(x, wg[...].astype(BF16)) + bg[...])
        term = gate * _dot(y[...], p[...].astype(BF16))
        acc = term if acc is None else acc + term
    o_ref[...] = acc.astype(o_ref.dtype)


def _gated_merge(xb, ya, yb, yc, w_gates, b_gates, pa, pb, pc, *, tm, tn):
    n, d = xb.shape
    nj = d // tn
    row = lambda width: pl.BlockSpec((tm, width), lambda j, i: (i, 0))
    gate_w = lambda br: pl.BlockSpec((d, tn), lambda j, i, br=br: (0, br * nj + j))
    gate_b = lambda br: pl.BlockSpec((1, tn), lambda j, i, br=br: (0, br * nj + j))
    branch_w = lambda width: pl.BlockSpec((width, tn), lambda j, i: (0, j))
    return pl.pallas_call(
        _merge_kernel,
        out_shape=jax.ShapeDtypeStruct((n, d), BF16),
        grid=(nj, n // tm),
        in_specs=[row(d), row(ya.shape[1]), row(yb.shape[1]), row(yc.shape[1]),
                  gate_w(0), gate_w(1), gate_w(2), gate_b(0), gate_b(1), gate_b(2),
                  branch_w(pa.shape[0]), branch_w(pb.shape[0]), branch_w(pc.shape[0])],
        out_specs=pl.BlockSpec((tm, tn), lambda j, i: (i, j)),
        compiler_params=_params("parallel", "parallel"),
        name="gated_merge",
    )(xb, ya, yb, yc, w_gates, w_gates, w_gates, b_gates, b_gates, b_gates, pa, pb, pc)


def _layer_norm(z, g, b):
    mu = jnp.mean(z, axis=-1, keepdims=True)
    zc = z - mu
    var = jnp.mean(zc * zc, axis=-1, keepdims=True)
    return zc * lax.rsqrt(var + LN_EPS) * g + b


def _ln_router_kernel(x_ref, m_ref, g_ref, b_ref, wr_ref, br_ref,
                      h_ref, hp_ref, idx_ref, gate_ref, rank_ref, cnt_ref, carry_ref,
                      *, alpha, n_exp):
    @pl.when(pl.program_id(0) == 0)
    def _():
        carry_ref[...] = jnp.zeros_like(carry_ref)

    tm = x_ref.shape[0]
    h = _layer_norm(alpha * x_ref[...] + m_ref[...], g_ref[...], b_ref[...])
    h_ref[...] = h
    hp_ref[...] = _pack_halves(h)

    h_hi = h.astype(BF16)
    h_lo = (h - h_hi.astype(F32)).astype(BF16)
    w = wr_ref[...]
    r1 = _dot(h_hi, w)
    logits = r1[:, :LANES] + r1[:, LANES:] + _dot(h_lo, w[:, :LANES]) + br_ref[...]

    lane = lax.broadcasted_iota(I32, (tm, LANES), 1)
    lane_f = lane.astype(F32)
    cur = jnp.where(lane < n_exp, logits, -jnp.inf)
    vals, idxs = [], []
    for _ in range(TOP_K):
        mx = jnp.max(cur, axis=-1, keepdims=True)
        ix = jnp.min(jnp.where(cur == mx, lane_f, float(LANES)), axis=-1, keepdims=True).astype(I32)
        vals.append(mx)
        idxs.append(ix)
        cur = jnp.where(lane == ix, -jnp.inf, cur)
    exps = [jnp.exp(v - vals[0]) for v in vals]
    den = exps[0]
    for e in exps[1:]:
        den = den + e

    tri = (lax.broadcasted_iota(I32, (tm, tm), 0) > lax.broadcasted_iota(I32, (tm, tm), 1)).astype(BF16)
    carry = carry_ref[...]
    idx_out = jnp.zeros((tm, LANES), I32)
    gate_out = jnp.zeros((tm, LANES), F32)
    rank_out = jnp.zeros((tm, LANES), I32)
    for k in range(TOP_K):
        onehot = (lane == idxs[k]).astype(F32)
        before = _dot(tri, onehot.astype(BF16)) + carry
        rank = jnp.sum(onehot * before, axis=-1, keepdims=True)
        carry = carry + jnp.sum(onehot, axis=0, keepdims=True)
        idx_out = jnp.where(lane == k, idxs[k], idx_out)
        gate_out = jnp.where(lane == k, exps[k] / den, gate_out)
        rank_out = jnp.where(lane == k, rank.astype(I32), rank_out)
    carry_ref[...] = carry
    idx_ref[...] = idx_out
    gate_ref[...] = gate_out
    rank_ref[...] = rank_out
    cnt_ref[...] = carry


def _ln_router(x, m, g, b, wr_split, br_pad, *, alpha, n_exp, tm):
    n, d = x.shape
    row = pl.BlockSpec((tm, d), lambda i: (i, 0))
    vec = pl.BlockSpec((1, d), lambda i: (0, 0))
    small = pl.BlockSpec((tm, LANES), lambda i: (i, 0))
    kern = functools.partial(_ln_router_kernel, alpha=alpha, n_exp=n_exp)
    return pl.pallas_call(
        kern,
        out_shape=(jax.ShapeDtypeStruct((n, d), F32),
                   jax.ShapeDtypeStruct((n, d // 2), U32),
                   jax.ShapeDtypeStruct((n, LANES), I32),
                   jax.ShapeDtypeStruct((n, LANES), F32),
                   jax.ShapeDtypeStruct((n, LANES), I32),
                   jax.ShapeDtypeStruct((1, LANES), F32)),
        grid=(n // tm,),
        in_specs=[row, row, vec, vec,
                  pl.BlockSpec((d, 2 * LANES), lambda i: (0, 0)),
                  pl.BlockSpec((1, LANES), lambda i: (0, 0))],
        out_specs=(row, pl.BlockSpec((tm, d // 2), lambda i: (i, 0)), small, small, small,
                   pl.BlockSpec((1, LANES), lambda i: (0, 0))),
        scratch_shapes=[pltpu.VMEM((1, LANES), F32)],
        compiler_params=_params("arbitrary"),
        name="ln_router",
    )(x, m, g, b, wr_split, br_pad)


MOE_RB = 256


def _cumsum_small(x):
    n = x.shape[0]
    keep = jnp.arange(n)[:, None] >= jnp.arange(n)[None, :]
    return jnp.sum(jnp.where(keep, x[None, :], 0), axis=1).astype(x.dtype)


def _lookup(table, idx):
    hit = idx[..., None] == jnp.arange(table.shape[0], dtype=idx.dtype)
    return jnp.sum(jnp.where(hit, table, 0), axis=-1).astype(table.dtype)


def _group_layout(counts, cap):
    padded = (counts + MOE_RB - 1) // MOE_RB * MOE_RB
    pend = _cumsum_small(padded)
    pstart = pend - padded
    tail = jnp.stack([pend[-1], (cap - pend[-1]) // MOE_RB])
    return pstart, padded // MOE_RB, pstart + counts, padded - counts, tail


def _tail_rows(tail, b):
    return pl.ds(pl.multiple_of(tail[0] + b * MOE_RB, MOE_RB), MOE_RB)


def _zero_tail(tail, zero_view, dst_view, sem):
    def start(b, carry):
        pltpu.make_async_copy(zero_view, dst_view(_tail_rows(tail, b)), sem).start()
        return carry

    lax.fori_loop(0, tail[1], start, 0)

    def done(b, carry):
        pltpu.make_async_copy(zero_view, dst_view(_tail_rows(tail, 0)), sem).wait()
        return carry

    lax.fori_loop(0, tail[1], done, 0)


def _row_copy(src, src_row, dst, dst_row, sem):
    return pltpu.make_async_copy(src.at[pl.ds(src_row, 1)], dst.at[pl.ds(dst_row, 1)], sem)


def _dispatch_kernel(pad_lo, pad_n, tail, dest_ref, hp_ref, xg_ref, zero_ref, sem, zsem):
    tm = hp_ref.shape[0]
    n_exp = pad_lo.shape[0]

    @pl.when(pl.program_id(0) == 0)
    def _():
        zero_ref[...] = jnp.zeros_like(zero_ref)
        _zero_tail(tail, zero_ref, lambda rows: xg_ref.at[rows], zsem)

        def fill(e, carry):
            def one(r, c):
                _row_copy(zero_ref, 0, xg_ref, pad_lo[e] + r, zsem).start()
                return c
            return lax.fori_loop(0, pad_n[e], one, carry)

        lax.fori_loop(0, n_exp, fill, 0)

        def fill_done(e, carry):
            def one(r, c):
                _row_copy(zero_ref, 0, xg_ref, 0, zsem).wait()
                return c
            return lax.fori_loop(0, pad_n[e], one, carry)

        lax.fori_loop(0, n_exp, fill_done, 0)

    def issue(t, carry):
        for k in range(TOP_K):
            _row_copy(hp_ref, t, xg_ref, dest_ref[0, t * TOP_K + k], sem).start(priority=k % 2)
        return carry

    lax.fori_loop(0, tm, issue, 0)

    def drain(t, carry):
        for k in range(TOP_K):
            _row_copy(hp_ref, 0, xg_ref, 0, sem).wait()
        return carry

    lax.fori_loop(0, tm, drain, 0)


def _dispatch(dest, hp, pad_lo, pad_n, tail, *, tm, cap):
    n, w = hp.shape
    dest3 = dest.reshape(n // tm, 1, tm * TOP_K)
    grid_spec = pltpu.PrefetchScalarGridSpec(
        num_scalar_prefetch=3,
        grid=(n // tm,),
        in_specs=[pl.BlockSpec((None, 1, tm * TOP_K), lambda i, *_: (i, 0, 0), memory_space=pltpu.SMEM),
                  pl.BlockSpec((tm, w), lambda i, *_: (i, 0))],
        out_specs=pl.BlockSpec(memory_space=pl.ANY),
        scratch_shapes=[pltpu.VMEM((MOE_RB, w), U32), pltpu.SemaphoreType.DMA(()), pltpu.SemaphoreType.DMA(())],
    )
    return pl.pallas_call(
        _dispatch_kernel,
        out_shape=jax.ShapeDtypeStruct((cap, w), U32),
        grid_spec=grid_spec,
        compiler_params=_params("arbitrary"),
        name="dispatch",
    )(pad_lo, pad_n, tail, dest3, hp)


MOE_GROUP_BLOCKS = 5
K_STEP = 512
K_CHUNK = 256
SEL_W = 512
ACC_PIECES = 2


def _group_table(pstart, nblk, n_rows):
    n_exp = nblk.shape[0]
    n_groups = n_exp + (n_rows // MOE_RB + n_exp + MOE_GROUP_BLOCKS - 1) // MOE_GROUP_BLOCKS
    per_e = (nblk + MOE_GROUP_BLOCKS - 1) // MOE_GROUP_BLOCKS
    gend = _cumsum_small(per_e)
    total = gend[-1]
    g = jnp.arange(n_groups, dtype=I32)
    gc = jnp.minimum(g, total - 1)
    e = jnp.minimum(jnp.sum((gend[None, :] <= gc[:, None]).astype(I32), axis=1), n_exp - 1)
    j = gc - _lookup(gend - per_e, e)
    grow = _lookup(pstart, e) + j * (MOE_GROUP_BLOCKS * MOE_RB)
    gn = jnp.where(g < total, jnp.minimum(MOE_GROUP_BLOCKS, _lookup(nblk, e) - j * MOE_GROUP_BLOCKS), 0)
    return e.astype(I32), grow.astype(I32), gn.astype(I32)


def _even_lane_selector():
    r = jnp.arange(SEL_W)[:, None]
    c = jnp.arange(SEL_W // 2)[None, :]
    return (r == 2 * c).astype(BF16)


def _swiglu_even_lanes(h, sel):
    tn = h.shape[1]
    glu = jnp.minimum(h, SWIGLU_LIMIT)
    lin = jnp.clip(h, -SWIGLU_LIMIT, SWIGLU_LIMIT) + 1.0
    gact = glu * jax.nn.sigmoid(SWIGLU_ALPHA * glu)
    parts = []
    for c in range(tn // LANES):
        cs = slice(c * LANES, (c + 1) * LANES)
        parts.append(gact[:, cs] * pltpu.roll(lin[:, cs], LANES - 1, 1))
    inter = jnp.concatenate(parts, axis=1).astype(BF16)
    acts = [_dot(inter[:, s * SEL_W:(s + 1) * SEL_W], sel) for s in range(tn // SEL_W)]
    return jnp.concatenate(acts, axis=1)


def _up_chunks(stage_block):
    hi, lo = _unpack_halves(stage_block)
    n = hi.shape[1] // K_CHUNK
    return ([hi[:, c * K_CHUNK:(c + 1) * K_CHUNK].astype(BF16) for c in range(n)]
            + [lo[:, c * K_CHUNK:(c + 1) * K_CHUNK].astype(BF16) for c in range(n)])


def _down_chunks(stage_block):
    return [stage_block[:, c * K_CHUNK:(c + 1) * K_CHUNK] for c in range(stage_block.shape[1] // K_CHUNK)]


UP_PIECE = 1024
DOWN_PIECE = 2048


def _up_finish(acc, rws, obuf, slot, sel_ref):
    for p in range(acc.shape[1] // UP_PIECE):
        h = acc[rws, p * UP_PIECE:(p + 1) * UP_PIECE]
        obuf[slot, :, p * UP_PIECE // 2:(p + 1) * UP_PIECE // 2] = (
            _swiglu_even_lanes(h, sel_ref[...]).astype(obuf.dtype))


def _down_finish(acc, rws, obuf, slot, sel_ref):
    del sel_ref
    for p in range(acc.shape[1] // DOWN_PIECE):
        y = acc[rws, p * DOWN_PIECE:(p + 1) * DOWN_PIECE]
        obuf[slot, :, p * DOWN_PIECE // 2:(p + 1) * DOWN_PIECE // 2] = _pack_halves(y)


def _ffn_kernel(ge, grow, gn, tail, rows_ref, w_ref, b_ref, sel_ref, out_ref,
                stage, xres, acc, obuf, wbf, sem_in, sem_out, *, chunks_of, finish):
    del ge
    g = pl.program_id(0)
    kc = pl.program_id(1)
    nk = pl.num_programs(1)
    n = gn[g]
    base = grow[g]
    rb = stage.shape[1]
    nsub = w_ref.shape[0] // K_CHUNK

    def hbm_rows(blk):
        return pl.ds(pl.multiple_of(base + blk * rb, rb), rb)

    def vm_rows(blk):
        return pl.ds(pl.multiple_of(blk * rb, rb), rb)

    def in_copy(blk, slot):
        return pltpu.make_async_copy(rows_ref.at[hbm_rows(blk)], stage.at[slot], sem_in.at[slot])

    def out_copy(blk, slot):
        return pltpu.make_async_copy(obuf.at[slot], out_ref.at[hbm_rows(blk)], sem_out.at[slot])

    def load_group():
        in_copy(0, 0).start()

        def body(blk, carry):
            slot = lax.rem(blk, 2)
            in_copy(blk, slot).wait()

            @pl.when(blk + 1 < n)
            def _():
                in_copy(blk + 1, 1 - slot).start()

            for ci, chunk in enumerate(chunks_of(stage[slot])):
                xres[ci, vm_rows(blk), :] = chunk
            acc[vm_rows(blk), :] = jnp.broadcast_to(b_ref[...], (rb, acc.shape[1]))
            return carry

        lax.fori_loop(0, n, body, 0)

    def accumulate(blk, cast):
        piece = acc.shape[1] // ACC_PIECES
        for p in range(ACC_PIECES):
            cols = slice(p * piece, (p + 1) * piece)
            a = acc[vm_rows(blk), cols]
            for s in range(nsub):
                ws = slice(s * K_CHUNK, (s + 1) * K_CHUNK)
                if cast:
                    wk = w_ref[ws, cols].astype(BF16)
                    wbf[ws, cols] = wk
                else:
                    wk = wbf[ws, cols]
                a = a + _dot(xres[kc * nsub + s, vm_rows(blk), :], wk)
            acc[vm_rows(blk), cols] = a

    @pl.when(n > 0)
    def _():
        @pl.when(kc == 0)
        def _():
            load_group()

        accumulate(0, True)

        @pl.when(kc < nk - 1)
        def _():
            def body(blk, carry):
                accumulate(blk, False)
                return carry

            lax.fori_loop(1, n, body, 0)

        @pl.when(kc == nk - 1)
        def _():
            def body(blk, carry):
                prev = lax.rem(blk - 1, 2)

                @pl.when(blk >= 3)
                def _():
                    out_copy(0, prev).wait()

                finish(acc, vm_rows(blk - 1), obuf, prev, sel_ref)
                accumulate(blk, False)
                out_copy(blk - 1, prev).start()
                return carry

            lax.fori_loop(1, n, body, 0)

            last = n - 1
            ls = lax.rem(last, 2)

            @pl.when(last >= 2)
            def _():
                out_copy(0, ls).wait()

            finish(acc, vm_rows(last), obuf, ls, sel_ref)
            out_copy(last, ls).start()

            @pl.when(last >= 1)
            def _():
                out_copy(0, 1 - ls).wait()

            out_copy(0, ls).wait()

    @pl.when((g == pl.num_programs(0) - 1) & (kc == nk - 1))
    def _():
        obuf[0] = jnp.zeros(obuf.shape[1:], obuf.dtype)
        _zero_tail(tail, obuf.at[0], lambda r: out_ref.at[r], sem_out.at[0])


def _expert_ffn(groups, tail, rows_in, w, b, *, out_width, out_dtype, chunks_of, finish, name):
    ge, grow, gn = groups
    cap, w_in = rows_in.shape
    n_exp, k, n_out = w.shape
    n_k = k // K_STEP
    g_rows = MOE_GROUP_BLOCKS * MOE_RB

    def w_index(g, kc, ge, grow, gn, tail):
        return ge[g], jnp.where(gn[g] > 0, kc, n_k - 1), 0

    grid_spec = pltpu.PrefetchScalarGridSpec(
        num_scalar_prefetch=4,
        grid=(ge.shape[0], n_k),
        in_specs=[
            pl.BlockSpec(memory_space=pl.ANY),
            pl.BlockSpec((None, K_STEP, n_out), w_index),
            pl.BlockSpec((None, 1, n_out), lambda g, kc, ge, *_: (ge[g], 0, 0)),
            pl.BlockSpec((SEL_W, SEL_W // 2), lambda g, kc, *_: (0, 0)),
        ],
        out_specs=pl.BlockSpec(memory_space=pl.ANY),
        scratch_shapes=[pltpu.VMEM((2, MOE_RB, w_in), rows_in.dtype),
                        pltpu.VMEM((k // K_CHUNK, g_rows, K_CHUNK), BF16),
                        pltpu.VMEM((g_rows, n_out), F32),
                        pltpu.VMEM((2, MOE_RB, out_width), out_dtype),
                        pltpu.VMEM((K_STEP, n_out), BF16),
                        pltpu.SemaphoreType.DMA((2,)),
                        pltpu.SemaphoreType.DMA((2,))],
    )
    return pl.pallas_call(
        functools.partial(_ffn_kernel, chunks_of=chunks_of, finish=finish),
        out_shape=jax.ShapeDtypeStruct((cap, out_width), out_dtype),
        grid_spec=grid_spec,
        compiler_params=_params("arbitrary", "arbitrary"),
        name=name,
    )(ge, grow, gn, tail, rows_in, w, b.reshape(n_exp, 1, n_out), _even_lane_selector())


def _combine_kernel(dest_ref, next_ref, h_ref, gate_ref, g_ref, b_ref, y_ref, o_ref, buf_ref, sem,
                    *, alpha, chunk):
    tm = h_ref.shape[0]
    i = pl.program_id(0)
    slot = lax.rem(i, 2)

    def gather(idx_ref, s):
        def issue(t, carry):
            for k in range(TOP_K):
                pltpu.make_async_copy(y_ref.at[pl.ds(idx_ref[0, t * TOP_K + k], 1)],
                                      buf_ref.at[s, k, pl.ds(t, 1)], sem.at[s]).start(priority=k % 2)
            return carry
        lax.fori_loop(0, tm, issue, 0)

    @pl.when(i == 0)
    def _():
        gather(dest_ref, 0)

    @pl.when(i + 1 < pl.num_programs(0))
    def _():
        gather(next_ref, 1 - slot)

    def drain(t, carry):
        for k in range(TOP_K):
            pltpu.make_async_copy(y_ref.at[pl.ds(0, 1)], buf_ref.at[slot, k, pl.ds(0, 1)], sem.at[slot]).wait()
        return carry

    lax.fori_loop(0, tm, drain, 0)

    gates = gate_ref[...]
    acc_hi = None
    acc_lo = None
    for k in range(TOP_K):
        hi, lo = _unpack_halves(buf_ref[slot, k])
        gk = gates[:, k:k + 1]
        acc_hi = gk * hi if acc_hi is None else acc_hi + gk * hi
        acc_lo = gk * lo if acc_lo is None else acc_lo + gk * lo
    hw = chunk // 2
    pieces = []
    for c in range(acc_hi.shape[1] // hw):
        pieces.append(acc_hi[:, c * hw:(c + 1) * hw])
        pieces.append(acc_lo[:, c * hw:(c + 1) * hw])
    ffn = jnp.concatenate(pieces, axis=1)
    o_ref[...] = _layer_norm(alpha * h_ref[...] + ffn, g_ref[...], b_ref[...]).astype(o_ref.dtype)


def _combine(dest, h, gates, g, b, y, *, alpha, tm, chunk):
    n, d = h.shape
    n_steps = n // tm
    dest3 = dest.reshape(n_steps, 1, tm * TOP_K)
    kern = functools.partial(_combine_kernel, alpha=alpha, chunk=chunk)
    return pl.pallas_call(
        kern,
        out_shape=jax.ShapeDtypeStruct((n, d), F32),
        grid=(n_steps,),
        in_specs=[pl.BlockSpec((None, 1, tm * TOP_K), lambda i: (i, 0, 0), memory_space=pltpu.SMEM),
                  pl.BlockSpec((None, 1, tm * TOP_K), lambda i: (jnp.minimum(i + 1, n_steps - 1), 0, 0),
                               memory_space=pltpu.SMEM),
                  pl.BlockSpec((tm, d), lambda i: (i, 0)),
                  pl.BlockSpec((tm, LANES), lambda i: (i, 0)),
                  pl.BlockSpec((1, d), lambda i: (0, 0)),
                  pl.BlockSpec((1, d), lambda i: (0, 0)),
                  pl.BlockSpec(memory_space=pl.ANY)],
        out_specs=pl.BlockSpec((tm, d), lambda i: (i, 0)),
        scratch_shapes=[pltpu.VMEM((2, TOP_K, tm, d // 2), U32), pltpu.SemaphoreType.DMA((2,))],
        compiler_params=_params("arbitrary"),
        name="combine_ln",
    )(dest3, dest3, h, gates, g, b, y)


def _tiles(n_tokens, d_model, d_expert):
    return dict(
        cast_tm=min(512, n_tokens),
        proj=dict(tm=min(1024, n_tokens), tn=512),
        memkv=dict(tm=512, tn=512),
        merge=dict(tm=min(512, n_tokens), tn=256),
        out=dict(tm=min(1024, n_tokens), tn=512),
        ln_tm=min(256, n_tokens),
        dispatch_tm=min(256, n_tokens),
        combine_tm=min(128, n_tokens),
    )


def _layer(h, mem2, lw, *, batch, seq, mem_len, lambda_init, alpha):
    n, d = h.shape
    t = _tiles(n, d, lw["w_mlp2"].shape[1])
    a_width = A_HEADS * A_HEAD_DIM
    b_width = B_HEADS * 2 * B_HEAD_DIM
    col = dict(a_q=0, a_k=a_width, a_v=2 * a_width, b_q=3 * a_width, b_k=3 * a_width + b_width,
               b_v=3 * a_width + 2 * b_width, c_q=3 * a_width + 3 * b_width)

    xb = _cast_bf16(h, t["cast_tm"])
    proj = _matmul(xb, lw["w_in"], out_dtype=BF16, name="in_proj", **t["proj"])
    ckv = _matmul(mem2, lw["w_mem_kv"], out_dtype=BF16, name="mem_kv",
                  tm=min(t["memkv"]["tm"], mem2.shape[0]), tn=t["memkv"]["tn"])

    ya = _attention_a(proj, _band_bias_base(lw["rel_bias"]), batch=batch, seq=seq,
                      heads=A_HEADS, dh=A_HEAD_DIM, col_q=col["a_q"], col_k=col["a_k"], col_v=col["a_v"])
    cos, sin_signed = _rope_tables(seq, B_HEAD_DIM)
    lam_vecs = jnp.stack([lw["lambda_q1"], lw["lambda_k1"], lw["lambda_q2"], lw["lambda_k2"]]).astype(F32)
    yb = _attention_b(proj, cos, sin_signed, lam_vecs, lw["diff_norm_g"].reshape(1, -1),
                      batch=batch, seq=seq, heads=B_HEADS, dh=B_HEAD_DIM,
                      col_q=col["b_q"], col_k=col["b_k"], col_v=col["b_v"], lambda_init=lambda_init)
    yc = _attention_c(proj, ckv, batch=batch, seq=seq, mem_len=mem_len, heads=C_HEADS,
                      dh=C_HEAD_DIM, col_q=col["c_q"])

    merged = _gated_merge(xb, ya, yb, yc, lw["w_gates"], lw["b_gates"].reshape(1, -1),
                          lw["w_branch_a"], lw["w_branch_b"], lw["w_branch_c"], **t["merge"])
    mix = _matmul(merged, lw["w_o"], out_dtype=F32, name="out_proj", **t["out"])

    n_exp = lw["w_router"].shape[1]
    wr = jnp.pad(lw["w_router"], ((0, 0), (0, LANES - n_exp)))
    wr_hi = wr.astype(BF16)
    wr_lo = (wr - wr_hi.astype(F32)).astype(BF16)
    br = jnp.pad(lw["b_router"], (0, LANES - n_exp)).reshape(1, LANES)
    h1, h1_packed, top_idx, gates, rank, cnt = _ln_router(
        h, mix, lw["ln1_g"].reshape(1, -1), lw["ln1_b"].reshape(1, -1),
        jnp.concatenate([wr_hi, wr_lo], axis=1), br, alpha=alpha, n_exp=n_exp, tm=t["ln_tm"])

    counts = cnt[0, :n_exp].astype(I32)
    cap = n * TOP_K + n_exp * MOE_RB
    pstart, nblk, pad_lo, pad_n, tail = _group_layout(counts, cap)
    dest = _lookup(pstart, top_idx[:, :TOP_K]) + rank[:, :TOP_K]

    xg = _dispatch(dest, h1_packed, pad_lo, pad_n, tail, tm=t["dispatch_tm"], cap=cap)
    groups = _group_table(pstart, nblk, n * TOP_K)
    d_expert = lw["w_mlp2"].shape[1]
    act = _expert_ffn(groups, tail, xg, lw["w_mlp1"], lw["b_mlp1"], out_width=d_expert, out_dtype=BF16,
                      chunks_of=_up_chunks, finish=_up_finish, name="expert_up")
    y = _expert_ffn(groups, tail, act, lw["w_mlp2"], lw["b_mlp2"], out_width=d // 2, out_dtype=U32,
                    chunks_of=_down_chunks, finish=_down_finish, name="expert_down")
    return _combine(dest, h1, gates, lw["ln2_g"].reshape(1, -1), lw["ln2_b"].reshape(1, -1), y,
                    alpha=alpha, tm=t["combine_tm"], chunk=min(DOWN_PIECE, d))


def kernel(x, mem, w_in, w_mem_kv, rel_bias, lambda_q1, lambda_k1, lambda_q2, lambda_k2, diff_norm_g,
           w_branch_a, w_branch_b, w_branch_c, w_gates, b_gates, w_o, ln1_g, ln1_b, w_router, b_router,
           w_mlp1, b_mlp1, w_mlp2, b_mlp2, ln2_g, ln2_b):
    batch, seq, d = x.shape
    mem_len = mem.shape[1]
    depth = w_in.shape[0]
    alpha = (2 * depth) ** 0.25
    stacked = dict(w_in=w_in, w_mem_kv=w_mem_kv, rel_bias=rel_bias, lambda_q1=lambda_q1,
                   lambda_k1=lambda_k1, lambda_q2=lambda_q2, lambda_k2=lambda_k2, diff_norm_g=diff_norm_g,
                   w_branch_a=w_branch_a, w_branch_b=w_branch_b, w_branch_c=w_branch_c, w_gates=w_gates,
                   b_gates=b_gates, w_o=w_o, ln1_g=ln1_g, ln1_b=ln1_b, w_router=w_router,
                   b_router=b_router, w_mlp1=w_mlp1, b_mlp1=b_mlp1, w_mlp2=w_mlp2, b_mlp2=b_mlp2,
                   ln2_g=ln2_g, ln2_b=ln2_b)
    h = x.reshape(batch * seq, d)
    mem2 = mem.reshape(batch * mem_len, d)
    for l in range(depth):
        lw = {name: w[l] for name, w in stacked.items()}
        lambda_init = 0.8 - 0.6 * math.exp(-0.3 * l)
        h = _layer(h, mem2, lw, batch=batch, seq=seq, mem_len=mem_len, lambda_init=lambda_init, alpha=alpha)
    return h.reshape(batch, seq, d)
```

```python
import functools
import math

import jax
import jax.numpy as jnp
from jax import lax
from jax.experimental import pallas as pl
from jax.experimental.pallas import tpu as pltpu

F32 = jnp.float32
BF16 = jnp.bfloat16
U32 = jnp.uint32
I32 = jnp.int32

CHUNK = 64
LEFT_CHUNKS = 8
MAX_REL = 128
A_HEADS = 16
A_HEAD_DIM = 128
B_HEADS = 4
B_HEAD_DIM = 128
C_HEADS = 4
C_HEAD_DIM = 256
N_BRANCHES = 3
ROPE_THETA = 10000.0
TOP_K = 4
SWIGLU_LIMIT = 7.0
SWIGLU_ALPHA = 1.702
LN_EPS = 1e-5
RMS_EPS = 1e-5
MASK_VALUE = -1e30

V7X_VMEM_BYTES = 64 * 1024 * 1024
V7X_VMEM_LIMIT = V7X_VMEM_BYTES - 8 * 1024 * 1024
LANES = 128

NT_DIMS = (((1,), (1,)), ((), ()))


def _params(*semantics):
    return pltpu.CompilerParams(dimension_semantics=semantics,
                                vmem_limit_bytes=V7X_VMEM_LIMIT)


def _dot(a, b):
    return jnp.dot(a, b, preferred_element_type=F32)


def _pack_halves(x):
    w = x.shape[1] // 2
    hi = lax.bitcast_convert_type(x[:, :w].astype(jnp.bfloat16).astype(F32), U32)
    lo = lax.bitcast_convert_type(x[:, w:].astype(jnp.bfloat16).astype(F32), U32)
    return hi | (lo >> 16)


def _unpack_halves(p):
    hi = lax.bitcast_convert_type(p & jnp.uint32(0xFFFF0000), F32)
    lo = lax.bitcast_convert_type(p << 16, F32)
    return hi, lo


def _cast_kernel(x_ref, o_ref):
    o_ref[...] = x_ref[...].astype(o_ref.dtype)


def _cast_bf16(x, tm):
    m, d = x.shape
    return pl.pallas_call(
        _cast_kernel,
        out_shape=jax.ShapeDtypeStruct((m, d), BF16),
        grid=(m // tm,),
        in_specs=[pl.BlockSpec((tm, d), lambda i: (i, 0))],
        out_specs=pl.BlockSpec((tm, d), lambda i: (i, 0)),
        compiler_params=_params("parallel"),
        name="cast_bf16",
    )(x)


def _mm_kernel(a_ref, w_ref, o_ref):
    a = a_ref[...].astype(BF16)
    o_ref[...] = _dot(a, w_ref[...].astype(BF16)).astype(o_ref.dtype)


def _matmul(a, w, *, tm, tn, out_dtype, name):
    m, k = a.shape
    n = w.shape[1]
    return pl.pallas_call(
        _mm_kernel,
        out_shape=jax.ShapeDtypeStruct((m, n), out_dtype),
        grid=(n // tn, m // tm),
        in_specs=[pl.BlockSpec((tm, k), lambda j, i: (i, 0)),
                  pl.BlockSpec((k, tn), lambda j, i: (0, j))],
        out_specs=pl.BlockSpec((tm, tn), lambda j, i: (i, j)),
        compiler_params=_params("parallel", "parallel"),
        name=name,
    )(a, w)


A_TQ = 2 * CHUNK
A_WIN = (LEFT_CHUNKS + 2) * CHUNK
A_VARIANTS = LEFT_CHUNKS * CHUNK // A_TQ + 1


A_BASE_W = A_WIN + A_TQ


def _band_bias_base(rel_bias):
    reach = A_WIN
    ext =jnp.pad(rel_bias.astype(F32), ((0, 0), (reach - MAX_REL, reach - MAX_REL)), mode="edge")
    rev = ext[:, ::-1]
    rows = []
    for v in range(A_VARIANTS):
        c = rev[:, reach - A_TQ * v - A_TQ: reach - A_TQ * v + A_WIN]
        rows.append(jnp.concatenate([c[:, A_TQ:], c[:, :A_TQ]], axis=1))
    return jnp.stack(rows)[:, :, None, :]


def _attn_a_kernel(q_ref, k_ref, v_ref, base_ref, o_ref, tb_ref, *, heads, dh, scale):
    i = pl.program_id(2)

    @pl.when(i == 0)
    def _():
        r = lax.broadcasted_iota(I32, (A_TQ, A_WIN), 0)
        j = lax.broadcasted_iota(I32, (A_TQ, A_WIN), 1)
        for var in range(A_VARIANTS):
            cdiff = (A_TQ * var + r) // CHUNK - j // CHUNK
            valid = (cdiff >= 0) & (cdiff <= LEFT_CHUNKS)
            for h in range(heads):
                rows = jnp.broadcast_to(base_ref[var, h], (A_TQ, A_BASE_W))
                toeplitz = pltpu.roll(rows, 0, 1, stride=1, stride_axis=0)[:, :A_WIN]
                tb_ref[var, h] = jnp.where(valid, toeplitz, MASK_VALUE)

    var = jnp.minimum(i, A_VARIANTS - 1)
    start = pl.multiple_of(jnp.maximum(i - (A_VARIANTS - 1), 0) * A_TQ, A_TQ)
    for h in range(heads):
        cs = slice(h * dh, (h + 1) * dh)
        q = q_ref[:, cs]
        k = k_ref[pl.ds(start, A_WIN), cs]
        v = v_ref[pl.ds(start, A_WIN), cs]
        s = lax.dot_general(q, k, NT_DIMS, preferred_element_type=F32) * scale + tb_ref[var, h]
        m = jnp.max(s, axis=-1, keepdims=True)
        p = jnp.exp(s - m)
        l = jnp.sum(p, axis=-1, keepdims=True)
        o = _dot(p.astype(BF16), v)
        o_ref[:, cs] = (o / l).astype(o_ref.dtype)


def _attention_a(proj, base, *, batch, seq, heads, dh, col_q, col_k, col_v, heads_per_step=8):
    n = proj.shape[0]
    gw = heads_per_step * dh
    n_groups = heads // heads_per_step
    n_qb = seq // A_TQ
    kern = functools.partial(_attn_a_kernel, heads=heads_per_step, dh=dh, scale=dh ** -0.5)
    return pl.pallas_call(
        kern,
        out_shape=jax.ShapeDtypeStruct((n, heads * dh), BF16),
        grid=(batch, n_groups, n_qb),
        in_specs=[
            pl.BlockSpec((A_TQ, gw), lambda b, g, i: (b * n_qb + i, col_q // gw + g)),
            pl.BlockSpec((seq, gw), lambda b, g, i: (b, col_k // gw + g)),
            pl.BlockSpec((seq, gw), lambda b, g, i: (b, col_v // gw + g)),
            pl.BlockSpec((A_VARIANTS, heads_per_step, 1, A_BASE_W), lambda b, g, i: (0, g, 0, 0)),
        ],
        out_specs=pl.BlockSpec((A_TQ, gw), lambda b, g, i: (b * n_qb + i, g)),
        scratch_shapes=[pltpu.VMEM((A_VARIANTS, heads_per_step, A_TQ, A_WIN), F32)],
        compiler_params=_params("parallel", "parallel", "arbitrary"),
        name="attn_band",
    )(proj, proj, proj, base)


B_TQ = 256


def _rope_tables(seq, dim):
    inv = 1.0 / (ROPE_THETA ** (jnp.arange(0, dim, 2, dtype=F32) / dim))
    ang = jnp.arange(seq, dtype=F32)[:, None] * inv[None, :]
    ang = jnp.concatenate([ang, ang], -1)
    sign = jnp.where(jnp.arange(dim) < dim // 2, -1.0, 1.0).astype(F32)
    return jnp.cos(ang), jnp.sin(ang) * sign[None, :]


def _rope(x, cos, sin_signed):
    return x * cos + pltpu.roll(x, x.shape[1] // 2, 1) * sin_signed


def _attn_b_kernel(q_ref, k_ref, v_ref, cos_ref, sin_ref, lam_ref, g_ref, o_ref, krot_ref,
                   *, dh, scale, lambda_init):
    qi = pl.program_id(2)
    seq = k_ref.shape[0]

    @pl.when(qi == 0)
    def _():
        for m in range(2):
            kf = k_ref[:, m * dh:(m + 1) * dh].astype(F32)
            krot_ref[m] = _rope(kf, cos_ref[...], sin_ref[...]).astype(BF16)

    lv = lam_ref[...]
    lam = (jnp.exp(jnp.sum(lv[0:1] * lv[1:2], axis=-1, keepdims=True))
           - jnp.exp(jnp.sum(lv[2:3] * lv[3:4], axis=-1, keepdims=True)) + lambda_init)

    def block(blk):
        row0 = blk * B_TQ
        kl = row0 + B_TQ
        cos_q = cos_ref[row0:kl, :]
        sin_q = sin_ref[row0:kl, :]
        q_chunk = (row0 + lax.broadcasted_iota(I32, (B_TQ, kl), 0)) // CHUNK
        k_chunk = lax.broadcasted_iota(I32, (B_TQ, kl), 1) // CHUNK
        allowed = k_chunk <= q_chunk
        probs = []
        for m in range(2):
            qf = q_ref[:, m * dh:(m + 1) * dh].astype(F32)
            qr = _rope(qf, cos_q, sin_q).astype(BF16)
            s = lax.dot_general(qr, krot_ref[m, :kl, :], NT_DIMS, preferred_element_type=F32) * scale
            s = jnp.where(allowed, s, MASK_VALUE)
            e = jnp.exp(s - jnp.max(s, axis=-1, keepdims=True))
            probs.append(e / jnp.sum(e, axis=-1, keepdims=True))
        w = (probs[0] - lam * probs[1]).astype(BF16)
        o = _dot(w, v_ref[:kl, :])
        ms = jnp.mean(o * o, axis=-1, keepdims=True)
        y = o * lax.rsqrt(ms + RMS_EPS) * g_ref[...] * (1.0 - lambda_init)
        o_ref[...] = y.astype(o_ref.dtype)

    for blk in range(seq // B_TQ):
        pl.when(qi == blk)(functools.partial(block, blk))


def _attention_b(proj, cos, sin_signed, lam_vecs, norm_g, *, batch, seq, heads, dh,
                 col_q, col_k, col_v, lambda_init):
    n = proj.shape[0]
    hw = 2 * dh
    n_qb = seq // B_TQ
    kern = functools.partial(_attn_b_kernel, dh=dh, scale=dh ** -0.5, lambda_init=lambda_init)
    return pl.pallas_call(
        kern,
        out_shape=jax.ShapeDtypeStruct((n, heads * hw), BF16),
        grid=(batch, heads, n_qb),
        in_specs=[
            pl.BlockSpec((B_TQ, hw), lambda b, h, i: (b * n_qb + i, col_q // hw + h)),
            pl.BlockSpec((seq, hw), lambda b, h, i: (b, col_k // hw + h)),
            pl.BlockSpec((seq, hw), lambda b, h, i: (b, col_v // hw + h)),
            pl.BlockSpec((seq, dh), lambda b, h, i: (0, 0)),
            pl.BlockSpec((seq, dh), lambda b, h, i: (0, 0)),
            pl.BlockSpec((4, dh), lambda b, h, i: (0, 0)),
            pl.BlockSpec((1, hw), lambda b, h, i: (0, 0)),
        ],
        out_specs=pl.BlockSpec((B_TQ, hw), lambda b, h, i: (b * n_qb + i, h)),
        scratch_shapes=[pltpu.VMEM((2, seq, dh), BF16)],
        compiler_params=_params("parallel", "parallel", "arbitrary"),
        name="attn_diff",
    )(proj, proj, proj, cos, sin_signed, lam_vecs, norm_g)


C_TQ = 512


def _attn_c_kernel(q_ref, k_ref, v_ref, o_ref, *, scale):
    s = lax.dot_general(q_ref[...], k_ref[...], NT_DIMS, preferred_element_type=F32) * scale
    e = jnp.exp(s - jnp.max(s, axis=-1, keepdims=True))
    p = (e / jnp.sum(e, axis=-1, keepdims=True)).astype(BF16)
    o_ref[...] = _dot(p, v_ref[...]).astype(o_ref.dtype)


def _attention_c(proj, ckv, *, batch, seq, mem_len, heads, dh, col_q):
    n = proj.shape[0]
    n_qb = seq // C_TQ
    kern = functools.partial(_attn_c_kernel, scale=dh ** -0.5)
    return pl.pallas_call(
        kern,
        out_shape=jax.ShapeDtypeStruct((n, heads * dh), BF16),
        grid=(batch, heads, n_qb),
        in_specs=[
            pl.BlockSpec((C_TQ, dh), lambda b, h, i: (b * n_qb + i, col_q // dh + h)),
            pl.BlockSpec((mem_len, dh), lambda b, h, i: (b, h)),
            pl.BlockSpec((mem_len, dh), lambda b, h, i: (b, heads + h)),
        ],
        out_specs=pl.BlockSpec((C_TQ, dh), lambda b, h, i: (b * n_qb + i, h)),
        compiler_params=_params("parallel", "parallel", "parallel"),
        name="attn_mem",
    )(proj, ckv, ckv)


def _merge_kernel(x_ref, ya_ref, yb_ref, yc_ref, wga_ref, wgb_ref, wgc_ref,
                  bga_ref, bgb_ref, bgc_ref, pa_ref, pb_ref, pc_ref, o_ref):
    x = x_ref[...]
    acc = None
    for wg, bg, y, p in ((wga_ref, bga_ref, ya_ref, pa_ref),
                         (wgb_ref, bgb_ref, yb_ref, pb_ref),
                         (wgc_ref, bgc_ref, yc_ref, pc_ref)):
        gate = jax.nn.sigmoid(_dot(x, wg[...].astype(BF16)) + bg[...])
        term = gate * _dot(y[...], p[...].astype(BF16))
        acc = term if acc is None else acc + term
    o_ref[...] = acc.astype(o_ref.dtype)


def _gated_merge(xb, ya, yb, yc, w_gates, b_gates, pa, pb, pc, *, tm, tn):
    n, d = xb.shape
    nj = d // tn
    row = lambda width: pl.BlockSpec((tm, width), lambda j, i: (i, 0))
    gate_w = lambda br: pl.BlockSpec((d, tn), lambda j, i, br=br: (0, br * nj + j))
    gate_b = lambda br: pl.BlockSpec((1, tn), lambda j, i, br=br: (0, br * nj + j))
    branch_w = lambda width: pl.BlockSpec((width, tn), lambda j, i: (0, j))
    return pl.pallas_call(
        _merge_kernel,
        out_shape=jax.ShapeDtypeStruct((n, d), BF16),
        grid=(nj, n // tm),
        in_specs=[row(d), row(ya.shape[1]), row(yb.shape[1]), row(yc.shape[1]),
                  gate_w(0), gate_w(1), gate_w(2), gate_b(0), gate_b(1), gate_b(2),
                  branch_w(pa.shape[0]), branch_w(pb.shape[0]), branch_w(pc.shape[0])],
        out_specs=pl.BlockSpec((tm, tn), lambda j, i: (i, j)),
        compiler_params=_params("parallel", "parallel"),
        name="gated_merge",
    )(xb, ya, yb, yc, w_gates, w_gates, w_gates, b_gates, b_gates, b_gates, pa, pb, pc)


def _layer_norm(z, g, b):
    mu = jnp.mean(z, axis=-1, keepdims=True)
    zc = z - mu
    var = jnp.mean(zc * zc, axis=-1, keepdims=True)
    return zc * lax.rsqrt(var + LN_EPS) * g + b


def _ln_router_kernel(x_ref, m_ref, g_ref, b_ref, wr_ref, br_ref,
                      h_ref, hp_ref, idx_ref, gate_ref, rank_ref, cnt_ref, carry_ref,
                      *, alpha, n_exp):
    @pl.when(pl.program_id(0) == 0)
    def _():
        carry_ref[...] = jnp.zeros_like(carry_ref)

    tm = x_ref.shape[0]
    h = _layer_norm(alpha * x_ref[...] + m_ref[...], g_ref[...], b_ref[...])
    h_ref[...] = h
    hp_ref[...] = _pack_halves(h)

    h_hi = h.astype(BF16)
    h_lo = (h - h_hi.astype(F32)).astype(BF16)
    w = wr_ref[...]
    r1 = _dot(h_hi, w)
    logits = r1[:, :LANES] + r1[:, LANES:] + _dot(h_lo, w[:, :LANES]) + br_ref[...]

    lane = lax.broadcasted_iota(I32, (tm, LANES), 1)
    lane_f = lane.astype(F32)
    cur = jnp.where(lane < n_exp, logits, -jnp.inf)
    vals, idxs = [], []
    for _ in range(TOP_K):
        mx = jnp.max(cur, axis=-1, keepdims=True)
        ix = jnp.min(jnp.where(cur == mx, lane_f, float(LANES)), axis=-1, keepdims=True).astype(I32)
        vals.append(mx)
        idxs.append(ix)
        cur = jnp.where(lane == ix, -jnp.inf, cur)
    exps = [jnp.exp(v - vals[0]) for v in vals]
    den = exps[0]
    for e in exps[1:]:
        den = den + e

    tri = (lax.broadcasted_iota(I32, (tm, tm), 0) > lax.broadcasted_iota(I32, (tm, tm), 1)).astype(BF16)
    carry = carry_ref[...]
    idx_out = jnp.zeros((tm, LANES), I32)
    gate_out = jnp.zeros((tm, LANES), F32)
    rank_out = jnp.zeros((tm, LANES), I32)
    for k in range(TOP_K):
        onehot = (lane == idxs[k]).astype(F32)
        before = _dot(tri, onehot.astype(BF16)) + carry
        rank = jnp.sum(onehot * before, axis=-1, keepdims=True)
        carry = carry + jnp.sum(onehot, axis=0, keepdims=True)
        idx_out = jnp.where(lane == k, idxs[k], idx_out)
        gate_out = jnp.where(lane == k, exps[k] / den, gate_out)
        rank_out = jnp.where(lane == k, rank.astype(I32), rank_out)
    carry_ref[...] = carry
    idx_ref[...] = idx_out
    gate_ref[...] = gate_out
    rank_ref[...] = rank_out
    cnt_ref[...] = carry


def _ln_router(x, m, g, b, wr_split, br_pad, *, alpha, n_exp, tm):
    n, d = x.shape
    row = pl.BlockSpec((tm, d), lambda i: (i, 0))
    vec = pl.BlockSpec((1, d), lambda i: (0, 0))
    small = pl.BlockSpec((tm, LANES), lambda i: (i, 0))
    kern = functools.partial(_ln_router_kernel, alpha=alpha, n_exp=n_exp)
    return pl.pallas_call(
        kern,
        out_shape=(jax.ShapeDtypeStruct((n, d), F32),
                   jax.ShapeDtypeStruct((n, d // 2), U32),
                   jax.ShapeDtypeStruct((n, LANES), I32),
                   jax.ShapeDtypeStruct((n, LANES), F32),
                   jax.ShapeDtypeStruct((n, LANES), I32),
                   jax.ShapeDtypeStruct((1, LANES), F32)),
        grid=(n // tm,),
        in_specs=[row, row, vec, vec,
                  pl.BlockSpec((d, 2 * LANES), lambda i: (0, 0)),
                  pl.BlockSpec((1, LANES), lambda i: (0, 0))],
        out_specs=(row, pl.BlockSpec((tm, d // 2), lambda i: (i, 0)), small, small, small,
                   pl.BlockSpec((1, LANES), lambda i: (0, 0))),
        scratch_shapes=[pltpu.VMEM((1, LANES), F32)],
        compiler_params=_params("arbitrary"),
        name="ln_router",
    )(x, m, g, b, wr_split, br_pad)


def _row_copy(src, src_row, dst, dst_row, sem):
    return pltpu.make_async_copy(src.at[pl.ds(src_row, 1)], dst.at[pl.ds(dst_row, 1)], sem)


def _dispatch_kernel(dest_ref, hp_ref, xg_ref, sem):
    tm = hp_ref.shape[0]

    def issue(t, carry):
        for k in range(TOP_K):
            _row_copy(hp_ref, t, xg_ref, dest_ref[0, t * TOP_K + k], sem).start()
        return carry

    lax.fori_loop(0, tm, issue, 0)

    def drain(t, carry):
        for k in range(TOP_K):
            _row_copy(hp_ref, 0, xg_ref, 0, sem).wait()
        return carry

    lax.fori_loop(0, tm, drain, 0)


def _dispatch(dest, hp, *, tm):
    n, w = hp.shape
    dest3 = dest.reshape(n // tm, 1, tm * TOP_K)
    return pl.pallas_call(
        _dispatch_kernel,
        out_shape=jax.ShapeDtypeStruct((n * TOP_K, w), U32),
        grid=(n // tm,),
        in_specs=[pl.BlockSpec((None, 1, tm * TOP_K), lambda i: (i, 0, 0), memory_space=pltpu.SMEM),
                  pl.BlockSpec((tm, w), lambda i: (i, 0))],
        out_specs=pl.BlockSpec(memory_space=pl.ANY),
        scratch_shapes=[pltpu.SemaphoreType.DMA(())],
        compiler_params=_params("arbitrary"),
        name="dispatch",
    )(dest3, hp)


def _cumsum_small(x):
    n = x.shape[0]
    keep = jnp.arange(n)[:, None] >= jnp.arange(n)[None, :]
    return jnp.sum(jnp.where(keep, x[None, :], 0), axis=1).astype(x.dtype)


def _lookup(table, idx):
    hit = idx[..., None] == jnp.arange(table.shape[0], dtype=idx.dtype)
    return jnp.sum(jnp.where(hit, table, 0), axis=-1).astype(table.dtype)


def _visit_schedule(counts, n_rows, tr):
    n_exp = counts.shape[0]
    n_tiles = n_rows // tr
    n_vis = n_tiles + n_exp
    gend = _cumsum_small(counts)
    gstart = gend - counts
    first_tile = gstart // tr
    last_tile = jnp.maximum(gend - 1, 0) // tr
    nvis = jnp.where(counts > 0, last_tile - first_tile + 1, 0)
    vend = _cumsum_small(nvis)
    vstart = vend - nvis
    total = vend[-1]
    v = jnp.arange(n_vis, dtype=I32)
    vc = jnp.minimum(v, total - 1)
    e_v = jnp.minimum(jnp.sum((vend[None, :] <= vc[:, None]).astype(I32), axis=1), n_exp - 1)
    tile_v = _lookup(first_tile, e_v) + (vc - _lookup(vstart, e_v))
    lo = jnp.clip(_lookup(gstart, e_v) - tile_v * tr, 0, tr)
    hi = jnp.clip(_lookup(gend, e_v) - tile_v * tr, 0, tr)
    live = v < total
    lo = jnp.where(live, lo, 0).astype(I32)
    hi = jnp.where(live, hi, 0).astype(I32)
    changed = jnp.concatenate([jnp.ones((1,), I32), (e_v[1:] != e_v[:-1]).astype(I32)])
    run = _cumsum_small(changed) - 1
    later_other = (v[None, :] > v[:, None]) & (e_v[None, :] != e_v[:, None])
    nxt_pos = jnp.min(jnp.where(later_other, v[None, :], n_vis), axis=1)
    nxt_e = jnp.where(nxt_pos < n_vis, _lookup(e_v, jnp.minimum(nxt_pos, n_vis - 1)), -1).astype(I32)
    meta = jnp.stack([run[-1] + 1, e_v[0]]).astype(I32)
    return tile_v.astype(I32), e_v.astype(I32), lo, hi, run.astype(I32), nxt_e, meta


def _visit_state(vt, vlo, vhi):
    v = pl.program_id(1)
    lo = vlo[v]
    hi = vhi[v]
    first = jnp.logical_or(v == 0, vt[v] != vt[jnp.maximum(v - 1, 0)])
    return lo, hi, first


def _resident_weights(ve, run, nxt_e, meta, w_hbm, wbuf, wsem, n_chunks):
    c = pl.program_id(0)
    v = pl.program_id(1)
    tn = wbuf.shape[2]
    slot = lax.rem(c * meta[0] + run[v], 2)
    new_run = jnp.logical_or(v == 0, ve[v] != ve[jnp.maximum(v - 1, 0)])

    def fetch(e, chunk, s):
        for cc in range(n_chunks):
            @pl.when(chunk == cc)
            def _(cc=cc):
                pltpu.make_async_copy(w_hbm.at[e, :, pl.ds(cc * tn, tn)], wbuf.at[s], wsem.at[s]).start()

    @pl.when(jnp.logical_and(c == 0, v == 0))
    def _():
        fetch(ve[0], c, slot)

    @pl.when(new_run)
    def _():
        pltpu.make_async_copy(w_hbm.at[0, :, pl.ds(0, tn)], wbuf.at[slot], wsem.at[slot]).wait()
        more_here = nxt_e[v] >= 0

        @pl.when(more_here)
        def _():
            fetch(nxt_e[v], c, 1 - slot)

        @pl.when(jnp.logical_and(jnp.logical_not(more_here), c + 1 < n_chunks))
        def _():
            fetch(meta[1], c + 1, 1 - slot)

    return slot


def _store_rows(o_ref, val, lo, hi, first):
    rows = lax.broadcasted_iota(I32, (o_ref.shape[0], 1), 0)
    mine = (rows >= lo) & (rows < hi)

    @pl.when(first)
    def _():
        o_ref[...] = jnp.where(mine, val, jnp.zeros_like(val))

    @pl.when(jnp.logical_not(first))
    def _():
        o_ref[...] = jnp.where(mine, val, o_ref[...])


SEL_W = 512


def _even_lane_selector():
    r = jnp.arange(SEL_W)[:, None]
    c = jnp.arange(SEL_W // 2)[None, :]
    return (r == 2 * c).astype(BF16)


def _up_kernel(vt, ve, vlo, vhi, run, nxt_e, meta, xg_ref, w1_hbm, b1_ref, sel_ref, o_ref, wbuf, wsem,
               *, n_chunks):
    lo, hi, first = _visit_state(vt, vlo, vhi)
    slot = _resident_weights(ve, run, nxt_e, meta, w1_hbm, wbuf, wsem, n_chunks)

    @pl.when(hi > lo)
    def _():
        half = wbuf.shape[1] // 2
        tn = wbuf.shape[2]
        xa, xb = _unpack_halves(xg_ref[...])
        h = (_dot(xa.astype(BF16), wbuf[slot, :half, :].astype(BF16))
             + _dot(xb.astype(BF16), wbuf[slot, half:, :].astype(BF16)) + b1_ref[...])
        glu = jnp.minimum(h, SWIGLU_LIMIT)
        lin = jnp.clip(h, -SWIGLU_LIMIT, SWIGLU_LIMIT) + 1.0
        gact = glu * jax.nn.sigmoid(SWIGLU_ALPHA * glu)
        parts = []
        for c in range(tn // LANES):
            cs = slice(c * LANES, (c + 1) * LANES)
            parts.append(gact[:, cs] * pltpu.roll(lin[:, cs], LANES - 1, 1))
        inter = jnp.concatenate(parts, axis=1).astype(BF16)
        acts = [_dot(inter[:, s * SEL_W:(s + 1) * SEL_W], sel_ref[...]) for s in range(tn // SEL_W)]
        act = jnp.concatenate(acts, axis=1).astype(o_ref.dtype)
        _store_rows(o_ref, act, lo, hi, first)


def _expert_up(sched, xg, w1, b1, *, tr, tn):
    p_rows, w = xg.shape
    n_exp, d, f2 = w1.shape
    n_vis = sched[0].shape[0]
    n_chunks = f2 // tn
    grid_spec = pltpu.PrefetchScalarGridSpec(
        num_scalar_prefetch=len(sched),
        grid=(n_chunks, n_vis),
        in_specs=[
            pl.BlockSpec((tr, w), lambda c, v, vt, *_: (vt[v], 0)),
            pl.BlockSpec(memory_space=pl.ANY),
            pl.BlockSpec((None, 1, tn), lambda c, v, vt, ve, *_: (ve[v], 0, c)),
            pl.BlockSpec((SEL_W, SEL_W // 2), lambda c, v, *_: (0, 0)),
        ],
        out_specs=pl.BlockSpec((tr, tn // 2), lambda c, v, vt, *_: (vt[v], c)),
        scratch_shapes=[pltpu.VMEM((2, d, tn), F32), pltpu.SemaphoreType.DMA((2,))],
    )
    return pl.pallas_call(
        functools.partial(_up_kernel, n_chunks=n_chunks),
        out_shape=jax.ShapeDtypeStruct((p_rows, f2 // 2), BF16),
        grid_spec=grid_spec,
        compiler_params=_params("arbitrary", "arbitrary"),
        name="expert_up",
    )(*sched, xg, w1, b1.reshape(n_exp, 1, f2), _even_lane_selector())


def _down_kernel(vt, ve, vlo, vhi, run, nxt_e, meta, act_ref, w2_hbm, b2_ref, o_ref, wbuf, wsem, *, n_chunks):
    lo, hi, first = _visit_state(vt, vlo, vhi)
    slot = _resident_weights(ve, run, nxt_e, meta, w2_hbm, wbuf, wsem, n_chunks)

    @pl.when(hi > lo)
    def _():
        y = _dot(act_ref[...], wbuf[slot].astype(BF16)) + b2_ref[...]
        _store_rows(o_ref, _pack_halves(y), lo, hi, first)


def _expert_down(sched, act, w2, b2, *, tr, tn):
    p_rows, f = act.shape
    n_exp, _, d = w2.shape
    n_vis = sched[0].shape[0]
    n_chunks = d // tn
    grid_spec = pltpu.PrefetchScalarGridSpec(
        num_scalar_prefetch=len(sched),
        grid=(n_chunks, n_vis),
        in_specs=[
            pl.BlockSpec((tr, f), lambda c, v, vt, *_: (vt[v], 0)),
            pl.BlockSpec(memory_space=pl.ANY),
            pl.BlockSpec((None, 1, tn), lambda c, v, vt, ve, *_: (ve[v], 0, c)),
        ],
        out_specs=pl.BlockSpec((tr, tn // 2), lambda c, v, vt, *_: (vt[v], c)),
        scratch_shapes=[pltpu.VMEM((2, f, tn), F32), pltpu.SemaphoreType.DMA((2,))],
    )
    return pl.pallas_call(
        functools.partial(_down_kernel, n_chunks=n_chunks),
        out_shape=jax.ShapeDtypeStruct((p_rows, d // 2), U32),
        grid_spec=grid_spec,
        compiler_params=_params("arbitrary", "arbitrary"),
        name="expert_down",
    )(*sched, act, w2, b2.reshape(n_exp, 1, d))


def _combine_kernel(dest_ref, h_ref, gate_ref, g_ref, b_ref, y_ref, o_ref, buf_ref, sem,
                    *, alpha, chunk):
    tm = h_ref.shape[0]

    def issue(t, carry):
        for k in range(TOP_K):
            pltpu.make_async_copy(y_ref.at[pl.ds(dest_ref[0, t * TOP_K + k], 1)],
                                  buf_ref.at[k, pl.ds(t, 1)], sem).start()
        return carry

    lax.fori_loop(0, tm, issue, 0)

    def drain(t, carry):
        for k in range(TOP_K):
            pltpu.make_async_copy(y_ref.at[pl.ds(0, 1)], buf_ref.at[k, pl.ds(0, 1)], sem).wait()
        return carry

    lax.fori_loop(0, tm, drain, 0)

    gates = gate_ref[...]
    acc_hi = None
    acc_lo = None
    for k in range(TOP_K):
        hi, lo = _unpack_halves(buf_ref[k])
        gk = gates[:, k:k + 1]
        acc_hi = gk * hi if acc_hi is None else acc_hi + gk * hi
        acc_lo = gk * lo if acc_lo is None else acc_lo + gk * lo
    hw = chunk // 2
    pieces = []
    for c in range(acc_hi.shape[1] // hw):
        pieces.append(acc_hi[:, c * hw:(c + 1) * hw])
        pieces.append(acc_lo[:, c * hw:(c + 1) * hw])
    ffn = jnp.concatenate(pieces, axis=1)
    o_ref[...] = _layer_norm(alpha * h_ref[...] + ffn, g_ref[...], b_ref[...]).astype(o_ref.dtype)


def _combine(dest, h, gates, g, b, y, *, alpha, tm, chunk):
    n, d = h.shape
    dest3 = dest.reshape(n // tm, 1, tm * TOP_K)
    kern = functools.partial(_combine_kernel, alpha=alpha, chunk=chunk)
    return pl.pallas_call(
        kern,
        out_shape=jax.ShapeDtypeStruct((n, d), F32),
        grid=(n // tm,),
        in_specs=[pl.BlockSpec((None, 1, tm * TOP_K), lambda i: (i, 0, 0), memory_space=pltpu.SMEM),
                  pl.BlockSpec((tm, d), lambda i: (i, 0)),
                  pl.BlockSpec((tm, LANES), lambda i: (i, 0)),
                  pl.BlockSpec((1, d), lambda i: (0, 0)),
                  pl.BlockSpec((1, d), lambda i: (0, 0)),
                  pl.BlockSpec(memory_space=pl.ANY)],
        out_specs=pl.BlockSpec((tm, d), lambda i: (i, 0)),
        scratch_shapes=[pltpu.VMEM((TOP_K, tm, d // 2), U32), pltpu.SemaphoreType.DMA(())],
        compiler_params=_params("arbitrary"),
        name="combine_ln",
    )(dest3, h, gates, g, b, y)


def _tiles(n_tokens, d_model, d_expert):
    return dict(
        cast_tm=min(512, n_tokens),
        proj=dict(tm=min(1024, n_tokens), tn=512),
        memkv=dict(tm=512, tn=512),
        merge=dict(tm=min(512, n_tokens), tn=256),
        out=dict(tm=min(1024, n_tokens), tn=512),
        ln_tm=min(256, n_tokens),
        dispatch_tm=min(256, n_tokens),
        moe_tr=256,
        up_tn=min(1024, 2 * d_expert),
        down_tn=min(2048, d_model),
        combine_tm=min(128, n_tokens),
    )


def _layer(h, mem2, lw, *, batch, seq, mem_len, lambda_init, alpha):
    n, d = h.shape
    t = _tiles(n, d, lw["w_mlp2"].shape[1])
    a_width = A_HEADS * A_HEAD_DIM
    b_width = B_HEADS * 2 * B_HEAD_DIM
    col = dict(a_q=0, a_k=a_width, a_v=2 * a_width, b_q=3 * a_width, b_k=3 * a_width + b_width,
               b_v=3 * a_width + 2 * b_width, c_q=3 * a_width + 3 * b_width)

    xb = _cast_bf16(h, t["cast_tm"])
    proj = _matmul(xb, lw["w_in"], out_dtype=BF16, name="in_proj", **t["proj"])
    ckv = _matmul(mem2, lw["w_mem_kv"], out_dtype=BF16, name="mem_kv",
                  tm=min(t["memkv"]["tm"], mem2.shape[0]), tn=t["memkv"]["tn"])

    ya = _attention_a(proj, _band_bias_base(lw["rel_bias"]), batch=batch, seq=seq,
                      heads=A_HEADS, dh=A_HEAD_DIM, col_q=col["a_q"], col_k=col["a_k"], col_v=col["a_v"])
    cos, sin_signed = _rope_tables(seq, B_HEAD_DIM)
    lam_vecs = jnp.stack([lw["lambda_q1"], lw["lambda_k1"], lw["lambda_q2"], lw["lambda_k2"]]).astype(F32)
    yb = _attention_b(proj, cos, sin_signed, lam_vecs, lw["diff_norm_g"].reshape(1, -1),
                      batch=batch, seq=seq, heads=B_HEADS, dh=B_HEAD_DIM,
                      col_q=col["b_q"], col_k=col["b_k"], col_v=col["b_v"], lambda_init=lambda_init)
    yc = _attention_c(proj, ckv, batch=batch, seq=seq, mem_len=mem_len, heads=C_HEADS,
                      dh=C_HEAD_DIM, col_q=col["c_q"])

    merged = _gated_merge(xb, ya, yb, yc, lw["w_gates"], lw["b_gates"].reshape(1, -1),
                          lw["w_branch_a"], lw["w_branch_b"], lw["w_branch_c"], **t["merge"])
    mix = _matmul(merged, lw["w_o"], out_dtype=F32, name="out_proj", **t["out"])

    n_exp = lw["w_router"].shape[1]
    wr = jnp.pad(lw["w_router"], ((0, 0), (0, LANES - n_exp)))
    wr_hi = wr.astype(BF16)
    wr_lo = (wr - wr_hi.astype(F32)).astype(BF16)
    br = jnp.pad(lw["b_router"], (0, LANES - n_exp)).reshape(1, LANES)
    h1, h1_packed, top_idx, gates, rank, cnt = _ln_router(
        h, mix, lw["ln1_g"].reshape(1, -1), lw["ln1_b"].reshape(1, -1),
        jnp.concatenate([wr_hi, wr_lo], axis=1), br, alpha=alpha, n_exp=n_exp, tm=t["ln_tm"])

    counts = cnt[0, :n_exp].astype(I32)
    gstart = _cumsum_small(counts) - counts
    dest = _lookup(gstart, top_idx[:, :TOP_K]) + rank[:, :TOP_K]
    sched = _visit_schedule(counts, n * TOP_K, t["moe_tr"])

    xg = _dispatch(dest, h1_packed, tm=t["dispatch_tm"])
    act = _expert_up(sched, xg, lw["w_mlp1"], lw["b_mlp1"], tr=t["moe_tr"], tn=t["up_tn"])
    y = _expert_down(sched, act, lw["w_mlp2"], lw["b_mlp2"], tr=t["moe_tr"], tn=t["down_tn"])
    return _combine(dest, h1, gates, lw["ln2_g"].reshape(1, -1), lw["ln2_b"].reshape(1, -1), y,
                    alpha=alpha, tm=t["combine_tm"], chunk=t["down_tn"])


def kernel(x, mem, w_in, w_mem_kv, rel_bias, lambda_q1, lambda_k1, lambda_q2, lambda_k2, diff_norm_g,
           w_branch_a, w_branch_b, w_branch_c, w_gates, b_gates, w_o, ln1_g, ln1_b, w_router, b_router,
           w_mlp1, b_mlp1, w_mlp2, b_mlp2, ln2_g, ln2_b):
    batch, seq, d = x.shape
    mem_len = mem.shape[1]
    depth = w_in.shape[0]
    alpha = (2 * depth) ** 0.25
    stacked = dict(w_in=w_in, w_mem_kv=w_mem_kv, rel_bias=rel_bias, lambda_q1=lambda_q1,
                   lambda_k1=lambda_k1, lambda_q2=lambda_q2, lambda_k2=lambda_k2, diff_norm_g=diff_norm_g,
                   w_branch_a=w_branch_a, w_branch_b=w_branch_b, w_branch_c=w_branch_c, w_gates=w_gates,
                   b_gates=b_gates, w_o=w_o, ln1_g=ln1_g, ln1_b=ln1_b, w_router=w_router,
                   b_router=b_router, w_mlp1=w_mlp1, b_mlp1=b_mlp1, w_mlp2=w_mlp2, b_mlp2=b_mlp2,
                   ln2_g=ln2_g, ln2_b=ln2_b)
    h = x.reshape(batch * seq, d)
    mem2 = mem.reshape(batch * mem_len, d)
    for l in range(depth):
        lw = {name: w[l] for name, w in stacked.items()}
        lambda_init = 0.8 - 0.6 * math.exp(-0.3 * l)
        h = _layer(h, mem2, lw, batch=batch, seq=seq, mem_len=mem_len, lambda_init=lambda_init, alpha=alpha)
    return h.reshape(batch, seq, d)
```

```python
import functools
import math

import jax
import jax.numpy as jnp
from jax import lax
from jax.experimental import pallas as pl
from jax.experimental.pallas import tpu as pltpu

F32 = jnp.float32
BF16 = jnp.bfloat16
U32 = jnp.uint32
I32 = jnp.int32

CHUNK = 64
LEFT_CHUNKS = 8
MAX_REL = 128
A_HEADS = 16
A_HEAD_DIM = 128
B_HEADS = 4
B_HEAD_DIM = 128
C_HEADS = 4
C_HEAD_DIM = 256
N_BRANCHES = 3
ROPE_THETA = 10000.0
TOP_K = 4
SWIGLU_LIMIT = 7.0
SWIGLU_ALPHA = 1.702
LN_EPS = 1e-5
RMS_EPS = 1e-5
MASK_VALUE = -1e30

V7X_VMEM_BYTES = 64 * 1024 * 1024
V7X_VMEM_LIMIT = V7X_VMEM_BYTES - 8 * 1024 * 1024
LANES = 128

NT_DIMS = (((1,), (1,)), ((), ()))


def _params(*semantics):
    return pltpu.CompilerParams(dimension_semantics=semantics,
                                vmem_limit_bytes=V7X_VMEM_LIMIT)


def _dot(a, b):
    return jnp.dot(a, b, preferred_element_type=F32)


def _pack_halves(x):
    w = x.shape[1] // 2
    hi = lax.bitcast_convert_type(x[:, :w].astype(jnp.bfloat16).astype(F32), U32)
    lo = lax.bitcast_convert_type(x[:, w:].astype(jnp.bfloat16).astype(F32), U32)
    return hi | (lo >> 16)


def _unpack_halves(p):
    hi = lax.bitcast_convert_type(p & jnp.uint32(0xFFFF0000), F32)
    lo = lax.bitcast_convert_type(p << 16, F32)
    return hi, lo


def _cast_kernel(x_ref, o_ref):
    o_ref[...] = x_ref[...].astype(o_ref.dtype)


def _cast_bf16(x, tm):
    m, d = x.shape
    return pl.pallas_call(
        _cast_kernel,
        out_shape=jax.ShapeDtypeStruct((m, d), BF16),
        grid=(m // tm,),
        in_specs=[pl.BlockSpec((tm, d), lambda i: (i, 0))],
        out_specs=pl.BlockSpec((tm, d), lambda i: (i, 0)),
        compiler_params=_params("parallel"),
        name="cast_bf16",
    )(x)


def _mm_kernel(a_ref, w_ref, o_ref):
    a = a_ref[...].astype(BF16)
    o_ref[...] = _dot(a, w_ref[...].astype(BF16)).astype(o_ref.dtype)


def _matmul(a, w, *, tm, tn, out_dtype, name):
    m, k = a.shape
    n = w.shape[1]
    return pl.pallas_call(
        _mm_kernel,
        out_shape=jax.ShapeDtypeStruct((m, n), out_dtype),
        grid=(n // tn, m // tm),
        in_specs=[pl.BlockSpec((tm, k), lambda j, i: (i, 0)),
                  pl.BlockSpec((k, tn), lambda j, i: (0, j))],
        out_specs=pl.BlockSpec((tm, tn), lambda j, i: (i, j)),
        compiler_params=_params("parallel", "parallel"),
        name=name,
    )(a, w)


A_TQ = 2 * CHUNK
A_WIN = (LEFT_CHUNKS + 2) * CHUNK
A_VARIANTS = LEFT_CHUNKS * CHUNK // A_TQ + 1


A_BASE_W = A_WIN + A_TQ


def _band_bias_base(rel_bias):
    reach = A_WIN
    ext =jnp.pad(rel_bias.astype(F32), ((0, 0), (reach - MAX_REL, reach - MAX_REL)), mode="edge")
    rev = ext[:, ::-1]
    rows = []
    for v in range(A_VARIANTS):
        c = rev[:, reach - A_TQ * v - A_TQ: reach - A_TQ * v + A_WIN]
        rows.append(jnp.concatenate([c[:, A_TQ:], c[:, :A_TQ]], axis=1))
    return jnp.stack(rows)[:, :, None, :]


def _attn_a_kernel(q_ref, k_ref, v_ref, base_ref, o_ref, tb_ref, *, heads, dh, scale):
    i = pl.program_id(2)

    @pl.when(i == 0)
    def _():
        r = lax.broadcasted_iota(I32, (A_TQ, A_WIN), 0)
        j = lax.broadcasted_iota(I32, (A_TQ, A_WIN), 1)
        for var in range(A_VARIANTS):
            cdiff = (A_TQ * var + r) // CHUNK - j // CHUNK
            valid = (cdiff >= 0) & (cdiff <= LEFT_CHUNKS)
            for h in range(heads):
                rows = jnp.broadcast_to(base_ref[var, h], (A_TQ, A_BASE_W))
                toeplitz = pltpu.roll(rows, 0, 1, stride=1, stride_axis=0)[:, :A_WIN]
                tb_ref[var, h] = jnp.where(valid, toeplitz, MASK_VALUE)

    var = jnp.minimum(i, A_VARIANTS - 1)
    start = pl.multiple_of(jnp.maximum(i - (A_VARIANTS - 1), 0) * A_TQ, A_TQ)
    for h in range(heads):
        cs = slice(h * dh, (h + 1) * dh)
        q = q_ref[:, cs]
        k = k_ref[pl.ds(start, A_WIN), cs]
        v = v_ref[pl.ds(start, A_WIN), cs]
        s = lax.dot_general(q, k, NT_DIMS, preferred_element_type=F32) * scale + tb_ref[var, h]
        m = jnp.max(s, axis=-1, keepdims=True)
        p = jnp.exp(s - m)
        l = jnp.sum(p, axis=-1, keepdims=True)
        o = _dot(p.astype(BF16), v)
        o_ref[:, cs] = (o / l).astype(o_ref.dtype)


def _attention_a(proj, base, *, batch, seq, heads, dh, col_q, col_k, col_v, heads_per_step=8):
    n = proj.shape[0]
    gw = heads_per_step * dh
    n_groups = heads // heads_per_step
    n_qb = seq // A_TQ
    kern = functools.partial(_attn_a_kernel, heads=heads_per_step, dh=dh, scale=dh ** -0.5)
    return pl.pallas_call(
        kern,
        out_shape=jax.ShapeDtypeStruct((n, heads * dh), BF16),
        grid=(batch, n_groups, n_qb),
        in_specs=[
            pl.BlockSpec((A_TQ, gw), lambda b, g, i: (b * n_qb + i, col_q // gw + g)),
            pl.BlockSpec((seq, gw), lambda b, g, i: (b, col_k // gw + g)),
            pl.BlockSpec((seq, gw), lambda b, g, i: (b, col_v // gw + g)),
            pl.BlockSpec((A_VARIANTS, heads_per_step, 1, A_BASE_W), lambda b, g, i: (0, g, 0, 0)),
        ],
        out_specs=pl.BlockSpec((A_TQ, gw), lambda b, g, i: (b * n_qb + i, g)),
        scratch_shapes=[pltpu.VMEM((A_VARIANTS, heads_per_step, A_TQ, A_WIN), F32)],
        compiler_params=_params("parallel", "parallel", "arbitrary"),
        name="attn_band",
    )(proj, proj, proj, base)


B_TQ = 256


def _rope_tables(seq, dim):
    inv = 1.0 / (ROPE_THETA ** (jnp.arange(0, dim, 2, dtype=F32) / dim))
    ang = jnp.arange(seq, dtype=F32)[:, None] * inv[None, :]
    ang = jnp.concatenate([ang, ang], -1)
    sign = jnp.where(jnp.arange(dim) < dim // 2, -1.0, 1.0).astype(F32)
    return jnp.cos(ang), jnp.sin(ang) * sign[None, :]


def _rope(x, cos, sin_signed):
    return x * cos + pltpu.roll(x, x.shape[1] // 2, 1) * sin_signed


def _attn_b_kernel(q_ref, k_ref, v_ref, cos_ref, sin_ref, lam_ref, g_ref, o_ref, krot_ref,
                   *, dh, scale, lambda_init):
    qi = pl.program_id(2)
    seq = k_ref.shape[0]

    @pl.when(qi == 0)
    def _():
        for m in range(2):
            kf = k_ref[:, m * dh:(m + 1) * dh].astype(F32)
            krot_ref[m] = _rope(kf, cos_ref[...], sin_ref[...]).astype(BF16)

    lv = lam_ref[...]
    lam = (jnp.exp(jnp.sum(lv[0:1] * lv[1:2], axis=-1, keepdims=True))
           - jnp.exp(jnp.sum(lv[2:3] * lv[3:4], axis=-1, keepdims=True)) + lambda_init)

    def block(blk):
        row0 = blk * B_TQ
        kl = row0 + B_TQ
        cos_q = cos_ref[row0:kl, :]
        sin_q = sin_ref[row0:kl, :]
        q_chunk = (row0 + lax.broadcasted_iota(I32, (B_TQ, kl), 0)) // CHUNK
        k_chunk = lax.broadcasted_iota(I32, (B_TQ, kl), 1) // CHUNK
        allowed = k_chunk <= q_chunk
        probs = []
        for m in range(2):
            qf = q_ref[:, m * dh:(m + 1) * dh].astype(F32)
            qr = _rope(qf, cos_q, sin_q).astype(BF16)
            s = lax.dot_general(qr, krot_ref[m, :kl, :], NT_DIMS, preferred_element_type=F32) * scale
            s = jnp.where(allowed, s, MASK_VALUE)
            e = jnp.exp(s - jnp.max(s, axis=-1, keepdims=True))
            probs.append(e / jnp.sum(e, axis=-1, keepdims=True))
        w = (probs[0] - lam * probs[1]).astype(BF16)
        o = _dot(w, v_ref[:kl, :])
        ms = jnp.mean(o * o, axis=-1, keepdims=True)
        y = o * lax.rsqrt(ms + RMS_EPS) * g_ref[...] * (1.0 - lambda_init)
        o_ref[...] = y.astype(o_ref.dtype)

    for blk in range(seq // B_TQ):
        pl.when(qi == blk)(functools.partial(block, blk))


def _attention_b(proj, cos, sin_signed, lam_vecs, norm_g, *, batch, seq, heads, dh,
                 col_q, col_k, col_v, lambda_init):
    n = proj.shape[0]
    hw = 2 * dh
    n_qb = seq // B_TQ
    kern = functools.partial(_attn_b_kernel, dh=dh, scale=dh ** -0.5, lambda_init=lambda_init)
    return pl.pallas_call(
        kern,
        out_shape=jax.ShapeDtypeStruct((n, heads * hw), BF16),
        grid=(batch, heads, n_qb),
        in_specs=[
            pl.BlockSpec((B_TQ, hw), lambda b, h, i: (b * n_qb + i, col_q // hw + h)),
            pl.BlockSpec((seq, hw), lambda b, h, i: (b, col_k // hw + h)),
            pl.BlockSpec((seq, hw), lambda b, h, i: (b, col_v // hw + h)),
            pl.BlockSpec((seq, dh), lambda b, h, i: (0, 0)),
            pl.BlockSpec((seq, dh), lambda b, h, i: (0, 0)),
            pl.BlockSpec((4, dh), lambda b, h, i: (0, 0)),
            pl.BlockSpec((1, hw), lambda b, h, i: (0, 0)),
        ],
        out_specs=pl.BlockSpec((B_TQ, hw), lambda b, h, i: (b * n_qb + i, h)),
        scratch_shapes=[pltpu.VMEM((2, seq, dh), BF16)],
        compiler_params=_params("parallel", "parallel", "arbitrary"),
        name="attn_diff",
    )(proj, proj, proj, cos, sin_signed, lam_vecs, norm_g)


C_TQ = 512


def _attn_c_kernel(q_ref, k_ref, v_ref, o_ref, *, scale):
    s = lax.dot_general(q_ref[...], k_ref[...], NT_DIMS, preferred_element_type=F32) * scale
    e = jnp.exp(s - jnp.max(s, axis=-1, keepdims=True))
    p = (e / jnp.sum(e, axis=-1, keepdims=True)).astype(BF16)
    o_ref[...] = _dot(p, v_ref[...]).astype(o_ref.dtype)


def _attention_c(proj, ckv, *, batch, seq, mem_len, heads, dh, col_q):
    n = proj.shape[0]
    n_qb = seq // C_TQ
    kern = functools.partial(_attn_c_kernel, scale=dh ** -0.5)
    return pl.pallas_call(
        kern,
        out_shape=jax.ShapeDtypeStruct((n, heads * dh), BF16),
        grid=(batch, heads, n_qb),
        in_specs=[
            pl.BlockSpec((C_TQ, dh), lambda b, h, i: (b * n_qb + i, col_q // dh + h)),
            pl.BlockSpec((mem_len, dh), lambda b, h, i: (b, h)),
            pl.BlockSpec((mem_len, dh), lambda b, h, i: (b, heads + h)),
        ],
        out_specs=pl.BlockSpec((C_TQ, dh), lambda b, h, i: (b * n_qb + i, h)),
        compiler_params=_params("parallel", "parallel", "parallel"),
        name="attn_mem",
    )(proj, ckv, ckv)


def _merge_kernel(x_ref, ya_ref, yb_ref, yc_ref, wga_ref, wgb_ref, wgc_ref,
                  bga_ref, bgb_ref, bgc_ref, pa_ref, pb_ref, pc_ref, o_ref):
    x = x_ref[...]
    acc = None
    for wg, bg, y, p in ((wga_ref, bga_ref, ya_ref, pa_ref),
                         (wgb_ref, bgb_ref, yb_ref, pb_ref),
                         (wgc_ref, bgc_ref, yc_ref, pc_ref)):
        gate = jax.nn.sigmoid(_dot(x, wg[...].astype(BF16)) + bg[...])
        term = gate * _dot(y[...], p[...].astype(BF16))
        acc = term if acc is None else acc + term
    o_ref[...] = acc.astype(o_ref.dtype)


def _gated_merge(xb, ya, yb, yc, w_gates, b_gates, pa, pb, pc, *, tm, tn):
    n, d = xb.shape
    nj = d // tn
    row = lambda width: pl.BlockSpec((tm, width), lambda j, i: (i, 0))
    gate_w = lambda br: pl.BlockSpec((d, tn), lambda j, i, br=br: (0, br * nj + j))
    gate_b = lambda br: pl.BlockSpec((1, tn), lambda j, i, br=br: (0, br * nj + j))
    branch_w = lambda width: pl.BlockSpec((width, tn), lambda j, i: (0, j))
    return pl.pallas_call(
        _merge_kernel,
        out_shape=jax.ShapeDtypeStruct((n, d), BF16),
        grid=(nj, n // tm),
        in_specs=[row(d), row(ya.shape[1]), row(yb.shape[1]), row(yc.shape[1]),
                  gate_w(0), gate_w(1), gate_w(2), gate_b(0), gate_b(1), gate_b(2),
                  branch_w(pa.shape[0]), branch_w(pb.shape[0]), branch_w(pc.shape[0])],
        out_specs=pl.BlockSpec((tm, tn), lambda j, i: (i, j)),
        compiler_params=_params("parallel", "parallel"),
        name="gated_merge",
    )(xb, ya, yb, yc, w_gates, w_gates, w_gates, b_gates, b_gates, b_gates, pa, pb, pc)


def _layer_norm(z, g, b):
    mu = jnp.mean(z, axis=-1, keepdims=True)
    zc = z - mu
    var = jnp.mean(zc * zc, axis=-1, keepdims=True)
    return zc * lax.rsqrt(var + LN_EPS) * g + b


def _ln_router_kernel(x_ref, m_ref, g_ref, b_ref, wr_ref, br_ref,
                      h_ref, hp_ref, idx_ref, gate_ref, rank_ref, cnt_ref, carry_ref,
                      *, alpha, n_exp):
    @pl.when(pl.program_id(0) == 0)
    def _():
        carry_ref[...] = jnp.zeros_like(carry_ref)

    tm = x_ref.shape[0]
    h = _layer_norm(alpha * x_ref[...] + m_ref[...], g_ref[...], b_ref[...])
    h_ref[...] = h
    hp_ref[...] = _pack_halves(h)

    h_hi = h.astype(BF16)
    h_lo = (h - h_hi.astype(F32)).astype(BF16)
    w = wr_ref[...]
    r1 = _dot(h_hi, w)
    logits = r1[:, :LANES] + r1[:, LANES:] + _dot(h_lo, w[:, :LANES]) + br_ref[...]

    lane = lax.broadcasted_iota(I32, (tm, LANES), 1)
    lane_f = lane.astype(F32)
    cur = jnp.where(lane < n_exp, logits, -jnp.inf)
    vals, idxs = [], []
    for _ in range(TOP_K):
        mx = jnp.max(cur, axis=-1, keepdims=True)
        ix = jnp.min(jnp.where(cur == mx, lane_f, float(LANES)), axis=-1, keepdims=True).astype(I32)
        vals.append(mx)
        idxs.append(ix)
        cur = jnp.where(lane == ix, -jnp.inf, cur)
    exps = [jnp.exp(v - vals[0]) for v in vals]
    den = exps[0]
    for e in exps[1:]:
        den = den + e

    tri = (lax.broadcasted_iota(I32, (tm, tm), 0) > lax.broadcasted_iota(I32, (tm, tm), 1)).astype(BF16)
    carry = carry_ref[...]
    idx_out = jnp.zeros((tm, LANES), I32)
    gate_out = jnp.zeros((tm, LANES), F32)
    rank_out = jnp.zeros((tm, LANES), I32)
    for k in range(TOP_K):
        onehot = (lane == idxs[k]).astype(F32)
        before = _dot(tri, onehot.astype(BF16)) + carry
        rank = jnp.sum(onehot * before, axis=-1, keepdims=True)
        carry = carry + jnp.sum(onehot, axis=0, keepdims=True)
        idx_out = jnp.where(lane == k, idxs[k], idx_out)
        gate_out = jnp.where(lane == k, exps[k] / den, gate_out)
        rank_out = jnp.where(lane == k, rank.astype(I32), rank_out)
    carry_ref[...] = carry
    idx_ref[...] = idx_out
    gate_ref[...] = gate_out
    rank_ref[...] = rank_out
    cnt_ref[...] = carry


def _ln_router(x, m, g, b, wr_split, br_pad, *, alpha, n_exp, tm):
    n, d = x.shape
    row = pl.BlockSpec((tm, d), lambda i: (i, 0))
    vec = pl.BlockSpec((1, d), lambda i: (0, 0))
    small = pl.BlockSpec((tm, LANES), lambda i: (i, 0))
    kern = functools.partial(_ln_router_kernel, alpha=alpha, n_exp=n_exp)
    return pl.pallas_call(
        kern,
        out_shape=(jax.ShapeDtypeStruct((n, d), F32),
                   jax.ShapeDtypeStruct((n, d // 2), U32),
                   jax.ShapeDtypeStruct((n, LANES), I32),
                   jax.ShapeDtypeStruct((n, LANES), F32),
                   jax.ShapeDtypeStruct((n, LANES), I32),
                   jax.ShapeDtypeStruct((1, LANES), F32)),
        grid=(n // tm,),
        in_specs=[row, row, vec, vec,
                  pl.BlockSpec((d, 2 * LANES), lambda i: (0, 0)),
                  pl.BlockSpec((1, LANES), lambda i: (0, 0))],
        out_specs=(row, pl.BlockSpec((tm, d // 2), lambda i: (i, 0)), small, small, small,
                   pl.BlockSpec((1, LANES), lambda i: (0, 0))),
        scratch_shapes=[pltpu.VMEM((1, LANES), F32)],
        compiler_params=_params("arbitrary"),
        name="ln_router",
    )(x, m, g, b, wr_split, br_pad)


def _row_copy(src, src_row, dst, dst_row, sem):
    return pltpu.make_async_copy(src.at[pl.ds(src_row, 1)], dst.at[pl.ds(dst_row, 1)], sem)


def _dispatch_kernel(dest_ref, hp_ref, xg_ref, sem):
    tm = hp_ref.shape[0]

    def issue(t, carry):
        for k in range(TOP_K):
            _row_copy(hp_ref, t, xg_ref, dest_ref[0, t * TOP_K + k], sem).start()
        return carry

    lax.fori_loop(0, tm, issue, 0)

    def drain(t, carry):
        for k in range(TOP_K):
            _row_copy(hp_ref, 0, xg_ref, 0, sem).wait()
        return carry

    lax.fori_loop(0, tm, drain, 0)


def _dispatch(dest, hp, *, tm):
    n, w = hp.shape
    dest3 = dest.reshape(n // tm, 1, tm * TOP_K)
    return pl.pallas_call(
        _dispatch_kernel,
        out_shape=jax.ShapeDtypeStruct((n * TOP_K, w), U32),
        grid=(n // tm,),
        in_specs=[pl.BlockSpec((None, 1, tm * TOP_K), lambda i: (i, 0, 0), memory_space=pltpu.SMEM),
                  pl.BlockSpec((tm, w), lambda i: (i, 0))],
        out_specs=pl.BlockSpec(memory_space=pl.ANY),
        scratch_shapes=[pltpu.SemaphoreType.DMA(())],
        compiler_params=_params("arbitrary"),
        name="dispatch",
    )(dest3, hp)


def _cumsum_small(x):
    n = x.shape[0]
    keep = jnp.arange(n)[:, None] >= jnp.arange(n)[None, :]
    return jnp.sum(jnp.where(keep, x[None, :], 0), axis=1).astype(x.dtype)


def _lookup(table, idx):
    hit = idx[..., None] == jnp.arange(table.shape[0], dtype=idx.dtype)
    return jnp.sum(jnp.where(hit, table, 0), axis=-1).astype(table.dtype)


def _visit_schedule(counts, n_rows, tr):
    n_exp = counts.shape[0]
    n_tiles = n_rows // tr
    n_vis = n_tiles + n_exp
    gend = _cumsum_small(counts)
    gstart = gend - counts
    first_tile = gstart // tr
    last_tile = jnp.maximum(gend - 1, 0) // tr
    nvis = jnp.where(counts > 0, last_tile - first_tile + 1, 0)
    vend = _cumsum_small(nvis)
    vstart = vend - nvis
    total = vend[-1]
    v = jnp.arange(n_vis, dtype=I32)
    vc = jnp.minimum(v, total - 1)
    e_v = jnp.minimum(jnp.sum((vend[None, :] <= vc[:, None]).astype(I32), axis=1), n_exp - 1)
    tile_v = _lookup(first_tile, e_v) + (vc - _lookup(vstart, e_v))
    lo = jnp.clip(_lookup(gstart, e_v) - tile_v * tr, 0, tr)
    hi = jnp.clip(_lookup(gend, e_v) - tile_v * tr, 0, tr)
    live = v < total
    lo = jnp.where(live, lo, 0).astype(I32)
    hi = jnp.where(live, hi, 0).astype(I32)
    changed = jnp.concatenate([jnp.ones((1,), I32), (e_v[1:] != e_v[:-1]).astype(I32)])
    run = _cumsum_small(changed) - 1
    later_other = (v[None, :] > v[:, None]) & (e_v[None, :] != e_v[:, None])
    nxt_pos = jnp.min(jnp.where(later_other, v[None, :], n_vis), axis=1)
    nxt_e = jnp.where(nxt_pos < n_vis, _lookup(e_v, jnp.minimum(nxt_pos, n_vis - 1)), -1).astype(I32)
    meta = jnp.stack([run[-1] + 1, e_v[0]]).astype(I32)
    return tile_v.astype(I32), e_v.astype(I32), lo, hi, run.astype(I32), nxt_e, meta


def _visit_state(vt, vlo, vhi):
    v = pl.program_id(1)
    lo = vlo[v]
    hi = vhi[v]
    first = jnp.logical_or(v == 0, vt[v] != vt[jnp.maximum(v - 1, 0)])
    return lo, hi, first


def _resident_weights(ve, run, nxt_e, meta, w_hbm, wbuf, wsem, n_chunks):
    c = pl.program_id(0)
    v = pl.program_id(1)
    tn = wbuf.shape[2]
    slot = lax.rem(c * meta[0] + run[v], 2)
    new_run = jnp.logical_or(v == 0, ve[v] != ve[jnp.maximum(v - 1, 0)])

    def fetch(e, chunk, s):
        for cc in range(n_chunks):
            @pl.when(chunk == cc)
            def _(cc=cc):
                pltpu.make_async_copy(w_hbm.at[e, :, pl.ds(cc * tn, tn)], wbuf.at[s], wsem.at[s]).start()

    @pl.when(jnp.logical_and(c == 0, v == 0))
    def _():
        fetch(ve[0], c, slot)

    @pl.when(new_run)
    def _():
        pltpu.make_async_copy(w_hbm.at[0, :, pl.ds(0, tn)], wbuf.at[slot], wsem.at[slot]).wait()
        more_here = nxt_e[v] >= 0

        @pl.when(more_here)
        def _():
            fetch(nxt_e[v], c, 1 - slot)

        @pl.when(jnp.logical_and(jnp.logical_not(more_here), c + 1 < n_chunks))
        def _():
            fetch(meta[1], c + 1, 1 - slot)

    return slot


def _store_rows(o_ref, val, lo, hi, first):
    rows = lax.broadcasted_iota(I32, (o_ref.shape[0], 1), 0)
    mine = (rows >= lo) & (rows < hi)

    @pl.when(first)
    def _():
        o_ref[...] = jnp.where(mine, val, jnp.zeros_like(val))

    @pl.when(jnp.logical_not(first))
    def _():
        o_ref[...] = jnp.where(mine, val, o_ref[...])


MOE_SUB = 256


def _for_live_sub_blocks(o_ref, lo, hi, first, body):
    for sb in range(o_ref.shape[0] // MOE_SUB):
        r0 = sb * MOE_SUB
        rows = pl.ds(r0, MOE_SUB)
        lo_s = jnp.clip(lo - r0, 0, MOE_SUB)
        hi_s = jnp.clip(hi - r0, 0, MOE_SUB)
        live = hi_s > lo_s

        @pl.when(live)
        def _(rows=rows, lo_s=lo_s, hi_s=hi_s):
            body(rows, lo_s, hi_s)

        @pl.when(jnp.logical_and(first, jnp.logical_and(hi > lo, jnp.logical_not(live))))
        def _(rows=rows):
            o_ref[rows, :] = jnp.zeros((MOE_SUB, o_ref.shape[1]), o_ref.dtype)


SEL_W = 512


def _even_lane_selector():
    r = jnp.arange(SEL_W)[:, None]
    c = jnp.arange(SEL_W // 2)[None, :]
    return (r == 2 * c).astype(BF16)


def _up_kernel(vt, ve, vlo, vhi, run, nxt_e, meta, xg_ref, w1_hbm, b1_ref, sel_ref, o_ref, wbuf, wsem,
               *, n_chunks):
    lo, hi, first = _visit_state(vt, vlo, vhi)
    slot = _resident_weights(ve, run, nxt_e, meta, w1_hbm, wbuf, wsem, n_chunks)
    half = wbuf.shape[1] // 2
    tn = wbuf.shape[2]

    def sub_block(rows, lo_s, hi_s):
        xa, xb = _unpack_halves(xg_ref[rows, :])
        h = (_dot(xa.astype(BF16), wbuf[slot, :half, :].astype(BF16))
             + _dot(xb.astype(BF16), wbuf[slot, half:, :].astype(BF16)) + b1_ref[...])
        glu = jnp.minimum(h, SWIGLU_LIMIT)
        lin = jnp.clip(h, -SWIGLU_LIMIT, SWIGLU_LIMIT) + 1.0
        gact = glu * jax.nn.sigmoid(SWIGLU_ALPHA * glu)
        parts = []
        for c in range(tn // LANES):
            cs = slice(c * LANES, (c + 1) * LANES)
            parts.append(gact[:, cs] * pltpu.roll(lin[:, cs], LANES - 1, 1))
        inter = jnp.concatenate(parts, axis=1).astype(BF16)
        acts = [_dot(inter[:, s * SEL_W:(s + 1) * SEL_W], sel_ref[...]) for s in range(tn // SEL_W)]
        act = jnp.concatenate(acts, axis=1).astype(o_ref.dtype)
        _store_rows(o_ref.at[rows], act, lo_s, hi_s, first)

    _for_live_sub_blocks(o_ref, lo, hi, first, sub_block)


def _expert_up(sched, xg, w1, b1, *, tr, tn):
    p_rows, w = xg.shape
    n_exp, d, f2 = w1.shape
    n_vis = sched[0].shape[0]
    n_chunks = f2 // tn
    grid_spec = pltpu.PrefetchScalarGridSpec(
        num_scalar_prefetch=len(sched),
        grid=(n_chunks, n_vis),
        in_specs=[
            pl.BlockSpec((tr, w), lambda c, v, vt, *_: (vt[v], 0)),
            pl.BlockSpec(memory_space=pl.ANY),
            pl.BlockSpec((None, 1, tn), lambda c, v, vt, ve, *_: (ve[v], 0, c)),
            pl.BlockSpec((SEL_W, SEL_W // 2), lambda c, v, *_: (0, 0)),
        ],
        out_specs=pl.BlockSpec((tr, tn // 2), lambda c, v, vt, *_: (vt[v], c)),
        scratch_shapes=[pltpu.VMEM((2, d, tn), F32), pltpu.SemaphoreType.DMA((2,))],
    )
    return pl.pallas_call(
        functools.partial(_up_kernel, n_chunks=n_chunks),
        out_shape=jax.ShapeDtypeStruct((p_rows, f2 // 2), BF16),
        grid_spec=grid_spec,
        compiler_params=_params("arbitrary", "arbitrary"),
        name="expert_up",
    )(*sched, xg, w1, b1.reshape(n_exp, 1, f2), _even_lane_selector())


def _down_kernel(vt, ve, vlo, vhi, run, nxt_e, meta, act_ref, w2_hbm, b2_ref, o_ref, wbuf, wsem, *, n_chunks):
    lo, hi, first = _visit_state(vt, vlo, vhi)
    slot = _resident_weights(ve, run, nxt_e, meta, w2_hbm, wbuf, wsem, n_chunks)

    def sub_block(rows, lo_s, hi_s):
        y = _dot(act_ref[rows, :], wbuf[slot].astype(BF16)) + b2_ref[...]
        _store_rows(o_ref.at[rows], _pack_halves(y), lo_s, hi_s, first)

    _for_live_sub_blocks(o_ref, lo, hi, first, sub_block)


def _expert_down(sched, act, w2, b2, *, tr, tn):
    p_rows, f = act.shape
    n_exp, _, d = w2.shape
    n_vis = sched[0].shape[0]
    n_chunks = d // tn
    grid_spec = pltpu.PrefetchScalarGridSpec(
        num_scalar_prefetch=len(sched),
        grid=(n_chunks, n_vis),
        in_specs=[
            pl.BlockSpec((tr, f), lambda c, v, vt, *_: (vt[v], 0)),
            pl.BlockSpec(memory_space=pl.ANY),
            pl.BlockSpec((None, 1, tn), lambda c, v, vt, ve, *_: (ve[v], 0, c)),
        ],
        out_specs=pl.BlockSpec((tr, tn // 2), lambda c, v, vt, *_: (vt[v], c)),
        scratch_shapes=[pltpu.VMEM((2, f, tn), F32), pltpu.SemaphoreType.DMA((2,))],
    )
    return pl.pallas_call(
        functools.partial(_down_kernel, n_chunks=n_chunks),
        out_shape=jax.ShapeDtypeStruct((p_rows, d // 2), U32),
        grid_spec=grid_spec,
        compiler_params=_params("arbitrary", "arbitrary"),
        name="expert_down",
    )(*sched, act, w2, b2.reshape(n_exp, 1, d))


def _combine_kernel(dest_ref, h_ref, gate_ref, g_ref, b_ref, y_ref, o_ref, buf_ref, sem,
                    *, alpha, chunk):
    tm = h_ref.shape[0]

    def issue(t, carry):
        for k in range(TOP_K):
            pltpu.make_async_copy(y_ref.at[pl.ds(dest_ref[0, t * TOP_K + k], 1)],
                                  buf_ref.at[k, pl.ds(t, 1)], sem).start()
        return carry

    lax.fori_loop(0, tm, issue, 0)

    def drain(t, carry):
        for k in range(TOP_K):
            pltpu.make_async_copy(y_ref.at[pl.ds(0, 1)], buf_ref.at[k, pl.ds(0, 1)], sem).wait()
        return carry

    lax.fori_loop(0, tm, drain, 0)

    gates = gate_ref[...]
    acc_hi = None
    acc_lo = None
    for k in range(TOP_K):
        hi, lo = _unpack_halves(buf_ref[k])
        gk = gates[:, k:k + 1]
        acc_hi = gk * hi if acc_hi is None else acc_hi + gk * hi
        acc_lo = gk * lo if acc_lo is None else acc_lo + gk * lo
    hw = chunk // 2
    pieces = []
    for c in range(acc_hi.shape[1] // hw):
        pieces.append(acc_hi[:, c * hw:(c + 1) * hw])
        pieces.append(acc_lo[:, c * hw:(c + 1) * hw])
    ffn = jnp.concatenate(pieces, axis=1)
    o_ref[...] = _layer_norm(alpha * h_ref[...] + ffn, g_ref[...], b_ref[...]).astype(o_ref.dtype)


def _combine(dest, h, gates, g, b, y, *, alpha, tm, chunk):
    n, d = h.shape
    dest3 = dest.reshape(n // tm, 1, tm * TOP_K)
    kern = functools.partial(_combine_kernel, alpha=alpha, chunk=chunk)
    return pl.pallas_call(
        kern,
        out_shape=jax.ShapeDtypeStruct((n, d), F32),
        grid=(n // tm,),
        in_specs=[pl.BlockSpec((None, 1, tm * TOP_K), lambda i: (i, 0, 0), memory_space=pltpu.SMEM),
                  pl.BlockSpec((tm, d), lambda i: (i, 0)),
                  pl.BlockSpec((tm, LANES), lambda i: (i, 0)),
                  pl.BlockSpec((1, d), lambda i: (0, 0)),
                  pl.BlockSpec((1, d), lambda i: (0, 0)),
                  pl.BlockSpec(memory_space=pl.ANY)],
        out_specs=pl.BlockSpec((tm, d), lambda i: (i, 0)),
        scratch_shapes=[pltpu.VMEM((TOP_K, tm, d // 2), U32), pltpu.SemaphoreType.DMA(())],
        compiler_params=_params("arbitrary"),
        name="combine_ln",
    )(dest3, h, gates, g, b, y)


def _tiles(n_tokens, d_model, d_expert):
    return dict(
        cast_tm=min(512, n_tokens),
        proj=dict(tm=min(1024, n_tokens), tn=512),
        memkv=dict(tm=512, tn=512),
        merge=dict(tm=min(512, n_tokens), tn=256),
        out=dict(tm=min(1024, n_tokens), tn=512),
        ln_tm=min(256, n_tokens),
        dispatch_tm=min(256, n_tokens),
        moe_tr=512,
        up_tn=min(1024, 2 * d_expert),
        down_tn=min(2048, d_model),
        combine_tm=min(128, n_tokens),
    )


def _layer(h, mem2, lw, *, batch, seq, mem_len, lambda_init, alpha):
    n, d = h.shape
    t = _tiles(n, d, lw["w_mlp2"].shape[1])
    a_width = A_HEADS * A_HEAD_DIM
    b_width = B_HEADS * 2 * B_HEAD_DIM
    col = dict(a_q=0, a_k=a_width, a_v=2 * a_width, b_q=3 * a_width, b_k=3 * a_width + b_width,
               b_v=3 * a_width + 2 * b_width, c_q=3 * a_width + 3 * b_width)

    xb = _cast_bf16(h, t["cast_tm"])
    proj = _matmul(xb, lw["w_in"], out_dtype=BF16, name="in_proj", **t["proj"])
    ckv = _matmul(mem2, lw["w_mem_kv"], out_dtype=BF16, name="mem_kv",
                  tm=min(t["memkv"]["tm"], mem2.shape[0]), tn=t["memkv"]["tn"])

    ya = _attention_a(proj, _band_bias_base(lw["rel_bias"]), batch=batch, seq=seq,
                      heads=A_HEADS, dh=A_HEAD_DIM, col_q=col["a_q"], col_k=col["a_k"], col_v=col["a_v"])
    cos, sin_signed = _rope_tables(seq, B_HEAD_DIM)
    lam_vecs = jnp.stack([lw["lambda_q1"], lw["lambda_k1"], lw["lambda_q2"], lw["lambda_k2"]]).astype(F32)
    yb = _attention_b(proj, cos, sin_signed, lam_vecs, lw["diff_norm_g"].reshape(1, -1),
                      batch=batch, seq=seq, heads=B_HEADS, dh=B_HEAD_DIM,
                      col_q=col["b_q"], col_k=col["b_k"], col_v=col["b_v"], lambda_init=lambda_init)
    yc = _attention_c(proj, ckv, batch=batch, seq=seq, mem_len=mem_len, heads=C_HEADS,
                      dh=C_HEAD_DIM, col_q=col["c_q"])

    merged = _gated_merge(xb, ya, yb, yc, lw["w_gates"], lw["b_gates"].reshape(1, -1),
                          lw["w_branch_a"], lw["w_branch_b"], lw["w_branch_c"], **t["merge"])
    mix = _matmul(merged, lw["w_o"], out_dtype=F32, name="out_proj", **t["out"])

    n_exp = lw["w_router"].shape[1]
    wr = jnp.pad(lw["w_router"], ((0, 0), (0, LANES - n_exp)))
    wr_hi = wr.astype(BF16)
    wr_lo = (wr - wr_hi.astype(F32)).astype(BF16)
    br = jnp.pad(lw["b_router"], (0, LANES - n_exp)).reshape(1, LANES)
    h1, h1_packed, top_idx, gates, rank, cnt = _ln_router(
        h, mix, lw["ln1_g"].reshape(1, -1), lw["ln1_b"].reshape(1, -1),
        jnp.concatenate([wr_hi, wr_lo], axis=1), br, alpha=alpha, n_exp=n_exp, tm=t["ln_tm"])

    counts = cnt[0, :n_exp].astype(I32)
    gstart = _cumsum_small(counts) - counts
    dest = _lookup(gstart, top_idx[:, :TOP_K]) + rank[:, :TOP_K]
    sched = _visit_schedule(counts, n * TOP_K, t["moe_tr"])

    xg = _dispatch(dest, h1_packed, tm=t["dispatch_tm"])
    act = _expert_up(sched, xg, lw["w_mlp1"], lw["b_mlp1"], tr=t["moe_tr"], tn=t["up_tn"])
    y = _expert_down(sched, act, lw["w_mlp2"], lw["b_mlp2"], tr=t["moe_tr"], tn=t["down_tn"])
    return _combine(dest, h1, gates, lw["ln2_g"].reshape(1, -1), lw["ln2_b"].reshape(1, -1), y,
                    alpha=alpha, tm=t["combine_tm"], chunk=t["down_tn"])


def kernel(x, mem, w_in, w_mem_kv, rel_bias, lambda_q1, lambda_k1, lambda_q2, lambda_k2, diff_norm_g,
           w_branch_a, w_branch_b, w_branch_c, w_gates, b_gates, w_o, ln1_g, ln1_b, w_router, b_router,
           w_mlp1, b_mlp1, w_mlp2, b_mlp2, ln2_g, ln2_b):
    batch, seq, d = x.shape
    mem_len = mem.shape[1]
    depth = w_in.shape[0]
    alpha = (2 * depth) ** 0.25
    stacked = dict(w_in=w_in, w_mem_kv=w_mem_kv, rel_bias=rel_bias, lambda_q1=lambda_q1,
                   lambda_k1=lambda_k1, lambda_q2=lambda_q2, lambda_k2=lambda_k2, diff_norm_g=diff_norm_g,
                   w_branch_a=w_branch_a, w_branch_b=w_branch_b, w_branch_c=w_branch_c, w_gates=w_gates,
                   b_gates=b_gates, w_o=w_o, ln1_g=ln1_g, ln1_b=ln1_b, w_router=w_router,
                   b_router=b_router, w_mlp1=w_mlp1, b_mlp1=b_mlp1, w_mlp2=w_mlp2, b_mlp2=b_mlp2,
                   ln2_g=ln2_g, ln2_b=ln2_b)
    h = x.reshape(batch * seq, d)
    mem2 = mem.reshape(batch * mem_len, d)
    for l in range(depth):
        lw = {name: w[l] for name, w in stacked.items()}
        lambda_init = 0.8 - 0.6 * math.exp(-0.3 * l)
        h = _layer(h, mem2, lw, batch=batch, seq=seq, mem_len=mem_len, lambda_init=lambda_init, alpha=alpha)
    return h.reshape(batch, seq, d)
```

```python
import functools
import math

import jax
import jax.numpy as jnp
from jax import lax
from jax.experimental import pallas as pl
from jax.experimental.pallas import tpu as pltpu

F32 = jnp.float32
BF16 = jnp.bfloat16
U32 = jnp.uint32
I32 = jnp.int32

CHUNK = 64
LEFT_CHUNKS = 8
MAX_REL = 128
A_HEADS = 16
A_HEAD_DIM = 128
B_HEADS = 4
B_HEAD_DIM = 128
C_HEADS = 4
C_HEAD_DIM = 256
N_BRANCHES = 3
ROPE_THETA = 10000.0
TOP_K = 4
SWIGLU_LIMIT = 7.0
SWIGLU_ALPHA = 1.702
LN_EPS = 1e-5
RMS_EPS = 1e-5
MASK_VALUE = -1e30

V7X_VMEM_BYTES = 64 * 1024 * 1024
V7X_VMEM_LIMIT = V7X_VMEM_BYTES - 8 * 1024 * 1024
LANES = 128

NT_DIMS = (((1,), (1,)), ((), ()))


def _params(*semantics):
    return pltpu.CompilerParams(dimension_semantics=semantics,
                                vmem_limit_bytes=V7X_VMEM_LIMIT)


def _dot(a, b):
    return jnp.dot(a, b, preferred_element_type=F32)


def _pack_halves(x):
    w = x.shape[1] // 2
    hi = lax.bitcast_convert_type(x[:, :w].astype(jnp.bfloat16).astype(F32), U32)
    lo = lax.bitcast_convert_type(x[:, w:].astype(jnp.bfloat16).astype(F32), U32)
    return hi | (lo >> 16)


def _unpack_halves(p):
    hi = lax.bitcast_convert_type(p & jnp.uint32(0xFFFF0000), F32)
    lo = lax.bitcast_convert_type(p << 16, F32)
    return hi, lo


def _cast_kernel(x_ref, o_ref):
    o_ref[...] = x_ref[...].astype(o_ref.dtype)


def _cast_bf16(x, tm):
    m, d = x.shape
    return pl.pallas_call(
        _cast_kernel,
        out_shape=jax.ShapeDtypeStruct((m, d), BF16),
        grid=(m // tm,),
        in_specs=[pl.BlockSpec((tm, d), lambda i: (i, 0))],
        out_specs=pl.BlockSpec((tm, d), lambda i: (i, 0)),
        compiler_params=_params("parallel"),
        name="cast_bf16",
    )(x)


def _mm_kernel(a_ref, w_ref, o_ref):
    a = a_ref[...].astype(BF16)
    o_ref[...] = _dot(a, w_ref[...].astype(BF16)).astype(o_ref.dtype)


def _matmul(a, w, *, tm, tn, out_dtype, name):
    m, k = a.shape
    n = w.shape[1]
    return pl.pallas_call(
        _mm_kernel,
        out_shape=jax.ShapeDtypeStruct((m, n), out_dtype),
        grid=(n // tn, m // tm),
        in_specs=[pl.BlockSpec((tm, k), lambda j, i: (i, 0)),
                  pl.BlockSpec((k, tn), lambda j, i: (0, j))],
        out_specs=pl.BlockSpec((tm, tn), lambda j, i: (i, j)),
        compiler_params=_params("parallel", "parallel"),
        name=name,
    )(a, w)


A_TQ = 2 * CHUNK
A_WIN = (LEFT_CHUNKS + 2) * CHUNK
A_VARIANTS = LEFT_CHUNKS * CHUNK // A_TQ + 1


A_BASE_W = A_WIN + A_TQ


def _band_bias_base(rel_bias):
    reach = A_WIN
    ext =jnp.pad(rel_bias.astype(F32), ((0, 0), (reach - MAX_REL, reach - MAX_REL)), mode="edge")
    rev = ext[:, ::-1]
    rows = []
    for v in range(A_VARIANTS):
        c = rev[:, reach - A_TQ * v - A_TQ: reach - A_TQ * v + A_WIN]
        rows.append(jnp.concatenate([c[:, A_TQ:], c[:, :A_TQ]], axis=1))
    return jnp.stack(rows)[:, :, None, :]


def _attn_a_kernel(q_ref, k_ref, v_ref, base_ref, o_ref, tb_ref, *, heads, dh, scale):
    i = pl.program_id(2)

    @pl.when(i == 0)
    def _():
        r = lax.broadcasted_iota(I32, (A_TQ, A_WIN), 0)
        j = lax.broadcasted_iota(I32, (A_TQ, A_WIN), 1)
        for var in range(A_VARIANTS):
            cdiff = (A_TQ * var + r) // CHUNK - j // CHUNK
            valid = (cdiff >= 0) & (cdiff <= LEFT_CHUNKS)
            for h in range(heads):
                rows = jnp.broadcast_to(base_ref[var, h], (A_TQ, A_BASE_W))
                toeplitz = pltpu.roll(rows, 0, 1, stride=1, stride_axis=0)[:, :A_WIN]
                tb_ref[var, h] = jnp.where(valid, toeplitz, MASK_VALUE)

    var = jnp.minimum(i, A_VARIANTS - 1)
    start = pl.multiple_of(jnp.maximum(i - (A_VARIANTS - 1), 0) * A_TQ, A_TQ)
    for h in range(heads):
        cs = slice(h * dh, (h + 1) * dh)
        q = q_ref[:, cs]
        k = k_ref[pl.ds(start, A_WIN), cs]
        v = v_ref[pl.ds(start, A_WIN), cs]
        s = lax.dot_general(q, k, NT_DIMS, preferred_element_type=F32) * scale + tb_ref[var, h]
        m = jnp.max(s, axis=-1, keepdims=True)
        p = jnp.exp(s - m)
        l = jnp.sum(p, axis=-1, keepdims=True)
        o = _dot(p.astype(BF16), v)
        o_ref[:, cs] = (o / l).astype(o_ref.dtype)


def _attention_a(proj, base, *, batch, seq, heads, dh, col_q, col_k, col_v, heads_per_step=8):
    n = proj.shape[0]
    gw = heads_per_step * dh
    n_groups = heads // heads_per_step
    n_qb = seq // A_TQ
    kern = functools.partial(_attn_a_kernel, heads=heads_per_step, dh=dh, scale=dh ** -0.5)
    return pl.pallas_call(
        kern,
        out_shape=jax.ShapeDtypeStruct((n, heads * dh), BF16),
        grid=(batch, n_groups, n_qb),
        in_specs=[
            pl.BlockSpec((A_TQ, gw), lambda b, g, i: (b * n_qb + i, col_q // gw + g)),
            pl.BlockSpec((seq, gw), lambda b, g, i: (b, col_k // gw + g)),
            pl.BlockSpec((seq, gw), lambda b, g, i: (b, col_v // gw + g)),
            pl.BlockSpec((A_VARIANTS, heads_per_step, 1, A_BASE_W), lambda b, g, i: (0, g, 0, 0)),
        ],
        out_specs=pl.BlockSpec((A_TQ, gw), lambda b, g, i: (b * n_qb + i, g)),
        scratch_shapes=[pltpu.VMEM((A_VARIANTS, heads_per_step, A_TQ, A_WIN), F32)],
        compiler_params=_params("parallel", "parallel", "arbitrary"),
        name="attn_band",
    )(proj, proj, proj, base)


B_TQ = 256


def _rope_tables(seq, dim):
    inv = 1.0 / (ROPE_THETA ** (jnp.arange(0, dim, 2, dtype=F32) / dim))
    ang = jnp.arange(seq, dtype=F32)[:, None] * inv[None, :]
    ang = jnp.concatenate([ang, ang], -1)
    sign = jnp.where(jnp.arange(dim) < dim // 2, -1.0, 1.0).astype(F32)
    return jnp.cos(ang), jnp.sin(ang) * sign[None, :]


def _rope(x, cos, sin_signed):
    return x * cos + pltpu.roll(x, x.shape[1] // 2, 1) * sin_signed


def _attn_b_kernel(q_ref, k_ref, v_ref, cos_ref, sin_ref, lam_ref, g_ref, o_ref, krot_ref,
                   *, dh, scale, lambda_init):
    qi = pl.program_id(2)
    seq = k_ref.shape[0]

    @pl.when(qi == 0)
    def _():
        for m in range(2):
            kf = k_ref[:, m * dh:(m + 1) * dh].astype(F32)
            krot_ref[m] = _rope(kf, cos_ref[...], sin_ref[...]).astype(BF16)

    lv = lam_ref[...]
    lam = (jnp.exp(jnp.sum(lv[0:1] * lv[1:2], axis=-1, keepdims=True))
           - jnp.exp(jnp.sum(lv[2:3] * lv[3:4], axis=-1, keepdims=True)) + lambda_init)

    def block(blk):
        row0 = blk * B_TQ
        kl = row0 + B_TQ
        cos_q = cos_ref[row0:kl, :]
        sin_q = sin_ref[row0:kl, :]
        q_chunk = (row0 + lax.broadcasted_iota(I32, (B_TQ, kl), 0)) // CHUNK
        k_chunk = lax.broadcasted_iota(I32, (B_TQ, kl), 1) // CHUNK
        allowed = k_chunk <= q_chunk
        probs = []
        for m in range(2):
            qf = q_ref[:, m * dh:(m + 1) * dh].astype(F32)
            qr = _rope(qf, cos_q, sin_q).astype(BF16)
            s = lax.dot_general(qr, krot_ref[m, :kl, :], NT_DIMS, preferred_element_type=F32) * scale
            s = jnp.where(allowed, s, MASK_VALUE)
            e = jnp.exp(s - jnp.max(s, axis=-1, keepdims=True))
            probs.append(e / jnp.sum(e, axis=-1, keepdims=True))
        w = (probs[0] - lam * probs[1]).astype(BF16)
        o = _dot(w, v_ref[:kl, :])
        ms = jnp.mean(o * o, axis=-1, keepdims=True)
        y = o * lax.rsqrt(ms + RMS_EPS) * g_ref[...] * (1.0 - lambda_init)
        o_ref[...] = y.astype(o_ref.dtype)

    for blk in range(seq // B_TQ):
        pl.when(qi == blk)(functools.partial(block, blk))


def _attention_b(proj, cos, sin_signed, lam_vecs, norm_g, *, batch, seq, heads, dh,
                 col_q, col_k, col_v, lambda_init):
    n = proj.shape[0]
    hw = 2 * dh
    n_qb = seq // B_TQ
    kern = functools.partial(_attn_b_kernel, dh=dh, scale=dh ** -0.5, lambda_init=lambda_init)
    return pl.pallas_call(
        kern,
        out_shape=jax.ShapeDtypeStruct((n, heads * hw), BF16),
        grid=(batch, heads, n_qb),
        in_specs=[
            pl.BlockSpec((B_TQ, hw), lambda b, h, i: (b * n_qb + i, col_q // hw + h)),
            pl.BlockSpec((seq, hw), lambda b, h, i: (b, col_k // hw + h)),
            pl.BlockSpec((seq, hw), lambda b, h, i: (b, col_v // hw + h)),
            pl.BlockSpec((seq, dh), lambda b, h, i: (0, 0)),
            pl.BlockSpec((seq, dh), lambda b, h, i: (0, 0)),
            pl.BlockSpec((4, dh), lambda b, h, i: (0, 0)),
            pl.BlockSpec((1, hw), lambda b, h, i: (0, 0)),
        ],
        out_specs=pl.BlockSpec((B_TQ, hw), lambda b, h, i: (b * n_qb + i, h)),
        scratch_shapes=[pltpu.VMEM((2, seq, dh), BF16)],
        compiler_params=_params("parallel", "parallel", "arbitrary"),
        name="attn_diff",
    )(proj, proj, proj, cos, sin_signed, lam_vecs, norm_g)


C_TQ = 512


def _attn_c_kernel(q_ref, k_ref, v_ref, o_ref, *, scale):
    s = lax.dot_general(q_ref[...], k_ref[...], NT_DIMS, preferred_element_type=F32) * scale
    e = jnp.exp(s - jnp.max(s, axis=-1, keepdims=True))
    p = (e / jnp.sum(e, axis=-1, keepdims=True)).astype(BF16)
    o_ref[...] = _dot(p, v_ref[...]).astype(o_ref.dtype)


def _attention_c(proj, ckv, *, batch, seq, mem_len, heads, dh, col_q):
    n = proj.shape[0]
    n_qb = seq // C_TQ
    kern = functools.partial(_attn_c_kernel, scale=dh ** -0.5)
    return pl.pallas_call(
        kern,
        out_shape=jax.ShapeDtypeStruct((n, heads * dh), BF16),
        grid=(batch, heads, n_qb),
        in_specs=[
            pl.BlockSpec((C_TQ, dh), lambda b, h, i: (b * n_qb + i, col_q // dh + h)),
            pl.BlockSpec((mem_len, dh), lambda b, h, i: (b, h)),
            pl.BlockSpec((mem_len, dh), lambda b, h, i: (b, heads + h)),
        ],
        out_specs=pl.BlockSpec((C_TQ, dh), lambda b, h, i: (b * n_qb + i, h)),
        compiler_params=_params("parallel", "parallel", "parallel"),
        name="attn_mem",
    )(proj, ckv, ckv)


def _merge_kernel(x_ref, ya_ref, yb_ref, yc_ref, wga_ref, wgb_ref, wgc_ref,
                  bga_ref, bgb_ref, bgc_ref, pa_ref, pb_ref, pc_ref, o_ref):
    x = x_ref[...]
    acc = None
    for wg, bg, y, p in ((wga_ref, bga_ref, ya_ref, pa_ref),
                         (wgb_ref, bgb_ref, yb_ref, pb_ref),
                         (wgc_ref, bgc_ref, yc_ref, pc_ref)):
        gate = jax.nn.sigmoid(_dot(x, wg[...].astype(BF16)) + bg[...])
        term = gate * _dot(y[...], p[...].astype(BF16))
        acc = term if acc is None else acc + term
    o_ref[...] = acc.astype(o_ref.dtype)


def _gated_merge(xb, ya, yb, yc, w_gates, b_gates, pa, pb, pc, *, tm, tn):
    n, d = xb.shape
    nj = d // tn
    row = lambda width: pl.BlockSpec((tm, width), lambda j, i: (i, 0))
    gate_w = lambda br: pl.BlockSpec((d, tn), lambda j, i, br=br: (0, br * nj + j))
    gate_b = lambda br: pl.BlockSpec((1, tn), lambda j, i, br=br: (0, br * nj + j))
    branch_w = lambda width: pl.BlockSpec((width, tn), lambda j, i: (0, j))
    return pl.pallas_call(
        _merge_kernel,
        out_shape=jax.ShapeDtypeStruct((n, d), BF16),
        grid=(nj, n // tm),
        in_specs=[row(d), row(ya.shape[1]), row(yb.shape[1]), row(yc.shape[1]),
                  gate_w(0), gate_w(1), gate_w(2), gate_b(0), gate_b(1), gate_b(2),
                  branch_w(pa.shape[0]), branch_w(pb.shape[0]), branch_w(pc.shape[0])],
        out_specs=pl.BlockSpec((tm, tn), lambda j, i: (i, j)),
        compiler_params=_params("parallel", "parallel"),
        name="gated_merge",
    )(xb, ya, yb, yc, w_gates, w_gates, w_gates, b_gates, b_gates, b_gates, pa, pb, pc)


def _layer_norm(z, g, b):
    mu = jnp.mean(z, axis=-1, keepdims=True)
    zc = z - mu
    var = jnp.mean(zc * zc, axis=-1, keepdims=True)
    return zc * lax.rsqrt(var + LN_EPS) * g + b


def _ln_router_kernel(x_ref, m_ref, g_ref, b_ref, wr_ref, br_ref,
                      h_ref, hp_ref, idx_ref, gate_ref, rank_ref, cnt_ref, carry_ref,
                      *, alpha, n_exp):
    @pl.when(pl.program_id(0) == 0)
    def _():
        carry_ref[...] = jnp.zeros_like(carry_ref)

    tm = x_ref.shape[0]
    h = _layer_norm(alpha * x_ref[...] + m_ref[...], g_ref[...], b_ref[...])
    h_ref[...] = h
    hp_ref[...] = _pack_halves(h)

    h_hi = h.astype(BF16)
    h_lo = (h - h_hi.astype(F32)).astype(BF16)
    w = wr_ref[...]
    r1 = _dot(h_hi, w)
    logits = r1[:, :LANES] + r1[:, LANES:] + _dot(h_lo, w[:, :LANES]) + br_ref[...]

    lane = lax.broadcasted_iota(I32, (tm, LANES), 1)
    lane_f = lane.astype(F32)
    cur = jnp.where(lane < n_exp, logits, -jnp.inf)
    vals, idxs = [], []
    for _ in range(TOP_K):
        mx = jnp.max(cur, axis=-1, keepdims=True)
        ix = jnp.min(jnp.where(cur == mx, lane_f, float(LANES)), axis=-1, keepdims=True).astype(I32)
        vals.append(mx)
        idxs.append(ix)
        cur = jnp.where(lane == ix, -jnp.inf, cur)
    exps = [jnp.exp(v - vals[0]) for v in vals]
    den = exps[0]
    for e in exps[1:]:
        den = den + e

    tri = (lax.broadcasted_iota(I32, (tm, tm), 0) > lax.broadcasted_iota(I32, (tm, tm), 1)).astype(BF16)
    carry = carry_ref[...]
    idx_out = jnp.zeros((tm, LANES), I32)
    gate_out = jnp.zeros((tm, LANES), F32)
    rank_out = jnp.zeros((tm, LANES), I32)
    for k in range(TOP_K):
        onehot = (lane == idxs[k]).astype(F32)
        before = _dot(tri, onehot.astype(BF16)) + carry
        rank = jnp.sum(onehot * before, axis=-1, keepdims=True)
        carry = carry + jnp.sum(onehot, axis=0, keepdims=True)
        idx_out = jnp.where(lane == k, idxs[k], idx_out)
        gate_out = jnp.where(lane == k, exps[k] / den, gate_out)
        rank_out = jnp.where(lane == k, rank.astype(I32), rank_out)
    carry_ref[...] = carry
    idx_ref[...] = idx_out
    gate_ref[...] = gate_out
    rank_ref[...] = rank_out
    cnt_ref[...] = carry


def _ln_router(x, m, g, b, wr_split, br_pad, *, alpha, n_exp, tm):
    n, d = x.shape
    row = pl.BlockSpec((tm, d), lambda i: (i, 0))
    vec = pl.BlockSpec((1, d), lambda i: (0, 0))
    small = pl.BlockSpec((tm, LANES), lambda i: (i, 0))
    kern = functools.partial(_ln_router_kernel, alpha=alpha, n_exp=n_exp)
    return pl.pallas_call(
        kern,
        out_shape=(jax.ShapeDtypeStruct((n, d), F32),
                   jax.ShapeDtypeStruct((n, d // 2), U32),
                   jax.ShapeDtypeStruct((n, LANES), I32),
                   jax.ShapeDtypeStruct((n, LANES), F32),
                   jax.ShapeDtypeStruct((n, LANES), I32),
                   jax.ShapeDtypeStruct((1, LANES), F32)),
        grid=(n // tm,),
        in_specs=[row, row, vec, vec,
                  pl.BlockSpec((d, 2 * LANES), lambda i: (0, 0)),
                  pl.BlockSpec((1, LANES), lambda i: (0, 0))],
        out_specs=(row, pl.BlockSpec((tm, d // 2), lambda i: (i, 0)), small, small, small,
                   pl.BlockSpec((1, LANES), lambda i: (0, 0))),
        scratch_shapes=[pltpu.VMEM((1, LANES), F32)],
        compiler_params=_params("arbitrary"),
        name="ln_router",
    )(x, m, g, b, wr_split, br_pad)


def _row_copy(src, src_row, dst, dst_row, sem):
    return pltpu.make_async_copy(src.at[pl.ds(src_row, 1)], dst.at[pl.ds(dst_row, 1)], sem)


def _dispatch_kernel(dest_ref, hp_ref, xg_ref, sem):
    tm = hp_ref.shape[0]

    def issue(t, carry):
        for k in range(TOP_K):
            _row_copy(hp_ref, t, xg_ref, dest_ref[0, t * TOP_K + k], sem).start()
        return carry

    lax.fori_loop(0, tm, issue, 0)

    def drain(t, carry):
        for k in range(TOP_K):
            _row_copy(hp_ref, 0, xg_ref, 0, sem).wait()
        return carry

    lax.fori_loop(0, tm, drain, 0)


def _dispatch(dest, hp, *, tm):
    n, w = hp.shape
    dest3 = dest.reshape(n // tm, 1, tm * TOP_K)
    return pl.pallas_call(
        _dispatch_kernel,
        out_shape=jax.ShapeDtypeStruct((n * TOP_K, w), U32),
        grid=(n // tm,),
        in_specs=[pl.BlockSpec((None, 1, tm * TOP_K), lambda i: (i, 0, 0), memory_space=pltpu.SMEM),
                  pl.BlockSpec((tm, w), lambda i: (i, 0))],
        out_specs=pl.BlockSpec(memory_space=pl.ANY),
        scratch_shapes=[pltpu.SemaphoreType.DMA(())],
        compiler_params=_params("arbitrary"),
        name="dispatch",
    )(dest3, hp)


def _cumsum_small(x):
    n = x.shape[0]
    keep = jnp.arange(n)[:, None] >= jnp.arange(n)[None, :]
    return jnp.sum(jnp.where(keep, x[None, :], 0), axis=1).astype(x.dtype)


def _lookup(table, idx):
    hit = idx[..., None] == jnp.arange(table.shape[0], dtype=idx.dtype)
    return jnp.sum(jnp.where(hit, table, 0), axis=-1).astype(table.dtype)


def _visit_schedule(counts, n_rows, tr):
    n_exp = counts.shape[0]
    n_tiles = n_rows // tr
    n_vis = n_tiles + n_exp
    gend = _cumsum_small(counts)
    gstart = gend - counts
    first_tile = gstart // tr
    last_tile = jnp.maximum(gend - 1, 0) // tr
    nvis = jnp.where(counts > 0, last_tile - first_tile + 1, 0)
    vend = _cumsum_small(nvis)
    vstart = vend - nvis
    total = vend[-1]
    v = jnp.arange(n_vis, dtype=I32)
    vc = jnp.minimum(v, total - 1)
    e_v = jnp.minimum(jnp.sum((vend[None, :] <= vc[:, None]).astype(I32), axis=1), n_exp - 1)
    tile_v = _lookup(first_tile, e_v) + (vc - _lookup(vstart, e_v))
    lo = jnp.clip(_lookup(gstart, e_v) - tile_v * tr, 0, tr)
    hi = jnp.clip(_lookup(gend, e_v) - tile_v * tr, 0, tr)
    live = v < total
    lo = jnp.where(live, lo, 0).astype(I32)
    hi = jnp.where(live, hi, 0).astype(I32)
    changed = jnp.concatenate([jnp.ones((1,), I32), (e_v[1:] != e_v[:-1]).astype(I32)])
    run = _cumsum_small(changed) - 1
    later_other = (v[None, :] > v[:, None]) & (e_v[None, :] != e_v[:, None])
    nxt_pos = jnp.min(jnp.where(later_other, v[None, :], n_vis), axis=1)
    nxt_e = jnp.where(nxt_pos < n_vis, _lookup(e_v, jnp.minimum(nxt_pos, n_vis - 1)), -1).astype(I32)
    meta = jnp.stack([run[-1] + 1, e_v[0]]).astype(I32)
    return tile_v.astype(I32), e_v.astype(I32), lo, hi, run.astype(I32), nxt_e, meta


def _visit_state(vt, vlo, vhi):
    v = pl.program_id(1)
    lo = vlo[v]
    hi = vhi[v]
    first = jnp.logical_or(v == 0, vt[v] != vt[jnp.maximum(v - 1, 0)])
    return lo, hi, first


def _resident_weights(ve, run, nxt_e, meta, w_hbm, wbuf, wsem, n_chunks):
    c = pl.program_id(0)
    v = pl.program_id(1)
    tn = wbuf.shape[2]
    slot = lax.rem(c * meta[0] + run[v], 2)
    new_run = jnp.logical_or(v == 0, ve[v] != ve[jnp.maximum(v - 1, 0)])

    def fetch(e, chunk, s):
        for cc in range(n_chunks):
            @pl.when(chunk == cc)
            def _(cc=cc):
                pltpu.make_async_copy(w_hbm.at[e, :, pl.ds(cc * tn, tn)], wbuf.at[s], wsem.at[s]).start()

    @pl.when(jnp.logical_and(c == 0, v == 0))
    def _():
        fetch(ve[0], c, slot)

    @pl.when(new_run)
    def _():
        pltpu.make_async_copy(w_hbm.at[0, :, pl.ds(0, tn)], wbuf.at[slot], wsem.at[slot]).wait()
        more_here = nxt_e[v] >= 0

        @pl.when(more_here)
        def _():
            fetch(nxt_e[v], c, 1 - slot)

        @pl.when(jnp.logical_and(jnp.logical_not(more_here), c + 1 < n_chunks))
        def _():
            fetch(meta[1], c + 1, 1 - slot)

    return slot


def _store_rows(o_ref, val, lo, hi):
    rows = lax.broadcasted_iota(I32, (o_ref.shape[0], 1), 0)
    mine = (rows >= lo) & (rows < hi)
    o_ref[...] = jnp.where(mine, val, o_ref[...])


MOE_SUB = 256


def _for_live_sub_blocks(o_ref, lo, hi, first, body):
    @pl.when(jnp.logical_and(first, hi > lo))
    def _():
        o_ref[...] = jnp.zeros(o_ref.shape, o_ref.dtype)

    subs = []
    for sb in range(o_ref.shape[0] // MOE_SUB):
        r0 = sb * MOE_SUB
        lo_s = jnp.clip(lo - r0, 0, MOE_SUB)
        hi_s = jnp.clip(hi - r0, 0, MOE_SUB)
        subs.append((pl.ds(r0, MOE_SUB), lo_s, hi_s, hi_s > lo_s))
    all_live = functools.reduce(jnp.logical_and, [live for _, _, _, live in subs])

    @pl.when(all_live)
    def _():
        for rows, lo_s, hi_s, _ in subs:
            body(rows, lo_s, hi_s)

    for rows, lo_s, hi_s, live in subs:
        @pl.when(jnp.logical_and(live, jnp.logical_not(all_live)))
        def _(rows=rows, lo_s=lo_s, hi_s=hi_s):
            body(rows, lo_s, hi_s)


SEL_W = 512


def _even_lane_selector():
    r = jnp.arange(SEL_W)[:, None]
    c = jnp.arange(SEL_W // 2)[None, :]
    return (r == 2 * c).astype(BF16)


def _up_kernel(vt, ve, vlo, vhi, run, nxt_e, meta, xg_ref, w1_hbm, b1_ref, sel_ref, o_ref, wbuf, wsem,
               *, n_chunks):
    lo, hi, first = _visit_state(vt, vlo, vhi)
    slot = _resident_weights(ve, run, nxt_e, meta, w1_hbm, wbuf, wsem, n_chunks)
    half = wbuf.shape[1] // 2
    tn = wbuf.shape[2]

    def sub_block(rows, lo_s, hi_s):
        xa, xb = _unpack_halves(xg_ref[rows, :])
        h = (_dot(xa.astype(BF16), wbuf[slot, :half, :].astype(BF16))
             + _dot(xb.astype(BF16), wbuf[slot, half:, :].astype(BF16)) + b1_ref[...])
        glu = jnp.minimum(h, SWIGLU_LIMIT)
        lin = jnp.clip(h, -SWIGLU_LIMIT, SWIGLU_LIMIT) + 1.0
        gact = glu * jax.nn.sigmoid(SWIGLU_ALPHA * glu)
        parts = []
        for c in range(tn // LANES):
            cs = slice(c * LANES, (c + 1) * LANES)
            parts.append(gact[:, cs] * pltpu.roll(lin[:, cs], LANES - 1, 1))
        inter = jnp.concatenate(parts, axis=1).astype(BF16)
        acts = [_dot(inter[:, s * SEL_W:(s + 1) * SEL_W], sel_ref[...]) for s in range(tn // SEL_W)]
        act = jnp.concatenate(acts, axis=1).astype(o_ref.dtype)
        _store_rows(o_ref.at[rows], act, lo_s, hi_s)

    _for_live_sub_blocks(o_ref, lo, hi, first, sub_block)


def _expert_up(sched, xg, w1, b1, *, tr, tn):
    p_rows, w = xg.shape
    n_exp, d, f2 = w1.shape
    n_vis = sched[0].shape[0]
    n_chunks = f2 // tn
    grid_spec = pltpu.PrefetchScalarGridSpec(
        num_scalar_prefetch=len(sched),
        grid=(n_chunks, n_vis),
        in_specs=[
            pl.BlockSpec((tr, w), lambda c, v, vt, *_: (vt[v], 0)),
            pl.BlockSpec(memory_space=pl.ANY),
            pl.BlockSpec((None, 1, tn), lambda c, v, vt, ve, *_: (ve[v], 0, c)),
            pl.BlockSpec((SEL_W, SEL_W // 2), lambda c, v, *_: (0, 0)),
        ],
        out_specs=pl.BlockSpec((tr, tn // 2), lambda c, v, vt, *_: (vt[v], c)),
        scratch_shapes=[pltpu.VMEM((2, d, tn), F32), pltpu.SemaphoreType.DMA((2,))],
    )
    return pl.pallas_call(
        functools.partial(_up_kernel, n_chunks=n_chunks),
        out_shape=jax.ShapeDtypeStruct((p_rows, f2 // 2), BF16),
        grid_spec=grid_spec,
        compiler_params=_params("arbitrary", "arbitrary"),
        name="expert_up",
    )(*sched, xg, w1, b1.reshape(n_exp, 1, f2), _even_lane_selector())


def _down_kernel(vt, ve, vlo, vhi, run, nxt_e, meta, act_ref, w2_hbm, b2_ref, o_ref, wbuf, wsem, *, n_chunks):
    lo, hi, first = _visit_state(vt, vlo, vhi)
    slot = _resident_weights(ve, run, nxt_e, meta, w2_hbm, wbuf, wsem, n_chunks)

    def sub_block(rows, lo_s, hi_s):
        y = _dot(act_ref[rows, :], wbuf[slot].astype(BF16)) + b2_ref[...]
        _store_rows(o_ref.at[rows], _pack_halves(y), lo_s, hi_s)

    _for_live_sub_blocks(o_ref, lo, hi, first, sub_block)


def _expert_down(sched, act, w2, b2, *, tr, tn):
    p_rows, f = act.shape
    n_exp, _, d = w2.shape
    n_vis = sched[0].shape[0]
    n_chunks = d // tn
    grid_spec = pltpu.PrefetchScalarGridSpec(
        num_scalar_prefetch=len(sched),
        grid=(n_chunks, n_vis),
        in_specs=[
            pl.BlockSpec((tr, f), lambda c, v, vt, *_: (vt[v], 0)),
            pl.BlockSpec(memory_space=pl.ANY),
            pl.BlockSpec((None, 1, tn), lambda c, v, vt, ve, *_: (ve[v], 0, c)),
        ],
        out_specs=pl.BlockSpec((tr, tn // 2), lambda c, v, vt, *_: (vt[v], c)),
        scratch_shapes=[pltpu.VMEM((2, f, tn), F32), pltpu.SemaphoreType.DMA((2,))],
    )
    return pl.pallas_call(
        functools.partial(_down_kernel, n_chunks=n_chunks),
        out_shape=jax.ShapeDtypeStruct((p_rows, d // 2), U32),
        grid_spec=grid_spec,
        compiler_params=_params("arbitrary", "arbitrary"),
        name="expert_down",
    )(*sched, act, w2, b2.reshape(n_exp, 1, d))


def _combine_kernel(dest_ref, h_ref, gate_ref, g_ref, b_ref, y_ref, o_ref, buf_ref, sem,
                    *, alpha, chunk):
    tm = h_ref.shape[0]

    def issue(t, carry):
        for k in range(TOP_K):
            pltpu.make_async_copy(y_ref.at[pl.ds(dest_ref[0, t * TOP_K + k], 1)],
                                  buf_ref.at[k, pl.ds(t, 1)], sem).start()
        return carry

    lax.fori_loop(0, tm, issue, 0)

    def drain(t, carry):
        for k in range(TOP_K):
            pltpu.make_async_copy(y_ref.at[pl.ds(0, 1)], buf_ref.at[k, pl.ds(0, 1)], sem).wait()
        return carry

    lax.fori_loop(0, tm, drain, 0)

    gates = gate_ref[...]
    acc_hi = None
    acc_lo = None
    for k in range(TOP_K):
        hi, lo = _unpack_halves(buf_ref[k])
        gk = gates[:, k:k + 1]
        acc_hi = gk * hi if acc_hi is None else acc_hi + gk * hi
        acc_lo = gk * lo if acc_lo is None else acc_lo + gk * lo
    hw = chunk // 2
    pieces = []
    for c in range(acc_hi.shape[1] // hw):
        pieces.append(acc_hi[:, c * hw:(c + 1) * hw])
        pieces.append(acc_lo[:, c * hw:(c + 1) * hw])
    ffn = jnp.concatenate(pieces, axis=1)
    o_ref[...] = _layer_norm(alpha * h_ref[...] + ffn, g_ref[...], b_ref[...]).astype(o_ref.dtype)


def _combine(dest, h, gates, g, b, y, *, alpha, tm, chunk):
    n, d = h.shape
    dest3 = dest.reshape(n // tm, 1, tm * TOP_K)
    kern = functools.partial(_combine_kernel, alpha=alpha, chunk=chunk)
    return pl.pallas_call(
        kern,
        out_shape=jax.ShapeDtypeStruct((n, d), F32),
        grid=(n // tm,),
        in_specs=[pl.BlockSpec((None, 1, tm * TOP_K), lambda i: (i, 0, 0), memory_space=pltpu.SMEM),
                  pl.BlockSpec((tm, d), lambda i: (i, 0)),
                  pl.BlockSpec((tm, LANES), lambda i: (i, 0)),
                  pl.BlockSpec((1, d), lambda i: (0, 0)),
                  pl.BlockSpec((1, d), lambda i: (0, 0)),
                  pl.BlockSpec(memory_space=pl.ANY)],
        out_specs=pl.BlockSpec((tm, d), lambda i: (i, 0)),
        scratch_shapes=[pltpu.VMEM((TOP_K, tm, d // 2), U32), pltpu.SemaphoreType.DMA(())],
        compiler_params=_params("arbitrary"),
        name="combine_ln",
    )(dest3, h, gates, g, b, y)


def _tiles(n_tokens, d_model, d_expert):
    return dict(
        cast_tm=min(512, n_tokens),
        proj=dict(tm=min(1024, n_tokens), tn=512),
        memkv=dict(tm=512, tn=512),
        merge=dict(tm=min(512, n_tokens), tn=256),
        out=dict(tm=min(1024, n_tokens), tn=512),
        ln_tm=min(256, n_tokens),
        dispatch_tm=min(256, n_tokens),
        moe_tr=512,
        up_tn=min(1024, 2 * d_expert),
        down_tn=min(2048, d_model),
        combine_tm=min(128, n_tokens),
    )


def _layer(h, mem2, lw, *, batch, seq, mem_len, lambda_init, alpha):
    n, d = h.shape
    t = _tiles(n, d, lw["w_mlp2"].shape[1])
    a_width = A_HEADS * A_HEAD_DIM
    b_width = B_HEADS * 2 * B_HEAD_DIM
    col = dict(a_q=0, a_k=a_width, a_v=2 * a_width, b_q=3 * a_width, b_k=3 * a_width + b_width,
               b_v=3 * a_width + 2 * b_width, c_q=3 * a_width + 3 * b_width)

    xb = _cast_bf16(h, t["cast_tm"])
    proj = _matmul(xb, lw["w_in"], out_dtype=BF16, name="in_proj", **t["proj"])
    ckv = _matmul(mem2, lw["w_mem_kv"], out_dtype=BF16, name="mem_kv",
                  tm=min(t["memkv"]["tm"], mem2.shape[0]), tn=t["memkv"]["tn"])

    ya = _attention_a(proj, _band_bias_base(lw["rel_bias"]), batch=batch, seq=seq,
                      heads=A_HEADS, dh=A_HEAD_DIM, col_q=col["a_q"], col_k=col["a_k"], col_v=col["a_v"])
    cos, sin_signed = _rope_tables(seq, B_HEAD_DIM)
    lam_vecs = jnp.stack([lw["lambda_q1"], lw["lambda_k1"], lw["lambda_q2"], lw["lambda_k2"]]).astype(F32)
    yb = _attention_b(proj, cos, sin_signed, lam_vecs, lw["diff_norm_g"].reshape(1, -1),
                      batch=batch, seq=seq, heads=B_HEADS, dh=B_HEAD_DIM,
                      col_q=col["b_q"], col_k=col["b_k"], col_v=col["b_v"], lambda_init=lambda_init)
    yc = _attention_c(proj, ckv, batch=batch, seq=seq, mem_len=mem_len, heads=C_HEADS,
                      dh=C_HEAD_DIM, col_q=col["c_q"])

    merged = _gated_merge(xb, ya, yb, yc, lw["w_gates"], lw["b_gates"].reshape(1, -1),
                          lw["w_branch_a"], lw["w_branch_b"], lw["w_branch_c"], **t["merge"])
    mix = _matmul(merged, lw["w_o"], out_dtype=F32, name="out_proj", **t["out"])

    n_exp = lw["w_router"].shape[1]
    wr = jnp.pad(lw["w_router"], ((0, 0), (0, LANES - n_exp)))
    wr_hi = wr.astype(BF16)
    wr_lo = (wr - wr_hi.astype(F32)).astype(BF16)
    br = jnp.pad(lw["b_router"], (0, LANES - n_exp)).reshape(1, LANES)
    h1, h1_packed, top_idx, gates, rank, cnt = _ln_router(
        h, mix, lw["ln1_g"].reshape(1, -1), lw["ln1_b"].reshape(1, -1),
        jnp.concatenate([wr_hi, wr_lo], axis=1), br, alpha=alpha, n_exp=n_exp, tm=t["ln_tm"])

    counts = cnt[0, :n_exp].astype(I32)
    gstart = _cumsum_small(counts) - counts
    dest = _lookup(gstart, top_idx[:, :TOP_K]) + rank[:, :TOP_K]
    sched = _visit_schedule(counts, n * TOP_K, t["moe_tr"])

    xg = _dispatch(dest, h1_packed, tm=t["dispatch_tm"])
    act = _expert_up(sched, xg, lw["w_mlp1"], lw["b_mlp1"], tr=t["moe_tr"], tn=t["up_tn"])
    y = _expert_down(sched, act, lw["w_mlp2"], lw["b_mlp2"], tr=t["moe_tr"], tn=t["down_tn"])
    return _combine(dest, h1, gates, lw["ln2_g"].reshape(1, -1), lw["ln2_b"].reshape(1, -1), y,
                    alpha=alpha, tm=t["combine_tm"], chunk=t["down_tn"])


def kernel(x, mem, w_in, w_mem_kv, rel_bias, lambda_q1, lambda_k1, lambda_q2, lambda_k2, diff_norm_g,
           w_branch_a, w_branch_b, w_branch_c, w_gates, b_gates, w_o, ln1_g, ln1_b, w_router, b_router,
           w_mlp1, b_mlp1, w_mlp2, b_mlp2, ln2_g, ln2_b):
    batch, seq, d = x.shape
    mem_len = mem.shape[1]
    depth = w_in.shape[0]
    alpha = (2 * depth) ** 0.25
    stacked = dict(w_in=w_in, w_mem_kv=w_mem_kv, rel_bias=rel_bias, lambda_q1=lambda_q1,
                   lambda_k1=lambda_k1, lambda_q2=lambda_q2, lambda_k2=lambda_k2, diff_norm_g=diff_norm_g,
                   w_branch_a=w_branch_a, w_branch_b=w_branch_b, w_branch_c=w_branch_c, w_gates=w_gates,
                   b_gates=b_gates, w_o=w_o, ln1_g=ln1_g, ln1_b=ln1_b, w_router=w_router,
                   b_router=b_router, w_mlp1=w_mlp1, b_mlp1=b_mlp1, w_mlp2=w_mlp2, b_mlp2=b_mlp2,
                   ln2_g=ln2_g, ln2_b=ln2_b)
    h = x.reshape(batch * seq, d)
    mem2 = mem.reshape(batch * mem_len, d)
    for l in range(depth):
        lw = {name: w[l] for name, w in stacked.items()}
        lambda_init = 0.8 - 0.6 * math.exp(-0.3 * l)
        h = _layer(h, mem2, lw, batch=batch, seq=seq, mem_len=mem_len, lambda_init=lambda_init, alpha=alpha)
    return h.reshape(batch, seq, d)
```

```python
import functools
import math

import jax
import jax.numpy as jnp
from jax import lax
from jax.experimental import pallas as pl
from jax.experimental.pallas import tpu as pltpu

F32 = jnp.float32
BF16 = jnp.bfloat16
U32 = jnp.uint32
I32 = jnp.int32

CHUNK = 64
LEFT_CHUNKS = 8
MAX_REL = 128
A_HEADS = 16
A_HEAD_DIM = 128
B_HEADS = 4
B_HEAD_DIM = 128
C_HEADS = 4
C_HEAD_DIM = 256
N_BRANCHES = 3
ROPE_THETA = 10000.0
TOP_K = 4
SWIGLU_LIMIT = 7.0
SWIGLU_ALPHA = 1.702
LN_EPS = 1e-5
RMS_EPS = 1e-5
MASK_VALUE = -1e30

V7X_VMEM_BYTES = 64 * 1024 * 1024
V7X_VMEM_LIMIT = V7X_VMEM_BYTES - 8 * 1024 * 1024
LANES = 128

NT_DIMS = (((1,), (1,)), ((), ()))


def _params(*semantics):
    return pltpu.CompilerParams(dimension_semantics=semantics,
                                vmem_limit_bytes=V7X_VMEM_LIMIT)


def _dot(a, b):
    return jnp.dot(a, b, preferred_element_type=F32)


def _pack_halves(x):
    w = x.shape[1] // 2
    hi = lax.bitcast_convert_type(x[:, :w].astype(jnp.bfloat16).astype(F32), U32)
    lo = lax.bitcast_convert_type(x[:, w:].astype(jnp.bfloat16).astype(F32), U32)
    return hi | (lo >> 16)


def _unpack_halves(p):
    hi = lax.bitcast_convert_type(p & jnp.uint32(0xFFFF0000), F32)
    lo = lax.bitcast_convert_type(p << 16, F32)
    return hi, lo


def _cast_kernel(x_ref, o_ref):
    o_ref[...] = x_ref[...].astype(o_ref.dtype)


def _cast_bf16(x, tm):
    m, d = x.shape
    return pl.pallas_call(
        _cast_kernel,
        out_shape=jax.ShapeDtypeStruct((m, d), BF16),
        grid=(m // tm,),
        in_specs=[pl.BlockSpec((tm, d), lambda i: (i, 0))],
        out_specs=pl.BlockSpec((tm, d), lambda i: (i, 0)),
        compiler_params=_params("parallel"),
        name="cast_bf16",
    )(x)


def _mm_kernel(a_ref, w_ref, o_ref):
    a = a_ref[...].astype(BF16)
    o_ref[...] = _dot(a, w_ref[...].astype(BF16)).astype(o_ref.dtype)


def _matmul(a, w, *, tm, tn, out_dtype, name):
    m, k = a.shape
    n = w.shape[1]
    return pl.pallas_call(
        _mm_kernel,
        out_shape=jax.ShapeDtypeStruct((m, n), out_dtype),
        grid=(n // tn, m // tm),
        in_specs=[pl.BlockSpec((tm, k), lambda j, i: (i, 0)),
                  pl.BlockSpec((k, tn), lambda j, i: (0, j))],
        out_specs=pl.BlockSpec((tm, tn), lambda j, i: (i, j)),
        compiler_params=_params("parallel", "parallel"),
        name=name,
    )(a, w)


A_QCHUNKS = 4
A_TQ = A_QCHUNKS * CHUNK
A_WIN = (LEFT_CHUNKS + A_QCHUNKS) * CHUNK
A_VARIANTS = LEFT_CHUNKS * CHUNK // A_TQ + 1


A_BASE_W = A_WIN + A_TQ


def _band_bias_base(rel_bias):
    reach = A_WIN
    ext =jnp.pad(rel_bias.astype(F32), ((0, 0), (reach - MAX_REL, reach - MAX_REL)), mode="edge")
    rev = ext[:, ::-1]
    rows = []
    for v in range(A_VARIANTS):
        c = rev[:, reach - A_TQ * v - A_TQ: reach - A_TQ * v + A_WIN]
        rows.append(jnp.concatenate([c[:, A_TQ:], c[:, :A_TQ]], axis=1))
    return jnp.stack(rows)[:, :, None, :]


def _attn_a_kernel(q_ref, k_ref, v_ref, base_ref, o_ref, tb_ref, *, heads, dh, scale):
    i = pl.program_id(2)

    @pl.when(i == 0)
    def _():
        r = lax.broadcasted_iota(I32, (A_TQ, A_WIN), 0)
        j = lax.broadcasted_iota(I32, (A_TQ, A_WIN), 1)
        for var in range(A_VARIANTS):
            cdiff = (A_TQ * var + r) // CHUNK - j // CHUNK
            valid = (cdiff >= 0) & (cdiff <= LEFT_CHUNKS)
            for h in range(heads):
                rows = jnp.broadcast_to(base_ref[var, h], (A_TQ, A_BASE_W))
                toeplitz = pltpu.roll(rows, 0, 1, stride=1, stride_axis=0)[:, :A_WIN]
                tb_ref[var, h] = jnp.where(valid, toeplitz, MASK_VALUE)

    var = jnp.minimum(i, A_VARIANTS - 1)
    start = pl.multiple_of(jnp.maximum(i - (A_VARIANTS - 1), 0) * A_TQ, A_TQ)
    for h in range(heads):
        cs = slice(h * dh, (h + 1) * dh)
        q = q_ref[:, cs]
        k = k_ref[pl.ds(start, A_WIN), cs]
        v = v_ref[pl.ds(start, A_WIN), cs]
        s = lax.dot_general(q, k, NT_DIMS, preferred_element_type=F32) * scale + tb_ref[var, h]
        m = jnp.max(s, axis=-1, keepdims=True)
        p = jnp.exp(s - m)
        l = jnp.sum(p, axis=-1, keepdims=True)
        o = _dot(p.astype(BF16), v)
        o_ref[:, cs] = (o / l).astype(o_ref.dtype)


def _attention_a(proj, base, *, batch, seq, heads, dh, col_q, col_k, col_v, heads_per_step=8):
    n = proj.shape[0]
    gw = heads_per_step * dh
    n_groups = heads // heads_per_step
    n_qb = seq // A_TQ
    kern = functools.partial(_attn_a_kernel, heads=heads_per_step, dh=dh, scale=dh ** -0.5)
    return pl.pallas_call(
        kern,
        out_shape=jax.ShapeDtypeStruct((n, heads * dh), BF16),
        grid=(batch, n_groups, n_qb),
        in_specs=[
            pl.BlockSpec((A_TQ, gw), lambda b, g, i: (b * n_qb + i, col_q // gw + g)),
            pl.BlockSpec((seq, gw), lambda b, g, i: (b, col_k // gw + g)),
            pl.BlockSpec((seq, gw), lambda b, g, i: (b, col_v // gw + g)),
            pl.BlockSpec((A_VARIANTS, heads_per_step, 1, A_BASE_W), lambda b, g, i: (0, g, 0, 0)),
        ],
        out_specs=pl.BlockSpec((A_TQ, gw), lambda b, g, i: (b * n_qb + i, g)),
        scratch_shapes=[pltpu.VMEM((A_VARIANTS, heads_per_step, A_TQ, A_WIN), F32)],
        compiler_params=_params("parallel", "parallel", "arbitrary"),
        name="attn_band",
    )(proj, proj, proj, base)


B_TQ = 256


def _rope_tables(seq, dim):
    inv = 1.0 / (ROPE_THETA ** (jnp.arange(0, dim, 2, dtype=F32) / dim))
    ang = jnp.arange(seq, dtype=F32)[:, None] * inv[None, :]
    ang = jnp.concatenate([ang, ang], -1)
    sign = jnp.where(jnp.arange(dim) < dim // 2, -1.0, 1.0).astype(F32)
    return jnp.cos(ang), jnp.sin(ang) * sign[None, :]


def _rope(x, cos, sin_signed):
    return x * cos + pltpu.roll(x, x.shape[1] // 2, 1) * sin_signed


def _attn_b_kernel(q_ref, k_ref, v_ref, cos_ref, sin_ref, lam_ref, g_ref, o_ref, krot_ref,
                   *, dh, scale, lambda_init):
    qi = pl.program_id(2)
    seq = k_ref.shape[0]

    @pl.when(qi == 0)
    def _():
        for m in range(2):
            kf = k_ref[:, m * dh:(m + 1) * dh].astype(F32)
            krot_ref[m] = _rope(kf, cos_ref[...], sin_ref[...]).astype(BF16)

    lv = lam_ref[...]
    lam = (jnp.exp(jnp.sum(lv[0:1] * lv[1:2], axis=-1, keepdims=True))
           - jnp.exp(jnp.sum(lv[2:3] * lv[3:4], axis=-1, keepdims=True)) + lambda_init)

    def block(blk):
        row0 = blk * B_TQ
        kl = row0 + B_TQ
        cos_q = cos_ref[row0:kl, :]
        sin_q = sin_ref[row0:kl, :]
        q_chunk = (row0 + lax.broadcasted_iota(I32, (B_TQ, kl), 0)) // CHUNK
        k_chunk = lax.broadcasted_iota(I32, (B_TQ, kl), 1) // CHUNK
        allowed = k_chunk <= q_chunk
        probs = []
        for m in range(2):
            qf = q_ref[:, m * dh:(m + 1) * dh].astype(F32)
            qr = _rope(qf, cos_q, sin_q).astype(BF16)
            s = lax.dot_general(qr, krot_ref[m, :kl, :], NT_DIMS, preferred_element_type=F32) * scale
            s = jnp.where(allowed, s, MASK_VALUE)
            e = jnp.exp(s - jnp.max(s, axis=-1, keepdims=True))
            probs.append(e / jnp.sum(e, axis=-1, keepdims=True))
        w = (probs[0] - lam * probs[1]).astype(BF16)
        o = _dot(w, v_ref[:kl, :])
        ms = jnp.mean(o * o, axis=-1, keepdims=True)
        y = o * lax.rsqrt(ms + RMS_EPS) * g_ref[...] * (1.0 - lambda_init)
        o_ref[...] = y.astype(o_ref.dtype)

    for blk in range(seq // B_TQ):
        pl.when(qi == blk)(functools.partial(block, blk))


def _attention_b(proj, cos, sin_signed, lam_vecs, norm_g, *, batch, seq, heads, dh,
                 col_q, col_k, col_v, lambda_init):
    n = proj.shape[0]
    hw = 2 * dh
    n_qb = seq // B_TQ
    kern = functools.partial(_attn_b_kernel, dh=dh, scale=dh ** -0.5, lambda_init=lambda_init)
    return pl.pallas_call(
        kern,
        out_shape=jax.ShapeDtypeStruct((n, heads * hw), BF16),
        grid=(batch, heads, n_qb),
        in_specs=[
            pl.BlockSpec((B_TQ, hw), lambda b, h, i: (b * n_qb + i, col_q // hw + h)),
            pl.BlockSpec((seq, hw), lambda b, h, i: (b, col_k // hw + h)),
            pl.BlockSpec((seq, hw), lambda b, h, i: (b, col_v // hw + h)),
            pl.BlockSpec((seq, dh), lambda b, h, i: (0, 0)),
            pl.BlockSpec((seq, dh), lambda b, h, i: (0, 0)),
            pl.BlockSpec((4, dh), lambda b, h, i: (0, 0)),
            pl.BlockSpec((1, hw), lambda b, h, i: (0, 0)),
        ],
        out_specs=pl.BlockSpec((B_TQ, hw), lambda b, h, i: (b * n_qb + i, h)),
        scratch_shapes=[pltpu.VMEM((2, seq, dh), BF16)],
        compiler_params=_params("parallel", "parallel", "arbitrary"),
        name="attn_diff",
    )(proj, proj, proj, cos, sin_signed, lam_vecs, norm_g)


C_TQ = 512


def _attn_c_kernel(q_ref, k_ref, v_ref, o_ref, *, scale):
    s = lax.dot_general(q_ref[...], k_ref[...], NT_DIMS, preferred_element_type=F32) * scale
    e = jnp.exp(s - jnp.max(s, axis=-1, keepdims=True))
    p = (e / jnp.sum(e, axis=-1, keepdims=True)).astype(BF16)
    o_ref[...] = _dot(p, v_ref[...]).astype(o_ref.dtype)


def _attention_c(proj, ckv, *, batch, seq, mem_len, heads, dh, col_q):
    n = proj.shape[0]
    n_qb = seq // C_TQ
    kern = functools.partial(_attn_c_kernel, scale=dh ** -0.5)
    return pl.pallas_call(
        kern,
        out_shape=jax.ShapeDtypeStruct((n, heads * dh), BF16),
        grid=(batch, heads, n_qb),
        in_specs=[
            pl.BlockSpec((C_TQ, dh), lambda b, h, i: (b * n_qb + i, col_q // dh + h)),
            pl.BlockSpec((mem_len, dh), lambda b, h, i: (b, h)),
            pl.BlockSpec((mem_len, dh), lambda b, h, i: (b, heads + h)),
        ],
        out_specs=pl.BlockSpec((C_TQ, dh), lambda b, h, i: (b * n_qb + i, h)),
        compiler_params=_params("parallel", "parallel", "parallel"),
        name="attn_mem",
    )(proj, ckv, ckv)


def _merge_kernel(x_ref, ya_ref, yb_ref, yc_ref, wga_ref, wgb_ref, wgc_ref,
                  bga_ref, bgb_ref, bgc_ref, pa_ref, pb_ref, pc_ref, o_ref):
    x = x_ref[...]
    acc = None
    for wg, bg, y, p in ((wga_ref, bga_ref, ya_ref, pa_ref),
                         (wgb_ref, bgb_ref, yb_ref, pb_ref),
                         (wgc_ref, bgc_ref, yc_ref, pc_ref)):
        gate = jax.nn.sigmoid(_dot(x, wg[...].astype(BF16)) + bg[...])
        term = gate * _dot(y[...], p[...].astype(BF16))
        acc = term if acc is None else acc + term
    o_ref[...] = acc.astype(o_ref.dtype)


def _gated_merge(xb, ya, yb, yc, w_gates, b_gates, pa, pb, pc, *, tm, tn):
    n, d = xb.shape
    nj = d // tn
    row = lambda width: pl.BlockSpec((tm, width), lambda j, i: (i, 0))
    gate_w = lambda br: pl.BlockSpec((d, tn), lambda j, i, br=br: (0, br * nj + j))
    gate_b = lambda br: pl.BlockSpec((1, tn), lambda j, i, br=br: (0, br * nj + j))
    branch_w = lambda width: pl.BlockSpec((width, tn), lambda j, i: (0, j))
    return pl.pallas_call(
        _merge_kernel,
        out_shape=jax.ShapeDtypeStruct((n, d), BF16),
        grid=(nj, n // tm),
        in_specs=[row(d), row(ya.shape[1]), row(yb.shape[1]), row(yc.shape[1]),
                  gate_w(0), gate_w(1), gate_w(2), gate_b(0), gate_b(1), gate_b(2),
                  branch_w(pa.shape[0]), branch_w(pb.shape[0]), branch_w(pc.shape[0])],
        out_specs=pl.BlockSpec((tm, tn), lambda j, i: (i, j)),
        compiler_params=_params("parallel", "parallel"),
        name="gated_merge",
    )(xb, ya, yb, yc, w_gates, w_gates, w_gates, b_gates, b_gates, b_gates, pa, pb, pc)


def _layer_norm(z, g, b):
    mu = jnp.mean(z, axis=-1, keepdims=True)
    zc = z - mu
    var = jnp.mean(zc * zc, axis=-1, keepdims=True)
    return zc * lax.rsqrt(var + LN_EPS) * g + b


def _ln_router_kernel(x_ref, m_ref, g_ref, b_ref, wr_ref, br_ref,
                      h_ref, hp_ref, idx_ref, gate_ref, rank_ref, cnt_ref, carry_ref,
                      *, alpha, n_exp):
    @pl.when(pl.program_id(0) == 0)
    def _():
        carry_ref[...] = jnp.zeros_like(carry_ref)

    tm = x_ref.shape[0]
    h = _layer_norm(alpha * x_ref[...] + m_ref[...], g_ref[...], b_ref[...])
    h_ref[...] = h
    hp_ref[...] = _pack_halves(h)

    h_hi = h.astype(BF16)
    h_lo = (h - h_hi.astype(F32)).astype(BF16)
    w = wr_ref[...]
    r1 = _dot(h_hi, w)
    logits = r1[:, :LANES] + r1[:, LANES:] + _dot(h_lo, w[:, :LANES]) + br_ref[...]

    lane = lax.broadcasted_iota(I32, (tm, LANES), 1)
    lane_f = lane.astype(F32)
    cur = jnp.where(lane < n_exp, logits, -jnp.inf)
    vals, idxs = [], []
    for _ in range(TOP_K):
        mx = jnp.max(cur, axis=-1, keepdims=True)
        ix = jnp.min(jnp.where(cur == mx, lane_f, float(LANES)), axis=-1, keepdims=True).astype(I32)
        vals.append(mx)
        idxs.append(ix)
        cur = jnp.where(lane == ix, -jnp.inf, cur)
    exps = [jnp.exp(v - vals[0]) for v in vals]
    den = exps[0]
    for e in exps[1:]:
        den = den + e

    tri = (lax.broadcasted_iota(I32, (tm, tm), 0) > lax.broadcasted_iota(I32, (tm, tm), 1)).astype(BF16)
    carry = carry_ref[...]
    idx_out = jnp.zeros((tm, LANES), I32)
    gate_out = jnp.zeros((tm, LANES), F32)
    rank_out = jnp.zeros((tm, LANES), I32)
    for k in range(TOP_K):
        onehot = (lane == idxs[k]).astype(F32)
        before = _dot(tri, onehot.astype(BF16)) + carry
        rank = jnp.sum(onehot * before, axis=-1, keepdims=True)
        carry = carry + jnp.sum(onehot, axis=0, keepdims=True)
        idx_out = jnp.where(lane == k, idxs[k], idx_out)
        gate_out = jnp.where(lane == k, exps[k] / den, gate_out)
        rank_out = jnp.where(lane == k, rank.astype(I32), rank_out)
    carry_ref[...] = carry
    idx_ref[...] = idx_out
    gate_ref[...] = gate_out
    rank_ref[...] = rank_out
    cnt_ref[...] = carry


def _ln_router(x, m, g, b, wr_split, br_pad, *, alpha, n_exp, tm):
    n, d = x.shape
    row = pl.BlockSpec((tm, d), lambda i: (i, 0))
    vec = pl.BlockSpec((1, d), lambda i: (0, 0))
    small = pl.BlockSpec((tm, LANES), lambda i: (i, 0))
    kern = functools.partial(_ln_router_kernel, alpha=alpha, n_exp=n_exp)
    return pl.pallas_call(
        kern,
        out_shape=(jax.ShapeDtypeStruct((n, d), F32),
                   jax.ShapeDtypeStruct((n, d // 2), U32),
                   jax.ShapeDtypeStruct((n, LANES), I32),
                   jax.ShapeDtypeStruct((n, LANES), F32),
                   jax.ShapeDtypeStruct((n, LANES), I32),
                   jax.ShapeDtypeStruct((1, LANES), F32)),
        grid=(n // tm,),
        in_specs=[row, row, vec, vec,
                  pl.BlockSpec((d, 2 * LANES), lambda i: (0, 0)),
                  pl.BlockSpec((1, LANES), lambda i: (0, 0))],
        out_specs=(row, pl.BlockSpec((tm, d // 2), lambda i: (i, 0)), small, small, small,
                   pl.BlockSpec((1, LANES), lambda i: (0, 0))),
        scratch_shapes=[pltpu.VMEM((1, LANES), F32)],
        compiler_params=_params("arbitrary"),
        name="ln_router",
    )(x, m, g, b, wr_split, br_pad)


def _row_copy(src, src_row, dst, dst_row, sem):
    return pltpu.make_async_copy(src.at[pl.ds(src_row, 1)], dst.at[pl.ds(dst_row, 1)], sem)


def _dispatch_kernel(dest_ref, hp_ref, xg_ref, sem):
    tm = hp_ref.shape[0]

    def issue(t, carry):
        for k in range(TOP_K):
            _row_copy(hp_ref, t, xg_ref, dest_ref[0, t * TOP_K + k], sem).start()
        return carry

    lax.fori_loop(0, tm, issue, 0)

    def drain(t, carry):
        for k in range(TOP_K):
            _row_copy(hp_ref, 0, xg_ref, 0, sem).wait()
        return carry

    lax.fori_loop(0, tm, drain, 0)


def _dispatch(dest, hp, *, tm):
    n, w = hp.shape
    dest3 = dest.reshape(n // tm, 1, tm * TOP_K)
    return pl.pallas_call(
        _dispatch_kernel,
        out_shape=jax.ShapeDtypeStruct((n * TOP_K, w), U32),
        grid=(n // tm,),
        in_specs=[pl.BlockSpec((None, 1, tm * TOP_K), lambda i: (i, 0, 0), memory_space=pltpu.SMEM),
                  pl.BlockSpec((tm, w), lambda i: (i, 0))],
        out_specs=pl.BlockSpec(memory_space=pl.ANY),
        scratch_shapes=[pltpu.SemaphoreType.DMA(())],
        compiler_params=_params("arbitrary"),
        name="dispatch",
    )(dest3, hp)


def _cumsum_small(x):
    n = x.shape[0]
    keep = jnp.arange(n)[:, None] >= jnp.arange(n)[None, :]
    return jnp.sum(jnp.where(keep, x[None, :], 0), axis=1).astype(x.dtype)


def _lookup(table, idx):
    hit = idx[..., None] == jnp.arange(table.shape[0], dtype=idx.dtype)
    return jnp.sum(jnp.where(hit, table, 0), axis=-1).astype(table.dtype)


def _visit_schedule(counts, n_rows, tr):
    n_exp = counts.shape[0]
    n_tiles = n_rows // tr
    n_vis = n_tiles + n_exp
    gend = _cumsum_small(counts)
    gstart = gend - counts
    first_tile = gstart // tr
    last_tile = jnp.maximum(gend - 1, 0) // tr
    nvis = jnp.where(counts > 0, last_tile - first_tile + 1, 0)
    vend = _cumsum_small(nvis)
    vstart = vend - nvis
    total = vend[-1]
    v = jnp.arange(n_vis, dtype=I32)
    vc = jnp.minimum(v, total - 1)
    e_v = jnp.minimum(jnp.sum((vend[None, :] <= vc[:, None]).astype(I32), axis=1), n_exp - 1)
    tile_v = _lookup(first_tile, e_v) + (vc - _lookup(vstart, e_v))
    lo = jnp.clip(_lookup(gstart, e_v) - tile_v * tr, 0, tr)
    hi = jnp.clip(_lookup(gend, e_v) - tile_v * tr, 0, tr)
    live = v < total
    lo = jnp.where(live, lo, 0).astype(I32)
    hi = jnp.where(live, hi, 0).astype(I32)
    changed = jnp.concatenate([jnp.ones((1,), I32), (e_v[1:] != e_v[:-1]).astype(I32)])
    run = _cumsum_small(changed) - 1
    later_other = (v[None, :] > v[:, None]) & (e_v[None, :] != e_v[:, None])
    nxt_pos = jnp.min(jnp.where(later_other, v[None, :], n_vis), axis=1)
    nxt_e = jnp.where(nxt_pos < n_vis, _lookup(e_v, jnp.minimum(nxt_pos, n_vis - 1)), -1).astype(I32)
    meta = jnp.stack([run[-1] + 1, e_v[0]]).astype(I32)
    return tile_v.astype(I32), e_v.astype(I32), lo, hi, run.astype(I32), nxt_e, meta


def _visit_state(vt, vlo, vhi):
    v = pl.program_id(1)
    lo = vlo[v]
    hi = vhi[v]
    first = jnp.logical_or(v == 0, vt[v] != vt[jnp.maximum(v - 1, 0)])
    return lo, hi, first


def _resident_weights(ve, run, nxt_e, meta, w_hbm, wbuf, wsem, n_chunks):
    c = pl.program_id(0)
    v = pl.program_id(1)
    tn = wbuf.shape[2]
    slot = lax.rem(c * meta[0] + run[v], 2)
    new_run = jnp.logical_or(v == 0, ve[v] != ve[jnp.maximum(v - 1, 0)])

    def fetch(e, chunk, s):
        for cc in range(n_chunks):
            @pl.when(chunk == cc)
            def _(cc=cc):
                pltpu.make_async_copy(w_hbm.at[e, :, pl.ds(cc * tn, tn)], wbuf.at[s], wsem.at[s]).start()

    @pl.when(jnp.logical_and(c == 0, v == 0))
    def _():
        fetch(ve[0], c, slot)

    @pl.when(new_run)
    def _():
        pltpu.make_async_copy(w_hbm.at[0, :, pl.ds(0, tn)], wbuf.at[slot], wsem.at[slot]).wait()
        more_here = nxt_e[v] >= 0

        @pl.when(more_here)
        def _():
            fetch(nxt_e[v], c, 1 - slot)

        @pl.when(jnp.logical_and(jnp.logical_not(more_here), c + 1 < n_chunks))
        def _():
            fetch(meta[1], c + 1, 1 - slot)

    return slot


def _store_rows(o_ref, val, lo, hi):
    rows = lax.broadcasted_iota(I32, (o_ref.shape[0], 1), 0)
    mine = (rows >= lo) & (rows < hi)
    o_ref[...] = jnp.where(mine, val, o_ref[...])


MOE_SUB = 256


def _for_live_sub_blocks(o_ref, lo, hi, first, body):
    @pl.when(jnp.logical_and(first, hi > lo))
    def _():
        o_ref[...] = jnp.zeros(o_ref.shape, o_ref.dtype)

    subs = []
    for sb in range(o_ref.shape[0] // MOE_SUB):
        r0 = sb * MOE_SUB
        lo_s = jnp.clip(lo - r0, 0, MOE_SUB)
        hi_s = jnp.clip(hi - r0, 0, MOE_SUB)
        subs.append((pl.ds(r0, MOE_SUB), lo_s, hi_s, hi_s > lo_s))
    all_live = functools.reduce(jnp.logical_and, [live for _, _, _, live in subs])

    @pl.when(all_live)
    def _():
        for rows, lo_s, hi_s, _ in subs:
            body(rows, lo_s, hi_s)

    for rows, lo_s, hi_s, live in subs:
        @pl.when(jnp.logical_and(live, jnp.logical_not(all_live)))
        def _(rows=rows, lo_s=lo_s, hi_s=hi_s):
            body(rows, lo_s, hi_s)


SEL_W = 512


def _even_lane_selector():
    r = jnp.arange(SEL_W)[:, None]
    c = jnp.arange(SEL_W // 2)[None, :]
    return (r == 2 * c).astype(BF16)


def _up_kernel(vt, ve, vlo, vhi, run, nxt_e, meta, xg_ref, w1_hbm, b1_ref, sel_ref, o_ref, wbuf, wsem,
               *, n_chunks):
    lo, hi, first = _visit_state(vt, vlo, vhi)
    slot = _resident_weights(ve, run, nxt_e, meta, w1_hbm, wbuf, wsem, n_chunks)
    half = wbuf.shape[1] // 2
    tn = wbuf.shape[2]

    def sub_block(rows, lo_s, hi_s):
        xa, xb = _unpack_halves(xg_ref[rows, :])
        h = (_dot(xa.astype(BF16), wbuf[slot, :half, :].astype(BF16))
             + _dot(xb.astype(BF16), wbuf[slot, half:, :].astype(BF16)) + b1_ref[...])
        glu = jnp.minimum(h, SWIGLU_LIMIT)
        lin = jnp.clip(h, -SWIGLU_LIMIT, SWIGLU_LIMIT) + 1.0
        gact = glu * jax.nn.sigmoid(SWIGLU_ALPHA * glu)
        parts = []
        for c in range(tn // LANES):
            cs = slice(c * LANES, (c + 1) * LANES)
            parts.append(gact[:, cs] * pltpu.roll(lin[:, cs], LANES - 1, 1))
        inter = jnp.concatenate(parts, axis=1).astype(BF16)
        acts = [_dot(inter[:, s * SEL_W:(s + 1) * SEL_W], sel_ref[...]) for s in range(tn // SEL_W)]
        act = jnp.concatenate(acts, axis=1).astype(o_ref.dtype)
        _store_rows(o_ref.at[rows], act, lo_s, hi_s)

    _for_live_sub_blocks(o_ref, lo, hi, first, sub_block)


def _expert_up(sched, xg, w1, b1, *, tr, tn):
    p_rows, w = xg.shape
    n_exp, d, f2 = w1.shape
    n_vis = sched[0].shape[0]
    n_chunks = f2 // tn
    grid_spec = pltpu.PrefetchScalarGridSpec(
        num_scalar_prefetch=len(sched),
        grid=(n_chunks, n_vis),
        in_specs=[
            pl.BlockSpec((tr, w), lambda c, v, vt, *_: (vt[v], 0)),
            pl.BlockSpec(memory_space=pl.ANY),
            pl.BlockSpec((None, 1, tn), lambda c, v, vt, ve, *_: (ve[v], 0, c)),
            pl.BlockSpec((SEL_W, SEL_W // 2), lambda c, v, *_: (0, 0)),
        ],
        out_specs=pl.BlockSpec((tr, tn // 2), lambda c, v, vt, *_: (vt[v], c)),
        scratch_shapes=[pltpu.VMEM((2, d, tn), F32), pltpu.SemaphoreType.DMA((2,))],
    )
    return pl.pallas_call(
        functools.partial(_up_kernel, n_chunks=n_chunks),
        out_shape=jax.ShapeDtypeStruct((p_rows, f2 // 2), BF16),
        grid_spec=grid_spec,
        compiler_params=_params("arbitrary", "arbitrary"),
        name="expert_up",
    )(*sched, xg, w1, b1.reshape(n_exp, 1, f2), _even_lane_selector())


def _down_kernel(vt, ve, vlo, vhi, run, nxt_e, meta, act_ref, w2_hbm, b2_ref, o_ref, wbuf, wsem, *, n_chunks):
    lo, hi, first = _visit_state(vt, vlo, vhi)
    slot = _resident_weights(ve, run, nxt_e, meta, w2_hbm, wbuf, wsem, n_chunks)

    def sub_block(rows, lo_s, hi_s):
        y = _dot(act_ref[rows, :], wbuf[slot].astype(BF16)) + b2_ref[...]
        _store_rows(o_ref.at[rows], _pack_halves(y), lo_s, hi_s)

    _for_live_sub_blocks(o_ref, lo, hi, first, sub_block)


def _expert_down(sched, act, w2, b2, *, tr, tn):
    p_rows, f = act.shape
    n_exp, _, d = w2.shape
    n_vis = sched[0].shape[0]
    n_chunks = d // tn
    grid_spec = pltpu.PrefetchScalarGridSpec(
        num_scalar_prefetch=len(sched),
        grid=(n_chunks, n_vis),
        in_specs=[
            pl.BlockSpec((tr, f), lambda c, v, vt, *_: (vt[v], 0)),
            pl.BlockSpec(memory_space=pl.ANY),
            pl.BlockSpec((None, 1, tn), lambda c, v, vt, ve, *_: (ve[v], 0, c)),
        ],
        out_specs=pl.BlockSpec((tr, tn // 2), lambda c, v, vt, *_: (vt[v], c)),
        scratch_shapes=[pltpu.VMEM((2, f, tn), F32), pltpu.SemaphoreType.DMA((2,))],
    )
    return pl.pallas_call(
        functools.partial(_down_kernel, n_chunks=n_chunks),
        out_shape=jax.ShapeDtypeStruct((p_rows, d // 2), U32),
        grid_spec=grid_spec,
        compiler_params=_params("arbitrary", "arbitrary"),
        name="expert_down",
    )(*sched, act, w2, b2.reshape(n_exp, 1, d))


def _combine_kernel(dest_ref, h_ref, gate_ref, g_ref, b_ref, y_ref, o_ref, buf_ref, sem,
                    *, alpha, chunk):
    tm = h_ref.shape[0]

    def issue(t, carry):
        for k in range(TOP_K):
            pltpu.make_async_copy(y_ref.at[pl.ds(dest_ref[0, t * TOP_K + k], 1)],
                                  buf_ref.at[k, pl.ds(t, 1)], sem).start()
        return carry

    lax.fori_loop(0, tm, issue, 0)

    def drain(t, carry):
        for k in range(TOP_K):
            pltpu.make_async_copy(y_ref.at[pl.ds(0, 1)], buf_ref.at[k, pl.ds(0, 1)], sem).wait()
        return carry

    lax.fori_loop(0, tm, drain, 0)

    gates = gate_ref[...]
    acc_hi = None
    acc_lo = None
    for k in range(TOP_K):
        hi, lo = _unpack_halves(buf_ref[k])
        gk = gates[:, k:k + 1]
        acc_hi = gk * hi if acc_hi is None else acc_hi + gk * hi
        acc_lo = gk * lo if acc_lo is None else acc_lo + gk * lo
    hw = chunk // 2
    pieces = []
    for c in range(acc_hi.shape[1] // hw):
        pieces.append(acc_hi[:, c * hw:(c + 1) * hw])
        pieces.append(acc_lo[:, c * hw:(c + 1) * hw])
    ffn = jnp.concatenate(pieces, axis=1)
    o_ref[...] = _layer_norm(alpha * h_ref[...] + ffn, g_ref[...], b_ref[...]).astype(o_ref.dtype)


def _combine(dest, h, gates, g, b, y, *, alpha, tm, chunk):
    n, d = h.shape
    dest3 = dest.reshape(n // tm, 1, tm * TOP_K)
    kern = functools.partial(_combine_kernel, alpha=alpha, chunk=chunk)
    return pl.pallas_call(
        kern,
        out_shape=jax.ShapeDtypeStruct((n, d), F32),
        grid=(n // tm,),
        in_specs=[pl.BlockSpec((None, 1, tm * TOP_K), lambda i: (i, 0, 0), memory_space=pltpu.SMEM),
                  pl.BlockSpec((tm, d), lambda i: (i, 0)),
                  pl.BlockSpec((tm, LANES), lambda i: (i, 0)),
                  pl.BlockSpec((1, d), lambda i: (0, 0)),
                  pl.BlockSpec((1, d), lambda i: (0, 0)),
                  pl.BlockSpec(memory_space=pl.ANY)],
        out_specs=pl.BlockSpec((tm, d), lambda i: (i, 0)),
        scratch_shapes=[pltpu.VMEM((TOP_K, tm, d // 2), U32), pltpu.SemaphoreType.DMA(())],
        compiler_params=_params("arbitrary"),
        name="combine_ln",
    )(dest3, h, gates, g, b, y)


def _tiles(n_tokens, d_model, d_expert):
    return dict(
        cast_tm=min(512, n_tokens),
        proj=dict(tm=min(1024, n_tokens), tn=512),
        memkv=dict(tm=512, tn=512),
        merge=dict(tm=min(512, n_tokens), tn=256),
        out=dict(tm=min(1024, n_tokens), tn=512),
        ln_tm=min(256, n_tokens),
        dispatch_tm=min(256, n_tokens),
        moe_tr=512,
        up_tn=min(1024, 2 * d_expert),
        down_tn=min(2048, d_model),
        combine_tm=min(128, n_tokens),
    )


def _layer(h, mem2, lw, *, batch, seq, mem_len, lambda_init, alpha):
    n, d = h.shape
    t = _tiles(n, d, lw["w_mlp2"].shape[1])
    a_width = A_HEADS * A_HEAD_DIM
    b_width = B_HEADS * 2 * B_HEAD_DIM
    col = dict(a_q=0, a_k=a_width, a_v=2 * a_width, b_q=3 * a_width, b_k=3 * a_width + b_width,
               b_v=3 * a_width + 2 * b_width, c_q=3 * a_width + 3 * b_width)

    xb = _cast_bf16(h, t["cast_tm"])
    proj = _matmul(xb, lw["w_in"], out_dtype=BF16, name="in_proj", **t["proj"])
    ckv = _matmul(mem2, lw["w_mem_kv"], out_dtype=BF16, name="mem_kv",
                  tm=min(t["memkv"]["tm"], mem2.shape[0]), tn=t["memkv"]["tn"])

    ya = _attention_a(proj, _band_bias_base(lw["rel_bias"]), batch=batch, seq=seq,
                      heads=A_HEADS, dh=A_HEAD_DIM, col_q=col["a_q"], col_k=col["a_k"], col_v=col["a_v"])
    cos, sin_signed = _rope_tables(seq, B_HEAD_DIM)
    lam_vecs = jnp.stack([lw["lambda_q1"], lw["lambda_k1"], lw["lambda_q2"], lw["lambda_k2"]]).astype(F32)
    yb = _attention_b(proj, cos, sin_signed, lam_vecs, lw["diff_norm_g"].reshape(1, -1),
                      batch=batch, seq=seq, heads=B_HEADS, dh=B_HEAD_DIM,
                      col_q=col["b_q"], col_k=col["b_k"], col_v=col["b_v"], lambda_init=lambda_init)
    yc = _attention_c(proj, ckv, batch=batch, seq=seq, mem_len=mem_len, heads=C_HEADS,
                      dh=C_HEAD_DIM, col_q=col["c_q"])

    merged = _gated_merge(xb, ya, yb, yc, lw["w_gates"], lw["b_gates"].reshape(1, -1),
                          lw["w_branch_a"], lw["w_branch_b"], lw["w_branch_c"], **t["merge"])
    mix = _matmul(merged, lw["w_o"], out_dtype=F32, name="out_proj", **t["out"])

    n_exp = lw["w_router"].shape[1]
    wr = jnp.pad(lw["w_router"], ((0, 0), (0, LANES - n_exp)))
    wr_hi = wr.astype(BF16)
    wr_lo = (wr - wr_hi.astype(F32)).astype(BF16)
    br = jnp.pad(lw["b_router"], (0, LANES - n_exp)).reshape(1, LANES)
    h1, h1_packed, top_idx, gates, rank, cnt = _ln_router(
        h, mix, lw["ln1_g"].reshape(1, -1), lw["ln1_b"].reshape(1, -1),
        jnp.concatenate([wr_hi, wr_lo], axis=1), br, alpha=alpha, n_exp=n_exp, tm=t["ln_tm"])

    counts = cnt[0, :n_exp].astype(I32)
    gstart = _cumsum_small(counts) - counts
    dest = _lookup(gstart, top_idx[:, :TOP_K]) + rank[:, :TOP_K]
    sched = _visit_schedule(counts, n * TOP_K, t["moe_tr"])

    xg = _dispatch(dest, h1_packed, tm=t["dispatch_tm"])
    act = _expert_up(sched, xg, lw["w_mlp1"], lw["b_mlp1"], tr=t["moe_tr"], tn=t["up_tn"])
    y = _expert_down(sched, act, lw["w_mlp2"], lw["b_mlp2"], tr=t["moe_tr"], tn=t["down_tn"])
    return _combine(dest, h1, gates, lw["ln2_g"].reshape(1, -1), lw["ln2_b"].reshape(1, -1), y,
                    alpha=alpha, tm=t["combine_tm"], chunk=t["down_tn"])


def kernel(x, mem, w_in, w_mem_kv, rel_bias, lambda_q1, lambda_k1, lambda_q2, lambda_k2, diff_norm_g,
           w_branch_a, w_branch_b, w_branch_c, w_gates, b_gates, w_o, ln1_g, ln1_b, w_router, b_router,
           w_mlp1, b_mlp1, w_mlp2, b_mlp2, ln2_g, ln2_b):
    batch, seq, d = x.shape
    mem_len = mem.shape[1]
    depth = w_in.shape[0]
    alpha = (2 * depth) ** 0.25
    stacked = dict(w_in=w_in, w_mem_kv=w_mem_kv, rel_bias=rel_bias, lambda_q1=lambda_q1,
                   lambda_k1=lambda_k1, lambda_q2=lambda_q2, lambda_k2=lambda_k2, diff_norm_g=diff_norm_g,
                   w_branch_a=w_branch_a, w_branch_b=w_branch_b, w_branch_c=w_branch_c, w_gates=w_gates,
                   b_gates=b_gates, w_o=w_o, ln1_g=ln1_g, ln1_b=ln1_b, w_router=w_router,
                   b_router=b_router, w_mlp1=w_mlp1, b_mlp1=b_mlp1, w_mlp2=w_mlp2, b_mlp2=b_mlp2,
                   ln2_g=ln2_g, ln2_b=ln2_b)
    h = x.reshape(batch * seq, d)
    mem2 = mem.reshape(batch * mem_len, d)
    for l in range(depth):
        lw = {name: w[l] for name, w in stacked.items()}
        lambda_init = 0.8 - 0.6 * math.exp(-0.3 * l)
        h = _layer(h, mem2, lw, batch=batch, seq=seq, mem_len=mem_len, lambda_init=lambda_init, alpha=alpha)
    return h.reshape(batch, seq, d)
```

```python
import functools
import math

import jax
import jax.numpy as jnp
from jax import lax
from jax.experimental import pallas as pl
from jax.experimental.pallas import tpu as pltpu

F32 = jnp.float32
BF16 = jnp.bfloat16
U32 = jnp.uint32
I32 = jnp.int32

CHUNK = 64
LEFT_CHUNKS = 8
MAX_REL = 128
A_HEADS = 16
A_HEAD_DIM = 128
B_HEADS = 4
B_HEAD_DIM = 128
C_HEADS = 4
C_HEAD_DIM = 256
N_BRANCHES = 3
ROPE_THETA = 10000.0
TOP_K = 4
SWIGLU_LIMIT = 7.0
SWIGLU_ALPHA = 1.702
LN_EPS = 1e-5
RMS_EPS = 1e-5
MASK_VALUE = -1e30

V7X_VMEM_BYTES = 64 * 1024 * 1024
V7X_VMEM_LIMIT = V7X_VMEM_BYTES - 8 * 1024 * 1024
LANES = 128

NT_DIMS = (((1,), (1,)), ((), ()))


def _params(*semantics):
    return pltpu.CompilerParams(dimension_semantics=semantics,
                                vmem_limit_bytes=V7X_VMEM_LIMIT)


def _dot(a, b):
    return jnp.dot(a, b, preferred_element_type=F32)


def _pack_halves(x):
    w = x.shape[1] // 2
    hi = lax.bitcast_convert_type(x[:, :w].astype(jnp.bfloat16).astype(F32), U32)
    lo = lax.bitcast_convert_type(x[:, w:].astype(jnp.bfloat16).astype(F32), U32)
    return hi | (lo >> 16)


def _unpack_halves(p):
    hi = lax.bitcast_convert_type(p & jnp.uint32(0xFFFF0000), F32)
    lo = lax.bitcast_convert_type(p << 16, F32)
    return hi, lo


def _cast_kernel(x_ref, o_ref):
    o_ref[...] = x_ref[...].astype(o_ref.dtype)


def _cast_bf16(x, tm):
    m, d = x.shape
    return pl.pallas_call(
        _cast_kernel,
        out_shape=jax.ShapeDtypeStruct((m, d), BF16),
        grid=(m // tm,),
        in_specs=[pl.BlockSpec((tm, d), lambda i: (i, 0))],
        out_specs=pl.BlockSpec((tm, d), lambda i: (i, 0)),
        compiler_params=_params("parallel"),
        name="cast_bf16",
    )(x)


def _mm_kernel(a_ref, w_ref, o_ref):
    a = a_ref[...].astype(BF16)
    o_ref[...] = _dot(a, w_ref[...].astype(BF16)).astype(o_ref.dtype)


def _matmul(a, w, *, tm, tn, out_dtype, name):
    m, k = a.shape
    n = w.shape[1]
    return pl.pallas_call(
        _mm_kernel,
        out_shape=jax.ShapeDtypeStruct((m, n), out_dtype),
        grid=(n // tn, m // tm),
        in_specs=[pl.BlockSpec((tm, k), lambda j, i: (i, 0)),
                  pl.BlockSpec((k, tn), lambda j, i: (0, j))],
        out_specs=pl.BlockSpec((tm, tn), lambda j, i: (i, j)),
        compiler_params=_params("parallel", "parallel"),
        name=name,
    )(a, w)


A_QCHUNKS = 4
A_TQ = A_QCHUNKS * CHUNK
A_WIN = (LEFT_CHUNKS + A_QCHUNKS) * CHUNK
A_VARIANTS = LEFT_CHUNKS * CHUNK // A_TQ + 1


A_BASE_W = A_WIN + A_TQ


def _band_bias_base(rel_bias):
    reach = A_WIN
    ext =jnp.pad(rel_bias.astype(F32), ((0, 0), (reach - MAX_REL, reach - MAX_REL)), mode="edge")
    rev = ext[:, ::-1]
    rows = []
    for v in range(A_VARIANTS):
        c = rev[:, reach - A_TQ * v - A_TQ: reach - A_TQ * v + A_WIN]
        rows.append(jnp.concatenate([c[:, A_TQ:], c[:, :A_TQ]], axis=1))
    return jnp.stack(rows)[:, :, None, :]


def _attn_a_kernel(q_ref, k_ref, v_ref, base_ref, o_ref, tb_ref, *, heads, dh, scale):
    i = pl.program_id(2)

    @pl.when(i == 0)
    def _():
        r = lax.broadcasted_iota(I32, (A_TQ, A_WIN), 0)
        j = lax.broadcasted_iota(I32, (A_TQ, A_WIN), 1)
        for var in range(A_VARIANTS):
            cdiff = (A_TQ * var + r) // CHUNK - j // CHUNK
            valid = (cdiff >= 0) & (cdiff <= LEFT_CHUNKS)
            for h in range(heads):
                rows = jnp.broadcast_to(base_ref[var, h], (A_TQ, A_BASE_W))
                toeplitz = pltpu.roll(rows, 0, 1, stride=1, stride_axis=0)[:, :A_WIN]
                tb_ref[var, h] = jnp.where(valid, toeplitz, MASK_VALUE)

    var = jnp.minimum(i, A_VARIANTS - 1)
    start = pl.multiple_of(jnp.maximum(i - (A_VARIANTS - 1), 0) * A_TQ, A_TQ)
    for h in range(heads):
        cs = slice(h * dh, (h + 1) * dh)
        q = q_ref[:, cs]
        k = k_ref[pl.ds(start, A_WIN), cs]
        v = v_ref[pl.ds(start, A_WIN), cs]
        s = lax.dot_general(q, k, NT_DIMS, preferred_element_type=F32) * scale + tb_ref[var, h]
        m = jnp.max(s, axis=-1, keepdims=True)
        p = jnp.exp(s - m)
        l = jnp.sum(p, axis=-1, keepdims=True)
        o = _dot(p.astype(BF16), v)
        o_ref[:, cs] = (o / l).astype(o_ref.dtype)


def _attention_a(proj, base, *, batch, seq, heads, dh, col_q, col_k, col_v, heads_per_step=8):
    n = proj.shape[0]
    gw = heads_per_step * dh
    n_groups = heads // heads_per_step
    n_qb = seq // A_TQ
    kern = functools.partial(_attn_a_kernel, heads=heads_per_step, dh=dh, scale=dh ** -0.5)
    return pl.pallas_call(
        kern,
        out_shape=jax.ShapeDtypeStruct((n, heads * dh), BF16),
        grid=(batch, n_groups, n_qb),
        in_specs=[
            pl.BlockSpec((A_TQ, gw), lambda b, g, i: (b * n_qb + i, col_q // gw + g)),
            pl.BlockSpec((seq, gw), lambda b, g, i: (b, col_k // gw + g)),
            pl.BlockSpec((seq, gw), lambda b, g, i: (b, col_v // gw + g)),
            pl.BlockSpec((A_VARIANTS, heads_per_step, 1, A_BASE_W), lambda b, g, i: (0, g, 0, 0)),
        ],
        out_specs=pl.BlockSpec((A_TQ, gw), lambda b, g, i: (b * n_qb + i, g)),
        scratch_shapes=[pltpu.VMEM((A_VARIANTS, heads_per_step, A_TQ, A_WIN), F32)],
        compiler_params=_params("parallel", "parallel", "arbitrary"),
        name="attn_band",
    )(proj, proj, proj, base)


B_TQ = 256


def _rope_tables(seq, dim):
    inv = 1.0 / (ROPE_THETA ** (jnp.arange(0, dim, 2, dtype=F32) / dim))
    ang = jnp.arange(seq, dtype=F32)[:, None] * inv[None, :]
    ang = jnp.concatenate([ang, ang], -1)
    sign = jnp.where(jnp.arange(dim) < dim // 2, -1.0, 1.0).astype(F32)
    return jnp.cos(ang), jnp.sin(ang) * sign[None, :]


def _rope(x, cos, sin_signed):
    return x * cos + pltpu.roll(x, x.shape[1] // 2, 1) * sin_signed


def _attn_b_kernel(q_ref, k_ref, v_ref, cos_ref, sin_ref, lam_ref, g_ref, o_ref, krot_ref,
                   *, heads, dh, scale, lambda_init):
    qi = pl.program_id(2)
    seq = k_ref.shape[0]
    hw = 2 * dh

    @pl.when(qi == 0)
    def _():
        for hm in range(2 * heads):
            kf = k_ref[:, hm * dh:(hm + 1) * dh].astype(F32)
            krot_ref[hm] = _rope(kf, cos_ref[...], sin_ref[...]).astype(BF16)

    lv = lam_ref[...]
    lam = (jnp.exp(jnp.sum(lv[0:1] * lv[1:2], axis=-1, keepdims=True))
           - jnp.exp(jnp.sum(lv[2:3] * lv[3:4], axis=-1, keepdims=True)) + lambda_init)

    def block(blk):
        row0 = blk * B_TQ
        kl = row0 + B_TQ
        cos_q = cos_ref[row0:kl, :]
        sin_q = sin_ref[row0:kl, :]
        q_chunk = lax.broadcasted_iota(I32, (B_TQ, B_TQ), 0) // CHUNK
        k_chunk = lax.broadcasted_iota(I32, (B_TQ, B_TQ), 1) // CHUNK
        allowed = k_chunk <= q_chunk
        for hh in range(heads):
            probs = []
            for m in range(2):
                hm = 2 * hh + m
                qf = q_ref[:, hm * dh:(hm + 1) * dh].astype(F32)
                qr = (_rope(qf, cos_q, sin_q) * scale).astype(BF16)
                s = lax.dot_general(qr, krot_ref[hm, :kl, :], NT_DIMS, preferred_element_type=F32)
                diag = jnp.where(allowed, s[:, row0:], MASK_VALUE)
                s = diag if row0 == 0 else jnp.concatenate([s[:, :row0], diag], axis=1)
                e = jnp.exp(s - jnp.max(s, axis=-1, keepdims=True))
                probs.append(e / jnp.sum(e, axis=-1, keepdims=True))
            w = (probs[0] - lam * probs[1]).astype(BF16)
            o = _dot(w, v_ref[:kl, hh * hw:(hh + 1) * hw])
            ms = jnp.mean(o * o, axis=-1, keepdims=True)
            y = o * lax.rsqrt(ms + RMS_EPS) * g_ref[...] * (1.0 - lambda_init)
            o_ref[:, hh * hw:(hh + 1) * hw] = y.astype(o_ref.dtype)

    for blk in range(seq // B_TQ):
        pl.when(qi == blk)(functools.partial(block, blk))


def _attention_b(proj, cos, sin_signed, lam_vecs, norm_g, *, batch, seq, heads, dh,
                 col_q, col_k, col_v, lambda_init, heads_per_step=4):
    n = proj.shape[0]
    hw = 2 * dh
    gw = heads_per_step * hw
    n_qb = seq // B_TQ
    kern = functools.partial(_attn_b_kernel, heads=heads_per_step, dh=dh, scale=dh ** -0.5,
                             lambda_init=lambda_init)
    return pl.pallas_call(
        kern,
        out_shape=jax.ShapeDtypeStruct((n, heads * hw), BF16),
        grid=(batch, heads // heads_per_step, n_qb),
        in_specs=[
            pl.BlockSpec((B_TQ, gw), lambda b, h, i: (b * n_qb + i, col_q // gw + h)),
            pl.BlockSpec((seq, gw), lambda b, h, i: (b, col_k // gw + h)),
            pl.BlockSpec((seq, gw), lambda b, h, i: (b, col_v // gw + h)),
            pl.BlockSpec((seq, dh), lambda b, h, i: (0, 0)),
            pl.BlockSpec((seq, dh), lambda b, h, i: (0, 0)),
            pl.BlockSpec((4, dh), lambda b, h, i: (0, 0)),
            pl.BlockSpec((1, hw), lambda b, h, i: (0, 0)),
        ],
        out_specs=pl.BlockSpec((B_TQ, gw), lambda b, h, i: (b * n_qb + i, h)),
        scratch_shapes=[pltpu.VMEM((2 * heads_per_step, seq, dh), BF16)],
        compiler_params=_params("parallel", "parallel", "arbitrary"),
        name="attn_diff",
    )(proj, proj, proj, cos, sin_signed, lam_vecs, norm_g)


C_TQ = 512


def _attn_c_kernel(q_ref, k_ref, v_ref, o_ref, *, scale):
    s = lax.dot_general(q_ref[...], k_ref[...], NT_DIMS, preferred_element_type=F32) * scale
    e = jnp.exp(s - jnp.max(s, axis=-1, keepdims=True))
    p = (e / jnp.sum(e, axis=-1, keepdims=True)).astype(BF16)
    o_ref[...] = _dot(p, v_ref[...]).astype(o_ref.dtype)


def _attention_c(proj, ckv, *, batch, seq, mem_len, heads, dh, col_q):
    n = proj.shape[0]
    n_qb = seq // C_TQ
    kern = functools.partial(_attn_c_kernel, scale=dh ** -0.5)
    return pl.pallas_call(
        kern,
        out_shape=jax.ShapeDtypeStruct((n, heads * dh), BF16),
        grid=(batch, heads, n_qb),
        in_specs=[
            pl.BlockSpec((C_TQ, dh), lambda b, h, i: (b * n_qb + i, col_q // dh + h)),
            pl.BlockSpec((mem_len, dh), lambda b, h, i: (b, h)),
            pl.BlockSpec((mem_len, dh), lambda b, h, i: (b, heads + h)),
        ],
        out_specs=pl.BlockSpec((C_TQ, dh), lambda b, h, i: (b * n_qb + i, h)),
        compiler_params=_params("parallel", "parallel", "parallel"),
        name="attn_mem",
    )(proj, ckv, ckv)


def _merge_kernel(x_ref, ya_ref, yb_ref, yc_ref, wga_ref, wgb_ref, wgc_ref,
                  bga_ref, bgb_ref, bgc_ref, pa_ref, pb_ref, pc_ref, o_ref):
    x = x_ref[...]
    acc = None
    for wg, bg, y, p in ((wga_ref, bga_ref, ya_ref, pa_ref),
                         (wgb_ref, bgb_ref, yb_ref, pb_ref),
                         (wgc_ref, bgc_ref, yc_ref, pc_ref)):
        gate = jax.nn.sigmoid(_dot(x, wg[...].astype(BF16)) + bg[...])
        term = gate * _dot(y[...], p[...].astype(BF16))
        acc = term if acc is None else acc + term
    o_ref[...] = acc.astype(o_ref.dtype)


def _gated_merge(xb, ya, yb, yc, w_gates, b_gates, pa, pb, pc, *, tm, tn):
    n, d = xb.shape
    nj = d // tn
    row = lambda width: pl.BlockSpec((tm, width), lambda j, i: (i, 0))
    gate_w = lambda br: pl.BlockSpec((d, tn), lambda j, i, br=br: (0, br * nj + j))
    gate_b = lambda br: pl.BlockSpec((1, tn), lambda j, i, br=br: (0, br * nj + j))
    branch_w = lambda width: pl.BlockSpec((width, tn), lambda j, i: (0, j))
    return pl.pallas_call(
        _merge_kernel,
        out_shape=jax.ShapeDtypeStruct((n, d), BF16),
        grid=(nj, n // tm),
        in_specs=[row(d), row(ya.shape[1]), row(yb.shape[1]), row(yc.shape[1]),
                  gate_w(0), gate_w(1), gate_w(2), gate_b(0), gate_b(1), gate_b(2),
                  branch_w(pa.shape[0]), branch_w(pb.shape[0]), branch_w(pc.shape[0])],
        out_specs=pl.BlockSpec((tm, tn), lambda j, i: (i, j)),
        compiler_params=_params("parallel", "parallel"),
        name="gated_merge",
    )(xb, ya, yb, yc, w_gates, w_gates, w_gates, b_gates, b_gates, b_gates, pa, pb, pc)


def _layer_norm(z, g, b):
    mu = jnp.mean(z, axis=-1, keepdims=True)
    zc = z - mu
    var = jnp.mean(zc * zc, axis=-1, keepdims=True)
    return zc * lax.rsqrt(var + LN_EPS) * g + b


def _ln_router_kernel(x_ref, m_ref, g_ref, b_ref, wr_ref, br_ref,
                      h_ref, hp_ref, idx_ref, gate_ref, rank_ref, cnt_ref, carry_ref,
                      *, alpha, n_exp):
    @pl.when(pl.program_id(0) == 0)
    def _():
        carry_ref[...] = jnp.zeros_like(carry_ref)

    tm = x_ref.shape[0]
    h = _layer_norm(alpha * x_ref[...] + m_ref[...], g_ref[...], b_ref[...])
    h_ref[...] = h
    hp_ref[...] = _pack_halves(h)

    h_hi = h.astype(BF16)
    h_lo = (h - h_hi.astype(F32)).astype(BF16)
    w = wr_ref[...]
    r1 = _dot(h_hi, w)
    logits = r1[:, :LANES] + r1[:, LANES:] + _dot(h_lo, w[:, :LANES]) + br_ref[...]

    lane = lax.broadcasted_iota(I32, (tm, LANES), 1)
    lane_f = lane.astype(F32)
    cur = jnp.where(lane < n_exp, logits, -jnp.inf)
    vals, idxs = [], []
    for _ in range(TOP_K):
        mx = jnp.max(cur, axis=-1, keepdims=True)
        ix = jnp.min(jnp.where(cur == mx, lane_f, float(LANES)), axis=-1, keepdims=True).astype(I32)
        vals.append(mx)
        idxs.append(ix)
        cur = jnp.where(lane == ix, -jnp.inf, cur)
    exps = [jnp.exp(v - vals[0]) for v in vals]
    den = exps[0]
    for e in exps[1:]:
        den = den + e

    tri = (lax.broadcasted_iota(I32, (tm, tm), 0) > lax.broadcasted_iota(I32, (tm, tm), 1)).astype(BF16)
    carry = carry_ref[...]
    idx_out = jnp.zeros((tm, LANES), I32)
    gate_out = jnp.zeros((tm, LANES), F32)
    rank_out = jnp.zeros((tm, LANES), I32)
    for k in range(TOP_K):
        onehot = (lane == idxs[k]).astype(F32)
        before = _dot(tri, onehot.astype(BF16)) + carry
        rank = jnp.sum(onehot * before, axis=-1, keepdims=True)
        carry = carry + jnp.sum(onehot, axis=0, keepdims=True)
        idx_out = jnp.where(lane == k, idxs[k], idx_out)
        gate_out = jnp.where(lane == k, exps[k] / den, gate_out)
        rank_out = jnp.where(lane == k, rank.astype(I32), rank_out)
    carry_ref[...] = carry
    idx_ref[...] = idx_out
    gate_ref[...] = gate_out
    rank_ref[...] = rank_out
    cnt_ref[...] = carry


def _ln_router(x, m, g, b, wr_split, br_pad, *, alpha, n_exp, tm):
    n, d = x.shape
    row = pl.BlockSpec((tm, d), lambda i: (i, 0))
    vec = pl.BlockSpec((1, d), lambda i: (0, 0))
    small = pl.BlockSpec((tm, LANES), lambda i: (i, 0))
    kern = functools.partial(_ln_router_kernel, alpha=alpha, n_exp=n_exp)
    return pl.pallas_call(
        kern,
        out_shape=(jax.ShapeDtypeStruct((n, d), F32),
                   jax.ShapeDtypeStruct((n, d // 2), U32),
                   jax.ShapeDtypeStruct((n, LANES), I32),
                   jax.ShapeDtypeStruct((n, LANES), F32),
                   jax.ShapeDtypeStruct((n, LANES), I32),
                   jax.ShapeDtypeStruct((1, LANES), F32)),
        grid=(n // tm,),
        in_specs=[row, row, vec, vec,
                  pl.BlockSpec((d, 2 * LANES), lambda i: (0, 0)),
                  pl.BlockSpec((1, LANES), lambda i: (0, 0))],
        out_specs=(row, pl.BlockSpec((tm, d // 2), lambda i: (i, 0)), small, small, small,
                   pl.BlockSpec((1, LANES), lambda i: (0, 0))),
        scratch_shapes=[pltpu.VMEM((1, LANES), F32)],
        compiler_params=_params("arbitrary"),
        name="ln_router",
    )(x, m, g, b, wr_split, br_pad)


def _row_copy(src, src_row, dst, dst_row, sem):
    return pltpu.make_async_copy(src.at[pl.ds(src_row, 1)], dst.at[pl.ds(dst_row, 1)], sem)


def _dispatch_kernel(dest_ref, hp_ref, xg_ref, sem):
    tm = hp_ref.shape[0]

    def issue(t, carry):
        for k in range(TOP_K):
            _row_copy(hp_ref, t, xg_ref, dest_ref[0, t * TOP_K + k], sem).start()
        return carry

    lax.fori_loop(0, tm, issue, 0)

    def drain(t, carry):
        for k in range(TOP_K):
            _row_copy(hp_ref, 0, xg_ref, 0, sem).wait()
        return carry

    lax.fori_loop(0, tm, drain, 0)


def _dispatch(dest, hp, *, tm):
    n, w = hp.shape
    dest3 = dest.reshape(n // tm, 1, tm * TOP_K)
    return pl.pallas_call(
        _dispatch_kernel,
        out_shape=jax.ShapeDtypeStruct((n * TOP_K, w), U32),
        grid=(n // tm,),
        in_specs=[pl.BlockSpec((None, 1, tm * TOP_K), lambda i: (i, 0, 0), memory_space=pltpu.SMEM),
                  pl.BlockSpec((tm, w), lambda i: (i, 0))],
        out_specs=pl.BlockSpec(memory_space=pl.ANY),
        scratch_shapes=[pltpu.SemaphoreType.DMA(())],
        compiler_params=_params("arbitrary"),
        name="dispatch",
    )(dest3, hp)


def _cumsum_small(x):
    n = x.shape[0]
    keep = jnp.arange(n)[:, None] >= jnp.arange(n)[None, :]
    return jnp.sum(jnp.where(keep, x[None, :], 0), axis=1).astype(x.dtype)


def _lookup(table, idx):
    hit = idx[..., None] == jnp.arange(table.shape[0], dtype=idx.dtype)
    return jnp.sum(jnp.where(hit, table, 0), axis=-1).astype(table.dtype)


def _visit_schedule(counts, n_rows, tr):
    n_exp = counts.shape[0]
    n_tiles = n_rows // tr
    n_vis = n_tiles + n_exp
    gend = _cumsum_small(counts)
    gstart = gend - counts
    first_tile = gstart // tr
    last_tile = jnp.maximum(gend - 1, 0) // tr
    nvis = jnp.where(counts > 0, last_tile - first_tile + 1, 0)
    vend = _cumsum_small(nvis)
    vstart = vend - nvis
    total = vend[-1]
    v = jnp.arange(n_vis, dtype=I32)
    vc = jnp.minimum(v, total - 1)
    e_v = jnp.minimum(jnp.sum((vend[None, :] <= vc[:, None]).astype(I32), axis=1), n_exp - 1)
    tile_v = _lookup(first_tile, e_v) + (vc - _lookup(vstart, e_v))
    lo = jnp.clip(_lookup(gstart, e_v) - tile_v * tr, 0, tr)
    hi = jnp.clip(_lookup(gend, e_v) - tile_v * tr, 0, tr)
    live = v < total
    lo = jnp.where(live, lo, 0).astype(I32)
    hi = jnp.where(live, hi, 0).astype(I32)
    changed = jnp.concatenate([jnp.ones((1,), I32), (e_v[1:] != e_v[:-1]).astype(I32)])
    run = _cumsum_small(changed) - 1
    later_other = (v[None, :] > v[:, None]) & (e_v[None, :] != e_v[:, None])
    nxt_pos = jnp.min(jnp.where(later_other, v[None, :], n_vis), axis=1)
    nxt_e = jnp.where(nxt_pos < n_vis, _lookup(e_v, jnp.minimum(nxt_pos, n_vis - 1)), -1).astype(I32)
    meta = jnp.stack([run[-1] + 1, e_v[0]]).astype(I32)
    return tile_v.astype(I32), e_v.astype(I32), lo, hi, run.astype(I32), nxt_e, meta


def _visit_state(vt, vlo, vhi):
    v = pl.program_id(1)
    lo = vlo[v]
    hi = vhi[v]
    first = jnp.logical_or(v == 0, vt[v] != vt[jnp.maximum(v - 1, 0)])
    return lo, hi, first


def _resident_weights(ve, run, nxt_e, meta, w_hbm, wbuf, wsem, n_chunks):
    c = pl.program_id(0)
    v = pl.program_id(1)
    tn = wbuf.shape[2]
    slot = lax.rem(c * meta[0] + run[v], 2)
    new_run = jnp.logical_or(v == 0, ve[v] != ve[jnp.maximum(v - 1, 0)])

    def fetch(e, chunk, s):
        for cc in range(n_chunks):
            @pl.when(chunk == cc)
            def _(cc=cc):
                pltpu.make_async_copy(w_hbm.at[e, :, pl.ds(cc * tn, tn)], wbuf.at[s], wsem.at[s]).start()

    @pl.when(jnp.logical_and(c == 0, v == 0))
    def _():
        fetch(ve[0], c, slot)

    @pl.when(new_run)
    def _():
        pltpu.make_async_copy(w_hbm.at[0, :, pl.ds(0, tn)], wbuf.at[slot], wsem.at[slot]).wait()
        more_here = nxt_e[v] >= 0

        @pl.when(more_here)
        def _():
            fetch(nxt_e[v], c, 1 - slot)

        @pl.when(jnp.logical_and(jnp.logical_not(more_here), c + 1 < n_chunks))
        def _():
            fetch(meta[1], c + 1, 1 - slot)

    return slot


def _store_rows(o_ref, val, lo, hi):
    rows = lax.broadcasted_iota(I32, (o_ref.shape[0], 1), 0)
    mine = (rows >= lo) & (rows < hi)
    o_ref[...] = jnp.where(mine, val, o_ref[...])


MOE_SUB = 256


def _for_live_sub_blocks(o_ref, lo, hi, first, body):
    @pl.when(jnp.logical_and(first, hi > lo))
    def _():
        o_ref[...] = jnp.zeros(o_ref.shape, o_ref.dtype)

    subs = []
    for sb in range(o_ref.shape[0] // MOE_SUB):
        r0 = sb * MOE_SUB
        lo_s = jnp.clip(lo - r0, 0, MOE_SUB)
        hi_s = jnp.clip(hi - r0, 0, MOE_SUB)
        subs.append((pl.ds(r0, MOE_SUB), lo_s, hi_s, hi_s > lo_s))
    all_live = functools.reduce(jnp.logical_and, [live for _, _, _, live in subs])

    @pl.when(all_live)
    def _():
        for rows, lo_s, hi_s, _ in subs:
            body(rows, lo_s, hi_s)

    for rows, lo_s, hi_s, live in subs:
        @pl.when(jnp.logical_and(live, jnp.logical_not(all_live)))
        def _(rows=rows, lo_s=lo_s, hi_s=hi_s):
            body(rows, lo_s, hi_s)


SEL_W = 512


def _even_lane_selector():
    r = jnp.arange(SEL_W)[:, None]
    c = jnp.arange(SEL_W // 2)[None, :]
    return (r == 2 * c).astype(BF16)


def _up_kernel(vt, ve, vlo, vhi, run, nxt_e, meta, xg_ref, w1_hbm, b1_ref, sel_ref, o_ref, wbuf, wsem,
               *, n_chunks):
    lo, hi, first = _visit_state(vt, vlo, vhi)
    slot = _resident_weights(ve, run, nxt_e, meta, w1_hbm, wbuf, wsem, n_chunks)
    half = wbuf.shape[1] // 2
    tn = wbuf.shape[2]

    def sub_block(rows, lo_s, hi_s):
        xa, xb = _unpack_halves(xg_ref[rows, :])
        h = (_dot(xa.astype(BF16), wbuf[slot, :half, :].astype(BF16))
             + _dot(xb.astype(BF16), wbuf[slot, half:, :].astype(BF16)) + b1_ref[...])
        glu = jnp.minimum(h, SWIGLU_LIMIT)
        lin = jnp.clip(h, -SWIGLU_LIMIT, SWIGLU_LIMIT) + 1.0
        gact = glu * jax.nn.sigmoid(SWIGLU_ALPHA * glu)
        parts = []
        for c in range(tn // LANES):
            cs = slice(c * LANES, (c + 1) * LANES)
            parts.append(gact[:, cs] * pltpu.roll(lin[:, cs], LANES - 1, 1))
        inter = jnp.concatenate(parts, axis=1).astype(BF16)
        acts = [_dot(inter[:, s * SEL_W:(s + 1) * SEL_W], sel_ref[...]) for s in range(tn // SEL_W)]
        act = jnp.concatenate(acts, axis=1).astype(o_ref.dtype)
        _store_rows(o_ref.at[rows], act, lo_s, hi_s)

    _for_live_sub_blocks(o_ref, lo, hi, first, sub_block)


def _expert_up(sched, xg, w1, b1, *, tr, tn):
    p_rows, w = xg.shape
    n_exp, d, f2 = w1.shape
    n_vis = sched[0].shape[0]
    n_chunks = f2 // tn
    grid_spec = pltpu.PrefetchScalarGridSpec(
        num_scalar_prefetch=len(sched),
        grid=(n_chunks, n_vis),
        in_specs=[
            pl.BlockSpec((tr, w), lambda c, v, vt, *_: (vt[v], 0)),
            pl.BlockSpec(memory_space=pl.ANY),
            pl.BlockSpec((None, 1, tn), lambda c, v, vt, ve, *_: (ve[v], 0, c)),
            pl.BlockSpec((SEL_W, SEL_W // 2), lambda c, v, *_: (0, 0)),
        ],
        out_specs=pl.BlockSpec((tr, tn // 2), lambda c, v, vt, *_: (vt[v], c)),
        scratch_shapes=[pltpu.VMEM((2, d, tn), F32), pltpu.SemaphoreType.DMA((2,))],
    )
    return pl.pallas_call(
        functools.partial(_up_kernel, n_chunks=n_chunks),
        out_shape=jax.ShapeDtypeStruct((p_rows, f2 // 2), BF16),
        grid_spec=grid_spec,
        compiler_params=_params("arbitrary", "arbitrary"),
        name="expert_up",
    )(*sched, xg, w1, b1.reshape(n_exp, 1, f2), _even_lane_selector())


def _down_kernel(vt, ve, vlo, vhi, run, nxt_e, meta, act_ref, w2_hbm, b2_ref, o_ref, wbuf, wsem, *, n_chunks):
    lo, hi, first = _visit_state(vt, vlo, vhi)
    slot = _resident_weights(ve, run, nxt_e, meta, w2_hbm, wbuf, wsem, n_chunks)

    def sub_block(rows, lo_s, hi_s):
        y = _dot(act_ref[rows, :], wbuf[slot].astype(BF16)) + b2_ref[...]
        _store_rows(o_ref.at[rows], _pack_halves(y), lo_s, hi_s)

    _for_live_sub_blocks(o_ref, lo, hi, first, sub_block)


def _expert_down(sched, act, w2, b2, *, tr, tn):
    p_rows, f = act.shape
    n_exp, _, d = w2.shape
    n_vis = sched[0].shape[0]
    n_chunks = d // tn
    grid_spec = pltpu.PrefetchScalarGridSpec(
        num_scalar_prefetch=len(sched),
        grid=(n_chunks, n_vis),
        in_specs=[
            pl.BlockSpec((tr, f), lambda c, v, vt, *_: (vt[v], 0)),
            pl.BlockSpec(memory_space=pl.ANY),
            pl.BlockSpec((None, 1, tn), lambda c, v, vt, ve, *_: (ve[v], 0, c)),
        ],
        out_specs=pl.BlockSpec((tr, tn // 2), lambda c, v, vt, *_: (vt[v], c)),
        scratch_shapes=[pltpu.VMEM((2, f, tn), F32), pltpu.SemaphoreType.DMA((2,))],
    )
    return pl.pallas_call(
        functools.partial(_down_kernel, n_chunks=n_chunks),
        out_shape=jax.ShapeDtypeStruct((p_rows, d // 2), U32),
        grid_spec=grid_spec,
        compiler_params=_params("arbitrary", "arbitrary"),
        name="expert_down",
    )(*sched, act, w2, b2.reshape(n_exp, 1, d))


def _combine_kernel(dest_ref, h_ref, gate_ref, g_ref, b_ref, y_ref, o_ref, buf_ref, sem,
                    *, alpha, chunk):
    tm = h_ref.shape[0]

    def issue(t, carry):
        for k in range(TOP_K):
            pltpu.make_async_copy(y_ref.at[pl.ds(dest_ref[0, t * TOP_K + k], 1)],
                                  buf_ref.at[k, pl.ds(t, 1)], sem).start()
        return carry

    lax.fori_loop(0, tm, issue, 0)

    def drain(t, carry):
        for k in range(TOP_K):
            pltpu.make_async_copy(y_ref.at[pl.ds(0, 1)], buf_ref.at[k, pl.ds(0, 1)], sem).wait()
        return carry

    lax.fori_loop(0, tm, drain, 0)

    gates = gate_ref[...]
    acc_hi = None
    acc_lo = None
    for k in range(TOP_K):
        hi, lo = _unpack_halves(buf_ref[k])
        gk = gates[:, k:k + 1]
        acc_hi = gk * hi if acc_hi is None else acc_hi + gk * hi
        acc_lo = gk * lo if acc_lo is None else acc_lo + gk * lo
    hw = chunk // 2
    pieces = []
    for c in range(acc_hi.shape[1] // hw):
        pieces.append(acc_hi[:, c * hw:(c + 1) * hw])
        pieces.append(acc_lo[:, c * hw:(c + 1) * hw])
    ffn = jnp.concatenate(pieces, axis=1)
    o_ref[...] = _layer_norm(alpha * h_ref[...] + ffn, g_ref[...], b_ref[...]).astype(o_ref.dtype)


def _combine(dest, h, gates, g, b, y, *, alpha, tm, chunk):
    n, d = h.shape
    dest3 = dest.reshape(n // tm, 1, tm * TOP_K)
    kern = functools.partial(_combine_kernel, alpha=alpha, chunk=chunk)
    return pl.pallas_call(
        kern,
        out_shape=jax.ShapeDtypeStruct((n, d), F32),
        grid=(n // tm,),
        in_specs=[pl.BlockSpec((None, 1, tm * TOP_K), lambda i: (i, 0, 0), memory_space=pltpu.SMEM),
                  pl.BlockSpec((tm, d), lambda i: (i, 0)),
                  pl.BlockSpec((tm, LANES), lambda i: (i, 0)),
                  pl.BlockSpec((1, d), lambda i: (0, 0)),
                  pl.BlockSpec((1, d), lambda i: (0, 0)),
                  pl.BlockSpec(memory_space=pl.ANY)],
        out_specs=pl.BlockSpec((tm, d), lambda i: (i, 0)),
        scratch_shapes=[pltpu.VMEM((TOP_K, tm, d // 2), U32), pltpu.SemaphoreType.DMA(())],
        compiler_params=_params("arbitrary"),
        name="combine_ln",
    )(dest3, h, gates, g, b, y)


def _tiles(n_tokens, d_model, d_expert):
    return dict(
        cast_tm=min(512, n_tokens),
        proj=dict(tm=min(1024, n_tokens), tn=512),
        memkv=dict(tm=512, tn=512),
        merge=dict(tm=min(512, n_tokens), tn=256),
        out=dict(tm=min(1024, n_tokens), tn=512),
        ln_tm=min(256, n_tokens),
        dispatch_tm=min(256, n_tokens),
        moe_tr=512,
        up_tn=min(1024, 2 * d_expert),
        down_tn=min(2048, d_model),
        combine_tm=min(128, n_tokens),
    )


def _layer(h, mem2, lw, *, batch, seq, mem_len, lambda_init, alpha):
    n, d = h.shape
    t = _tiles(n, d, lw["w_mlp2"].shape[1])
    a_width = A_HEADS * A_HEAD_DIM
    b_width = B_HEADS * 2 * B_HEAD_DIM
    col = dict(a_q=0, a_k=a_width, a_v=2 * a_width, b_q=3 * a_width, b_k=3 * a_width + b_width,
               b_v=3 * a_width + 2 * b_width, c_q=3 * a_width + 3 * b_width)

    xb = _cast_bf16(h, t["cast_tm"])
    proj = _matmul(xb, lw["w_in"], out_dtype=BF16, name="in_proj", **t["proj"])
    ckv = _matmul(mem2, lw["w_mem_kv"], out_dtype=BF16, name="mem_kv",
                  tm=min(t["memkv"]["tm"], mem2.shape[0]), tn=t["memkv"]["tn"])

    ya = _attention_a(proj, _band_bias_base(lw["rel_bias"]), batch=batch, seq=seq,
                      heads=A_HEADS, dh=A_HEAD_DIM, col_q=col["a_q"], col_k=col["a_k"], col_v=col["a_v"])
    cos, sin_signed = _rope_tables(seq, B_HEAD_DIM)
    lam_vecs = jnp.stack([lw["lambda_q1"], lw["lambda_k1"], lw["lambda_q2"], lw["lambda_k2"]]).astype(F32)
    yb = _attention_b(proj, cos, sin_signed, lam_vecs, lw["diff_norm_g"].reshape(1, -1),
                      batch=batch, seq=seq, heads=B_HEADS, dh=B_HEAD_DIM,
                      col_q=col["b_q"], col_k=col["b_k"], col_v=col["b_v"], lambda_init=lambda_init)
    yc = _attention_c(proj, ckv, batch=batch, seq=seq, mem_len=mem_len, heads=C_HEADS,
                      dh=C_HEAD_DIM, col_q=col["c_q"])

    merged = _gated_merge(xb, ya, yb, yc, lw["w_gates"], lw["b_gates"].reshape(1, -1),
                          lw["w_branch_a"], lw["w_branch_b"], lw["w_branch_c"], **t["merge"])
    mix = _matmul(merged, lw["w_o"], out_dtype=F32, name="out_proj", **t["out"])

    n_exp = lw["w_router"].shape[1]
    wr = jnp.pad(lw["w_router"], ((0, 0), (0, LANES - n_exp)))
    wr_hi = wr.astype(BF16)
    wr_lo = (wr - wr_hi.astype(F32)).astype(BF16)
    br = jnp.pad(lw["b_router"], (0, LANES - n_exp)).reshape(1, LANES)
    h1, h1_packed, top_idx, gates, rank, cnt = _ln_router(
        h, mix, lw["ln1_g"].reshape(1, -1), lw["ln1_b"].reshape(1, -1),
        jnp.concatenate([wr_hi, wr_lo], axis=1), br, alpha=alpha, n_exp=n_exp, tm=t["ln_tm"])

    counts = cnt[0, :n_exp].astype(I32)
    gstart = _cumsum_small(counts) - counts
    dest = _lookup(gstart, top_idx[:, :TOP_K]) + rank[:, :TOP_K]
    sched = _visit_schedule(counts, n * TOP_K, t["moe_tr"])

    xg = _dispatch(dest, h1_packed, tm=t["dispatch_tm"])
    act = _expert_up(sched, xg, lw["w_mlp1"], lw["b_mlp1"], tr=t["moe_tr"], tn=t["up_tn"])
    y = _expert_down(sched, act, lw["w_mlp2"], lw["b_mlp2"], tr=t["moe_tr"], tn=t["down_tn"])
    return _combine(dest, h1, gates, lw["ln2_g"].reshape(1, -1), lw["ln2_b"].reshape(1, -1), y,
                    alpha=alpha, tm=t["combine_tm"], chunk=t["down_tn"])


def kernel(x, mem, w_in, w_mem_kv, rel_bias, lambda_q1, lambda_k1, lambda_q2, lambda_k2, diff_norm_g,
           w_branch_a, w_branch_b, w_branch_c, w_gates, b_gates, w_o, ln1_g, ln1_b, w_router, b_router,
           w_mlp1, b_mlp1, w_mlp2, b_mlp2, ln2_g, ln2_b):
    batch, seq, d = x.shape
    mem_len = mem.shape[1]
    depth = w_in.shape[0]
    alpha = (2 * depth) ** 0.25
    stacked = dict(w_in=w_in, w_mem_kv=w_mem_kv, rel_bias=rel_bias, lambda_q1=lambda_q1,
                   lambda_k1=lambda_k1, lambda_q2=lambda_q2, lambda_k2=lambda_k2, diff_norm_g=diff_norm_g,
                   w_branch_a=w_branch_a, w_branch_b=w_branch_b, w_branch_c=w_branch_c, w_gates=w_gates,
                   b_gates=b_gates, w_o=w_o, ln1_g=ln1_g, ln1_b=ln1_b, w_router=w_router,
                   b_router=b_router, w_mlp1=w_mlp1, b_mlp1=b_mlp1, w_mlp2=w_mlp2, b_mlp2=b_mlp2,
                   ln2_g=ln2_g, ln2_b=ln2_b)
    h = x.reshape(batch * seq, d)
    mem2 = mem.reshape(batch * mem_len, d)
    for l in range(depth):
        lw = {name: w[l] for name, w in stacked.items()}
        lambda_init = 0.8 - 0.6 * math.exp(-0.3 * l)
        h = _layer(h, mem2, lw, batch=batch, seq=seq, mem_len=mem_len, lambda_init=lambda_init, alpha=alpha)
    return h.reshape(batch, seq, d)
```

```python
import functools
import math

import jax
import jax.numpy as jnp
from jax import lax
from jax.experimental import pallas as pl
from jax.experimental.pallas import tpu as pltpu

F32 = jnp.float32
BF16 = jnp.bfloat16
U32 = jnp.uint32
I32 = jnp.int32

CHUNK = 64
LEFT_CHUNKS = 8
MAX_REL = 128
A_HEADS = 16
A_HEAD_DIM = 128
B_HEADS = 4
B_HEAD_DIM = 128
C_HEADS = 4
C_HEAD_DIM = 256
N_BRANCHES = 3
ROPE_THETA = 10000.0
TOP_K = 4
SWIGLU_LIMIT = 7.0
SWIGLU_ALPHA = 1.702
LN_EPS = 1e-5
RMS_EPS = 1e-5
MASK_VALUE = -1e30

V7X_VMEM_BYTES = 64 * 1024 * 1024
V7X_VMEM_LIMIT = V7X_VMEM_BYTES - 8 * 1024 * 1024
LANES = 128

NT_DIMS = (((1,), (1,)), ((), ()))


def _params(*semantics):
    return pltpu.CompilerParams(dimension_semantics=semantics,
                                vmem_limit_bytes=V7X_VMEM_LIMIT)


def _dot(a, b):
    return jnp.dot(a, b, preferred_element_type=F32)


def _pack_halves(x):
    w = x.shape[1] // 2
    hi = lax.bitcast_convert_type(x[:, :w].astype(jnp.bfloat16).astype(F32), U32)
    lo = lax.bitcast_convert_type(x[:, w:].astype(jnp.bfloat16).astype(F32), U32)
    return hi | (lo >> 16)


def _unpack_halves(p):
    hi = lax.bitcast_convert_type(p & jnp.uint32(0xFFFF0000), F32)
    lo = lax.bitcast_convert_type(p << 16, F32)
    return hi, lo


def _cast_kernel(x_ref, o_ref):
    o_ref[...] = x_ref[...].astype(o_ref.dtype)


def _cast_bf16(x, tm):
    m, d = x.shape
    return pl.pallas_call(
        _cast_kernel,
        out_shape=jax.ShapeDtypeStruct((m, d), BF16),
        grid=(m // tm,),
        in_specs=[pl.BlockSpec((tm, d), lambda i: (i, 0))],
        out_specs=pl.BlockSpec((tm, d), lambda i: (i, 0)),
        compiler_params=_params("parallel"),
        name="cast_bf16",
    )(x)


def _mm_kernel(a_ref, w_ref, o_ref):
    a = a_ref[...].astype(BF16)
    o_ref[...] = _dot(a, w_ref[...].astype(BF16)).astype(o_ref.dtype)


def _matmul(a, w, *, tm, tn, out_dtype, name):
    m, k = a.shape
    n = w.shape[1]
    return pl.pallas_call(
        _mm_kernel,
        out_shape=jax.ShapeDtypeStruct((m, n), out_dtype),
        grid=(n // tn, m // tm),
        in_specs=[pl.BlockSpec((tm, k), lambda j, i: (i, 0)),
                  pl.BlockSpec((k, tn), lambda j, i: (0, j))],
        out_specs=pl.BlockSpec((tm, tn), lambda j, i: (i, j)),
        compiler_params=_params("parallel", "parallel"),
        name=name,
    )(a, w)


A_QCHUNKS = 4
A_TQ = A_QCHUNKS * CHUNK
A_WIN = (LEFT_CHUNKS + A_QCHUNKS) * CHUNK
A_VARIANTS = LEFT_CHUNKS * CHUNK // A_TQ + 1


A_BASE_W = A_WIN + A_TQ


def _band_bias_base(rel_bias):
    reach = A_WIN
    ext =jnp.pad(rel_bias.astype(F32), ((0, 0), (reach - MAX_REL, reach - MAX_REL)), mode="edge")
    rev = ext[:, ::-1]
    rows = []
    for v in range(A_VARIANTS):
        c = rev[:, reach - A_TQ * v - A_TQ: reach - A_TQ * v + A_WIN]
        rows.append(jnp.concatenate([c[:, A_TQ:], c[:, :A_TQ]], axis=1))
    return jnp.stack(rows)[:, :, None, :]


def _attn_a_kernel(q_ref, k_ref, v_ref, base_ref, o_ref, tb_ref, *, heads, dh, scale):
    i = pl.program_id(2)

    @pl.when(i == 0)
    def _():
        r = lax.broadcasted_iota(I32, (A_TQ, A_WIN), 0)
        j = lax.broadcasted_iota(I32, (A_TQ, A_WIN), 1)
        for var in range(A_VARIANTS):
            cdiff = (A_TQ * var + r) // CHUNK - j // CHUNK
            valid = (cdiff >= 0) & (cdiff <= LEFT_CHUNKS)
            for h in range(heads):
                rows = jnp.broadcast_to(base_ref[var, h], (A_TQ, A_BASE_W))
                toeplitz = pltpu.roll(rows, 0, 1, stride=1, stride_axis=0)[:, :A_WIN]
                tb_ref[var, h] = jnp.where(valid, toeplitz, MASK_VALUE)

    var = jnp.minimum(i, A_VARIANTS - 1)
    start = pl.multiple_of(jnp.maximum(i - (A_VARIANTS - 1), 0) * A_TQ, A_TQ)
    for h in range(heads):
        cs = slice(h * dh, (h + 1) * dh)
        q = q_ref[:, cs]
        k = k_ref[pl.ds(start, A_WIN), cs]
        v = v_ref[pl.ds(start, A_WIN), cs]
        s = lax.dot_general(q, k, NT_DIMS, preferred_element_type=F32) * scale + tb_ref[var, h]
        m = jnp.max(s, axis=-1, keepdims=True)
        p = jnp.exp(s - m)
        l = jnp.sum(p, axis=-1, keepdims=True)
        o = _dot(p.astype(BF16), v)
        o_ref[:, cs] = (o / l).astype(o_ref.dtype)


def _attention_a(proj, base, *, batch, seq, heads, dh, col_q, col_k, col_v, heads_per_step=8):
    n = proj.shape[0]
    gw = heads_per_step * dh
    n_groups = heads // heads_per_step
    n_qb = seq // A_TQ
    kern = functools.partial(_attn_a_kernel, heads=heads_per_step, dh=dh, scale=dh ** -0.5)
    return pl.pallas_call(
        kern,
        out_shape=jax.ShapeDtypeStruct((n, heads * dh), BF16),
        grid=(batch, n_groups, n_qb),
        in_specs=[
            pl.BlockSpec((A_TQ, gw), lambda b, g, i: (b * n_qb + i, col_q // gw + g)),
            pl.BlockSpec((seq, gw), lambda b, g, i: (b, col_k // gw + g)),
            pl.BlockSpec((seq, gw), lambda b, g, i: (b, col_v // gw + g)),
            pl.BlockSpec((A_VARIANTS, heads_per_step, 1, A_BASE_W), lambda b, g, i: (0, g, 0, 0)),
        ],
        out_specs=pl.BlockSpec((A_TQ, gw), lambda b, g, i: (b * n_qb + i, g)),
        scratch_shapes=[pltpu.VMEM((A_VARIANTS, heads_per_step, A_TQ, A_WIN), F32)],
        compiler_params=_params("parallel", "parallel", "arbitrary"),
        name="attn_band",
    )(proj, proj, proj, base)


B_TQ = 256


def _rope_tables(seq, dim):
    inv = 1.0 / (ROPE_THETA ** (jnp.arange(0, dim, 2, dtype=F32) / dim))
    ang = jnp.arange(seq, dtype=F32)[:, None] * inv[None, :]
    ang = jnp.concatenate([ang, ang], -1)
    sign = jnp.where(jnp.arange(dim) < dim // 2, -1.0, 1.0).astype(F32)
    return jnp.cos(ang), jnp.sin(ang) * sign[None, :]


def _rope(x, cos, sin_signed):
    return x * cos + pltpu.roll(x, x.shape[1] // 2, 1) * sin_signed


def _attn_b_kernel(q_ref, k_ref, v_ref, cos_ref, sin_ref, lam_ref, g_ref, o_ref, krot_ref,
                   *, heads, dh, scale, lambda_init):
    qi = pl.program_id(2)
    seq = k_ref.shape[0]
    hw = 2 * dh

    @pl.when(qi == 0)
    def _():
        for hm in range(2 * heads):
            kf = k_ref[:, hm * dh:(hm + 1) * dh].astype(F32)
            krot_ref[hm] = _rope(kf, cos_ref[...], sin_ref[...]).astype(BF16)

    lv = lam_ref[...]
    lam = (jnp.exp(jnp.sum(lv[0:1] * lv[1:2], axis=-1, keepdims=True))
           - jnp.exp(jnp.sum(lv[2:3] * lv[3:4], axis=-1, keepdims=True)) + lambda_init)

    def block(blk):
        row0 = blk * B_TQ
        kl = row0 + B_TQ
        cos_q = cos_ref[row0:kl, :]
        sin_q = sin_ref[row0:kl, :]
        q_chunk = lax.broadcasted_iota(I32, (B_TQ, B_TQ), 0) // CHUNK
        k_chunk = lax.broadcasted_iota(I32, (B_TQ, B_TQ), 1) // CHUNK
        allowed = k_chunk <= q_chunk
        for hh in range(heads):
            probs = []
            for m in range(2):
                hm = 2 * hh + m
                qf = q_ref[:, hm * dh:(hm + 1) * dh].astype(F32)
                qr = (_rope(qf, cos_q, sin_q) * scale).astype(BF16)
                s = lax.dot_general(qr, krot_ref[hm, :kl, :], NT_DIMS, preferred_element_type=F32)
                diag = jnp.where(allowed, s[:, row0:], MASK_VALUE)
                s = diag if row0 == 0 else jnp.concatenate([s[:, :row0], diag], axis=1)
                e = jnp.exp(s - jnp.max(s, axis=-1, keepdims=True))
                probs.append(e / jnp.sum(e, axis=-1, keepdims=True))
            w = (probs[0] - lam * probs[1]).astype(BF16)
            o = _dot(w, v_ref[:kl, hh * hw:(hh + 1) * hw])
            ms = jnp.mean(o * o, axis=-1, keepdims=True)
            y = o * lax.rsqrt(ms + RMS_EPS) * g_ref[...] * (1.0 - lambda_init)
            o_ref[:, hh * hw:(hh + 1) * hw] = y.astype(o_ref.dtype)

    for blk in range(seq // B_TQ):
        pl.when(qi == blk)(functools.partial(block, blk))


def _attention_b(proj, cos, sin_signed, lam_vecs, norm_g, *, batch, seq, heads, dh,
                 col_q, col_k, col_v, lambda_init, heads_per_step=2):
    n = proj.shape[0]
    hw = 2 * dh
    gw = heads_per_step * hw
    n_qb = seq // B_TQ
    kern = functools.partial(_attn_b_kernel, heads=heads_per_step, dh=dh, scale=dh ** -0.5,
                             lambda_init=lambda_init)
    return pl.pallas_call(
        kern,
        out_shape=jax.ShapeDtypeStruct((n, heads * hw), BF16),
        grid=(batch, heads // heads_per_step, n_qb),
        in_specs=[
            pl.BlockSpec((B_TQ, gw), lambda b, h, i: (b * n_qb + i, col_q // gw + h)),
            pl.BlockSpec((seq, gw), lambda b, h, i: (b, col_k // gw + h)),
            pl.BlockSpec((seq, gw), lambda b, h, i: (b, col_v // gw + h)),
            pl.BlockSpec((seq, dh), lambda b, h, i: (0, 0)),
            pl.BlockSpec((seq, dh), lambda b, h, i: (0, 0)),
            pl.BlockSpec((4, dh), lambda b, h, i: (0, 0)),
            pl.BlockSpec((1, hw), lambda b, h, i: (0, 0)),
        ],
        out_specs=pl.BlockSpec((B_TQ, gw), lambda b, h, i: (b * n_qb + i, h)),
        scratch_shapes=[pltpu.VMEM((2 * heads_per_step, seq, dh), BF16)],
        compiler_params=_params("parallel", "parallel", "arbitrary"),
        name="attn_diff",
    )(proj, proj, proj, cos, sin_signed, lam_vecs, norm_g)


C_TQ = 512


def _attn_c_kernel(q_ref, k_ref, v_ref, o_ref, *, scale):
    s = lax.dot_general(q_ref[...], k_ref[...], NT_DIMS, preferred_element_type=F32) * scale
    e = jnp.exp(s - jnp.max(s, axis=-1, keepdims=True))
    p = (e / jnp.sum(e, axis=-1, keepdims=True)).astype(BF16)
    o_ref[...] = _dot(p, v_ref[...]).astype(o_ref.dtype)


def _attention_c(proj, ckv, *, batch, seq, mem_len, heads, dh, col_q):
    n = proj.shape[0]
    n_qb = seq // C_TQ
    kern = functools.partial(_attn_c_kernel, scale=dh ** -0.5)
    return pl.pallas_call(
        kern,
        out_shape=jax.ShapeDtypeStruct((n, heads * dh), BF16),
        grid=(batch, heads, n_qb),
        in_specs=[
            pl.BlockSpec((C_TQ, dh), lambda b, h, i: (b * n_qb + i, col_q // dh + h)),
            pl.BlockSpec((mem_len, dh), lambda b, h, i: (b, h)),
            pl.BlockSpec((mem_len, dh), lambda b, h, i: (b, heads + h)),
        ],
        out_specs=pl.BlockSpec((C_TQ, dh), lambda b, h, i: (b * n_qb + i, h)),
        compiler_params=_params("parallel", "parallel", "parallel"),
        name="attn_mem",
    )(proj, ckv, ckv)


def _merge_kernel(x_ref, ya_ref, yb_ref, yc_ref, wga_ref, wgb_ref, wgc_ref,
                  bga_ref, bgb_ref, bgc_ref, pa_ref, pb_ref, pc_ref, o_ref):
    x = x_ref[...]
    acc = None
    for wg, bg, y, p in ((wga_ref, bga_ref, ya_ref, pa_ref),
                         (wgb_ref, bgb_ref, yb_ref, pb_ref),
                         (wgc_ref, bgc_ref, yc_ref, pc_ref)):
        gate = jax.nn.sigmoid(_dot(x, wg[...].astype(BF16)) + bg[...])
        term = gate * _dot(y[...], p[...].astype(BF16))
        acc = term if acc is None else acc + term
    o_ref[...] = acc.astype(o_ref.dtype)


def _gated_merge(xb, ya, yb, yc, w_gates, b_gates, pa, pb, pc, *, tm, tn):
    n, d = xb.shape
    nj = d // tn
    row = lambda width: pl.BlockSpec((tm, width), lambda j, i: (i, 0))
    gate_w = lambda br: pl.BlockSpec((d, tn), lambda j, i, br=br: (0, br * nj + j))
    gate_b = lambda br: pl.BlockSpec((1, tn), lambda j, i, br=br: (0, br * nj + j))
    branch_w = lambda width: pl.BlockSpec((width, tn), lambda j, i: (0, j))
    return pl.pallas_call(
        _merge_kernel,
        out_shape=jax.ShapeDtypeStruct((n, d), BF16),
        grid=(nj, n // tm),
        in_specs=[row(d), row(ya.shape[1]), row(yb.shape[1]), row(yc.shape[1]),
                  gate_w(0), gate_w(1), gate_w(2), gate_b(0), gate_b(1), gate_b(2),
                  branch_w(pa.shape[0]), branch_w(pb.shape[0]), branch_w(pc.shape[0])],
        out_specs=pl.BlockSpec((tm, tn), lambda j, i: (i, j)),
        compiler_params=_params("parallel", "parallel"),
        name="gated_merge",
    )(xb, ya, yb, yc, w_gates, w_gates, w_gates, b_gates, b_gates, b_gates, pa, pb, pc)


def _layer_norm(z, g, b):
    mu = jnp.mean(z, axis=-1, keepdims=True)
    zc = z - mu
    var = jnp.mean(zc * zc, axis=-1, keepdims=True)
    return zc * lax.rsqrt(var + LN_EPS) * g + b


def _ln_router_kernel(x_ref, m_ref, g_ref, b_ref, wr_ref, br_ref,
                      h_ref, hp_ref, idx_ref, gate_ref, rank_ref, cnt_ref, carry_ref,
                      *, alpha, n_exp):
    @pl.when(pl.program_id(0) == 0)
    def _():
        carry_ref[...] = jnp.zeros_like(carry_ref)

    tm = x_ref.shape[0]
    h = _layer_norm(alpha * x_ref[...] + m_ref[...], g_ref[...], b_ref[...])
    h_ref[...] = h
    hp_ref[...] = _pack_halves(h)

    h_hi = h.astype(BF16)
    h_lo = (h - h_hi.astype(F32)).astype(BF16)
    w = wr_ref[...]
    r1 = _dot(h_hi, w)
    logits = r1[:, :LANES] + r1[:, LANES:] + _dot(h_lo, w[:, :LANES]) + br_ref[...]

    lane = lax.broadcasted_iota(I32, (tm, LANES), 1)
    lane_f = lane.astype(F32)
    cur = jnp.where(lane < n_exp, logits, -jnp.inf)
    vals, idxs = [], []
    for _ in range(TOP_K):
        mx = jnp.max(cur, axis=-1, keepdims=True)
        ix = jnp.min(jnp.where(cur == mx, lane_f, float(LANES)), axis=-1, keepdims=True).astype(I32)
        vals.append(mx)
        idxs.append(ix)
        cur = jnp.where(lane == ix, -jnp.inf, cur)
    exps = [jnp.exp(v - vals[0]) for v in vals]
    den = exps[0]
    for e in exps[1:]:
        den = den + e

    tri = (lax.broadcasted_iota(I32, (tm, tm), 0) > lax.broadcasted_iota(I32, (tm, tm), 1)).astype(BF16)
    carry = carry_ref[...]
    idx_out = jnp.zeros((tm, LANES), I32)
    gate_out = jnp.zeros((tm, LANES), F32)
    rank_out = jnp.zeros((tm, LANES), I32)
    for k in range(TOP_K):
        onehot = (lane == idxs[k]).astype(F32)
        before = _dot(tri, onehot.astype(BF16)) + carry
        rank = jnp.sum(onehot * before, axis=-1, keepdims=True)
        carry = carry + jnp.sum(onehot, axis=0, keepdims=True)
        idx_out = jnp.where(lane == k, idxs[k], idx_out)
        gate_out = jnp.where(lane == k, exps[k] / den, gate_out)
        rank_out = jnp.where(lane == k, rank.astype(I32), rank_out)
    carry_ref[...] = carry
    idx_ref[...] = idx_out
    gate_ref[...] = gate_out
    rank_ref[...] = rank_out
    cnt_ref[...] = carry


def _ln_router(x, m, g, b, wr_split, br_pad, *, alpha, n_exp, tm):
    n, d = x.shape
    row = pl.BlockSpec((tm, d), lambda i: (i, 0))
    vec = pl.BlockSpec((1, d), lambda i: (0, 0))
    small = pl.BlockSpec((tm, LANES), lambda i: (i, 0))
    kern = functools.partial(_ln_router_kernel, alpha=alpha, n_exp=n_exp)
    return pl.pallas_call(
        kern,
        out_shape=(jax.ShapeDtypeStruct((n, d), F32),
                   jax.ShapeDtypeStruct((n, d // 2), U32),
                   jax.ShapeDtypeStruct((n, LANES), I32),
                   jax.ShapeDtypeStruct((n, LANES), F32),
                   jax.ShapeDtypeStruct((n, LANES), I32),
                   jax.ShapeDtypeStruct((1, LANES), F32)),
        grid=(n // tm,),
        in_specs=[row, row, vec, vec,
                  pl.BlockSpec((d, 2 * LANES), lambda i: (0, 0)),
                  pl.BlockSpec((1, LANES), lambda i: (0, 0))],
        out_specs=(row, pl.BlockSpec((tm, d // 2), lambda i: (i, 0)), small, small, small,
                   pl.BlockSpec((1, LANES), lambda i: (0, 0))),
        scratch_shapes=[pltpu.VMEM((1, LANES), F32)],
        compiler_params=_params("arbitrary"),
        name="ln_router",
    )(x, m, g, b, wr_split, br_pad)


def _row_copy(src, src_row, dst, dst_row, sem):
    return pltpu.make_async_copy(src.at[pl.ds(src_row, 1)], dst.at[pl.ds(dst_row, 1)], sem)


def _dispatch_kernel(dest_ref, hp_ref, xg_ref, sem):
    tm = hp_ref.shape[0]

    def issue(t, carry):
        for k in range(TOP_K):
            _row_copy(hp_ref, t, xg_ref, dest_ref[0, t * TOP_K + k], sem).start()
        return carry

    lax.fori_loop(0, tm, issue, 0)

    def drain(t, carry):
        for k in range(TOP_K):
            _row_copy(hp_ref, 0, xg_ref, 0, sem).wait()
        return carry

    lax.fori_loop(0, tm, drain, 0)


def _dispatch(dest, hp, *, tm):
    n, w = hp.shape
    dest3 = dest.reshape(n // tm, 1, tm * TOP_K)
    return pl.pallas_call(
        _dispatch_kernel,
        out_shape=jax.ShapeDtypeStruct((n * TOP_K, w), U32),
        grid=(n // tm,),
        in_specs=[pl.BlockSpec((None, 1, tm * TOP_K), lambda i: (i, 0, 0), memory_space=pltpu.SMEM),
                  pl.BlockSpec((tm, w), lambda i: (i, 0))],
        out_specs=pl.BlockSpec(memory_space=pl.ANY),
        scratch_shapes=[pltpu.SemaphoreType.DMA(())],
        compiler_params=_params("arbitrary"),
        name="dispatch",
    )(dest3, hp)


def _cumsum_small(x):
    n = x.shape[0]
    keep = jnp.arange(n)[:, None] >= jnp.arange(n)[None, :]
    return jnp.sum(jnp.where(keep, x[None, :], 0), axis=1).astype(x.dtype)


def _lookup(table, idx):
    hit = idx[..., None] == jnp.arange(table.shape[0], dtype=idx.dtype)
    return jnp.sum(jnp.where(hit, table, 0), axis=-1).astype(table.dtype)


def _visit_schedule(counts, n_rows, tr):
    n_exp = counts.shape[0]
    n_tiles = n_rows // tr
    n_vis = n_tiles + n_exp
    gend = _cumsum_small(counts)
    gstart = gend - counts
    first_tile = gstart // tr
    last_tile = jnp.maximum(gend - 1, 0) // tr
    nvis = jnp.where(counts > 0, last_tile - first_tile + 1, 0)
    vend = _cumsum_small(nvis)
    vstart = vend - nvis
    total = vend[-1]
    v = jnp.arange(n_vis, dtype=I32)
    vc = jnp.minimum(v, total - 1)
    e_v = jnp.minimum(jnp.sum((vend[None, :] <= vc[:, None]).astype(I32), axis=1), n_exp - 1)
    tile_v = _lookup(first_tile, e_v) + (vc - _lookup(vstart, e_v))
    lo = jnp.clip(_lookup(gstart, e_v) - tile_v * tr, 0, tr)
    hi = jnp.clip(_lookup(gend, e_v) - tile_v * tr, 0, tr)
    live = v < total
    lo = jnp.where(live, lo, 0).astype(I32)
    hi = jnp.where(live, hi, 0).astype(I32)
    changed = jnp.concatenate([jnp.ones((1,), I32), (e_v[1:] != e_v[:-1]).astype(I32)])
    run = _cumsum_small(changed) - 1
    later_other = (v[None, :] > v[:, None]) & (e_v[None, :] != e_v[:, None])
    nxt_pos = jnp.min(jnp.where(later_other, v[None, :], n_vis), axis=1)
    nxt_e = jnp.where(nxt_pos < n_vis, _lookup(e_v, jnp.minimum(nxt_pos, n_vis - 1)), -1).astype(I32)
    meta = jnp.stack([run[-1] + 1, e_v[0]]).astype(I32)
    return tile_v.astype(I32), e_v.astype(I32), lo, hi, run.astype(I32), nxt_e, meta


def _visit_state(vt, vlo, vhi):
    v = pl.program_id(1)
    lo = vlo[v]
    hi = vhi[v]
    first = jnp.logical_or(v == 0, vt[v] != vt[jnp.maximum(v - 1, 0)])
    return lo, hi, first


def _resident_weights(ve, run, nxt_e, meta, w_hbm, wbuf, wsem, n_chunks):
    c = pl.program_id(0)
    v = pl.program_id(1)
    tn = wbuf.shape[2]
    slot = lax.rem(c * meta[0] + run[v], 2)
    new_run = jnp.logical_or(v == 0, ve[v] != ve[jnp.maximum(v - 1, 0)])

    def fetch(e, chunk, s):
        for cc in range(n_chunks):
            @pl.when(chunk == cc)
            def _(cc=cc):
                pltpu.make_async_copy(w_hbm.at[e, :, pl.ds(cc * tn, tn)], wbuf.at[s], wsem.at[s]).start()

    @pl.when(jnp.logical_and(c == 0, v == 0))
    def _():
        fetch(ve[0], c, slot)

    @pl.when(new_run)
    def _():
        pltpu.make_async_copy(w_hbm.at[0, :, pl.ds(0, tn)], wbuf.at[slot], wsem.at[slot]).wait()
        more_here = nxt_e[v] >= 0

        @pl.when(more_here)
        def _():
            fetch(nxt_e[v], c, 1 - slot)

        @pl.when(jnp.logical_and(jnp.logical_not(more_here), c + 1 < n_chunks))
        def _():
            fetch(meta[1], c + 1, 1 - slot)

    return slot


def _store_rows(o_ref, val, lo, hi):
    rows = lax.broadcasted_iota(I32, (o_ref.shape[0], 1), 0)
    mine = (rows >= lo) & (rows < hi)
    o_ref[...] = jnp.where(mine, val, o_ref[...])


MOE_SUB = 256


def _for_live_sub_blocks(o_ref, lo, hi, first, body):
    @pl.when(jnp.logical_and(first, hi > lo))
    def _():
        o_ref[...] = jnp.zeros(o_ref.shape, o_ref.dtype)

    subs = []
    for sb in range(o_ref.shape[0] // MOE_SUB):
        r0 = sb * MOE_SUB
        lo_s = jnp.clip(lo - r0, 0, MOE_SUB)
        hi_s = jnp.clip(hi - r0, 0, MOE_SUB)
        subs.append((pl.ds(r0, MOE_SUB), lo_s, hi_s, hi_s > lo_s))
    all_live = functools.reduce(jnp.logical_and, [live for _, _, _, live in subs])

    @pl.when(all_live)
    def _():
        for rows, lo_s, hi_s, _ in subs:
            body(rows, lo_s, hi_s)

    for rows, lo_s, hi_s, live in subs:
        @pl.when(jnp.logical_and(live, jnp.logical_not(all_live)))
        def _(rows=rows, lo_s=lo_s, hi_s=hi_s):
            body(rows, lo_s, hi_s)


SEL_W = 512


def _even_lane_selector():
    r = jnp.arange(SEL_W)[:, None]
    c = jnp.arange(SEL_W // 2)[None, :]
    return (r == 2 * c).astype(BF16)


def _up_kernel(vt, ve, vlo, vhi, run, nxt_e, meta, xg_ref, w1_hbm, b1_ref, sel_ref, o_ref, wbuf, wsem,
               *, n_chunks):
    lo, hi, first = _visit_state(vt, vlo, vhi)
    slot = _resident_weights(ve, run, nxt_e, meta, w1_hbm, wbuf, wsem, n_chunks)
    half = wbuf.shape[1] // 2
    tn = wbuf.shape[2]

    def sub_block(rows, lo_s, hi_s):
        xa, xb = _unpack_halves(xg_ref[rows, :])
        h = (_dot(xa.astype(BF16), wbuf[slot, :half, :].astype(BF16))
             + _dot(xb.astype(BF16), wbuf[slot, half:, :].astype(BF16)) + b1_ref[...])
        glu = jnp.minimum(h, SWIGLU_LIMIT)
        lin = jnp.clip(h, -SWIGLU_LIMIT, SWIGLU_LIMIT) + 1.0
        gact = glu * jax.nn.sigmoid(SWIGLU_ALPHA * glu)
        parts = []
        for c in range(tn // LANES):
            cs = slice(c * LANES, (c + 1) * LANES)
            parts.append(gact[:, cs] * pltpu.roll(lin[:, cs], LANES - 1, 1))
        inter = jnp.concatenate(parts, axis=1).astype(BF16)
        acts = [_dot(inter[:, s * SEL_W:(s + 1) * SEL_W], sel_ref[...]) for s in range(tn // SEL_W)]
        act = jnp.concatenate(acts, axis=1).astype(o_ref.dtype)
        _store_rows(o_ref.at[rows], act, lo_s, hi_s)

    _for_live_sub_blocks(o_ref, lo, hi, first, sub_block)


def _expert_up(sched, xg, w1, b1, *, tr, tn):
    p_rows, w = xg.shape
    n_exp, d, f2 = w1.shape
    n_vis = sched[0].shape[0]
    n_chunks = f2 // tn
    grid_spec = pltpu.PrefetchScalarGridSpec(
        num_scalar_prefetch=len(sched),
        grid=(n_chunks, n_vis),
        in_specs=[
            pl.BlockSpec((tr, w), lambda c, v, vt, *_: (vt[v], 0)),
            pl.BlockSpec(memory_space=pl.ANY),
            pl.BlockSpec((None, 1, tn), lambda c, v, vt, ve, *_: (ve[v], 0, c)),
            pl.BlockSpec((SEL_W, SEL_W // 2), lambda c, v, *_: (0, 0)),
        ],
        out_specs=pl.BlockSpec((tr, tn // 2), lambda c, v, vt, *_: (vt[v], c)),
        scratch_shapes=[pltpu.VMEM((2, d, tn), F32), pltpu.SemaphoreType.DMA((2,))],
    )
    return pl.pallas_call(
        functools.partial(_up_kernel, n_chunks=n_chunks),
        out_shape=jax.ShapeDtypeStruct((p_rows, f2 // 2), BF16),
        grid_spec=grid_spec,
        compiler_params=_params("arbitrary", "arbitrary"),
        name="expert_up",
    )(*sched, xg, w1, b1.reshape(n_exp, 1, f2), _even_lane_selector())


def _down_kernel(vt, ve, vlo, vhi, run, nxt_e, meta, act_ref, w2_hbm, b2_ref, o_ref, wbuf, wsem, *, n_chunks):
    lo, hi, first = _visit_state(vt, vlo, vhi)
    slot = _resident_weights(ve, run, nxt_e, meta, w2_hbm, wbuf, wsem, n_chunks)

    def sub_block(rows, lo_s, hi_s):
        y = _dot(act_ref[rows, :], wbuf[slot].astype(BF16)) + b2_ref[...]
        _store_rows(o_ref.at[rows], _pack_halves(y), lo_s, hi_s)

    _for_live_sub_blocks(o_ref, lo, hi, first, sub_block)


def _expert_down(sched, act, w2, b2, *, tr, tn):
    p_rows, f = act.shape
    n_exp, _, d = w2.shape
    n_vis = sched[0].shape[0]
    n_chunks = d // tn
    grid_spec = pltpu.PrefetchScalarGridSpec(
        num_scalar_prefetch=len(sched),
        grid=(n_chunks, n_vis),
        in_specs=[
            pl.BlockSpec((tr, f), lambda c, v, vt, *_: (vt[v], 0)),
            pl.BlockSpec(memory_space=pl.ANY),
            pl.BlockSpec((None, 1, tn), lambda c, v, vt, ve, *_: (ve[v], 0, c)),
        ],
        out_specs=pl.BlockSpec((tr, tn // 2), lambda c, v, vt, *_: (vt[v], c)),
        scratch_shapes=[pltpu.VMEM((2, f, tn), F32), pltpu.SemaphoreType.DMA((2,))],
    )
    return pl.pallas_call(
        functools.partial(_down_kernel, n_chunks=n_chunks),
        out_shape=jax.ShapeDtypeStruct((p_rows, d // 2), U32),
        grid_spec=grid_spec,
        compiler_params=_params("arbitrary", "arbitrary"),
        name="expert_down",
    )(*sched, act, w2, b2.reshape(n_exp, 1, d))


def _combine_kernel(dest_ref, h_ref, gate_ref, g_ref, b_ref, y_ref, o_ref, buf_ref, sem,
                    *, alpha, chunk):
    tm = h_ref.shape[0]

    def issue(t, carry):
        for k in range(TOP_K):
            pltpu.make_async_copy(y_ref.at[pl.ds(dest_ref[0, t * TOP_K + k], 1)],
                                  buf_ref.at[k, pl.ds(t, 1)], sem).start()
        return carry

    lax.fori_loop(0, tm, issue, 0)

    def drain(t, carry):
        for k in range(TOP_K):
            pltpu.make_async_copy(y_ref.at[pl.ds(0, 1)], buf_ref.at[k, pl.ds(0, 1)], sem).wait()
        return carry

    lax.fori_loop(0, tm, drain, 0)

    gates = gate_ref[...]
    acc_hi = None
    acc_lo = None
    for k in range(TOP_K):
        hi, lo = _unpack_halves(buf_ref[k])
        gk = gates[:, k:k + 1]
        acc_hi = gk * hi if acc_hi is None else acc_hi + gk * hi
        acc_lo = gk * lo if acc_lo is None else acc_lo + gk * lo
    hw = chunk // 2
    pieces = []
    for c in range(acc_hi.shape[1] // hw):
        pieces.append(acc_hi[:, c * hw:(c + 1) * hw])
        pieces.append(acc_lo[:, c * hw:(c + 1) * hw])
    ffn = jnp.concatenate(pieces, axis=1)
    o_ref[...] = _layer_norm(alpha * h_ref[...] + ffn, g_ref[...], b_ref[...]).astype(o_ref.dtype)


def _combine(dest, h, gates, g, b, y, *, alpha, tm, chunk):
    n, d = h.shape
    dest3 = dest.reshape(n // tm, 1, tm * TOP_K)
    kern = functools.partial(_combine_kernel, alpha=alpha, chunk=chunk)
    return pl.pallas_call(
        kern,
        out_shape=jax.ShapeDtypeStruct((n, d), F32),
        grid=(n // tm,),
        in_specs=[pl.BlockSpec((None, 1, tm * TOP_K), lambda i: (i, 0, 0), memory_space=pltpu.SMEM),
                  pl.BlockSpec((tm, d), lambda i: (i, 0)),
                  pl.BlockSpec((tm, LANES), lambda i: (i, 0)),
                  pl.BlockSpec((1, d), lambda i: (0, 0)),
                  pl.BlockSpec((1, d), lambda i: (0, 0)),
                  pl.BlockSpec(memory_space=pl.ANY)],
        out_specs=pl.BlockSpec((tm, d), lambda i: (i, 0)),
        scratch_shapes=[pltpu.VMEM((TOP_K, tm, d // 2), U32), pltpu.SemaphoreType.DMA(())],
        compiler_params=_params("arbitrary"),
        name="combine_ln",
    )(dest3, h, gates, g, b, y)


def _tiles(n_tokens, d_model, d_expert):
    return dict(
        cast_tm=min(512, n_tokens),
        proj=dict(tm=min(1024, n_tokens), tn=512),
        memkv=dict(tm=512, tn=512),
        merge=dict(tm=min(512, n_tokens), tn=256),
        out=dict(tm=min(1024, n_tokens), tn=512),
        ln_tm=min(256, n_tokens),
        dispatch_tm=min(256, n_tokens),
        moe_tr=512,
        up_tn=min(1024, 2 * d_expert),
        down_tn=min(2048, d_model),
        combine_tm=min(128, n_tokens),
    )


def _layer(h, mem2, lw, *, batch, seq, mem_len, lambda_init, alpha):
    n, d = h.shape
    t = _tiles(n, d, lw["w_mlp2"].shape[1])
    a_width = A_HEADS * A_HEAD_DIM
    b_width = B_HEADS * 2 * B_HEAD_DIM
    col = dict(a_q=0, a_k=a_width, a_v=2 * a_width, b_q=3 * a_width, b_k=3 * a_width + b_width,
               b_v=3 * a_width + 2 * b_width, c_q=3 * a_width + 3 * b_width)

    xb = _cast_bf16(h, t["cast_tm"])
    proj = _matmul(xb, lw["w_in"], out_dtype=BF16, name="in_proj", **t["proj"])
    ckv = _matmul(mem2, lw["w_mem_kv"], out_dtype=BF16, name="mem_kv",
                  tm=min(t["memkv"]["tm"], mem2.shape[0]), tn=t["memkv"]["tn"])

    ya = _attention_a(proj, _band_bias_base(lw["rel_bias"]), batch=batch, seq=seq,
                      heads=A_HEADS, dh=A_HEAD_DIM, col_q=col["a_q"], col_k=col["a_k"], col_v=col["a_v"])
    cos, sin_signed = _rope_tables(seq, B_HEAD_DIM)
    lam_vecs = jnp.stack([lw["lambda_q1"], lw["lambda_k1"], lw["lambda_q2"], lw["lambda_k2"]]).astype(F32)
    yb = _attention_b(proj, cos, sin_signed, lam_vecs, lw["diff_norm_g"].reshape(1, -1),
                      batch=batch, seq=seq, heads=B_HEADS, dh=B_HEAD_DIM,
                      col_q=col["b_q"], col_k=col["b_k"], col_v=col["b_v"], lambda_init=lambda_init)
    yc = _attention_c(proj, ckv, batch=batch, seq=seq, mem_len=mem_len, heads=C_HEADS,
                      dh=C_HEAD_DIM, col_q=col["c_q"])

    merged = _gated_merge(xb, ya, yb, yc, lw["w_gates"], lw["b_gates"].reshape(1, -1),
                          lw["w_branch_a"], lw["w_branch_b"], lw["w_branch_c"], **t["merge"])
    mix = _matmul(merged, lw["w_o"], out_dtype=F32, name="out_proj", **t["out"])

    n_exp = lw["w_router"].shape[1]
    wr = jnp.pad(lw["w_router"], ((0, 0), (0, LANES - n_exp)))
    wr_hi = wr.astype(BF16)
    wr_lo = (wr - wr_hi.astype(F32)).astype(BF16)
    br = jnp.pad(lw["b_router"], (0, LANES - n_exp)).reshape(1, LANES)
    h1, h1_packed, top_idx, gates, rank, cnt = _ln_router(
        h, mix, lw["ln1_g"].reshape(1, -1), lw["ln1_b"].reshape(1, -1),
        jnp.concatenate([wr_hi, wr_lo], axis=1), br, alpha=alpha, n_exp=n_exp, tm=t["ln_tm"])

    counts = cnt[0, :n_exp].astype(I32)
    gstart = _cumsum_small(counts) - counts
    dest = _lookup(gstart, top_idx[:, :TOP_K]) + rank[:, :TOP_K]
    sched = _visit_schedule(counts, n * TOP_K, t["moe_tr"])

    xg = _dispatch(dest, h1_packed, tm=t["dispatch_tm"])
    act = _expert_up(sched, xg, lw["w_mlp1"], lw["b_mlp1"], tr=t["moe_tr"], tn=t["up_tn"])
    y = _expert_down(sched, act, lw["w_mlp2"], lw["b_mlp2"], tr=t["moe_tr"], tn=t["down_tn"])
    return _combine(dest, h1, gates, lw["ln2_g"].reshape(1, -1), lw["ln2_b"].reshape(1, -1), y,
                    alpha=alpha, tm=t["combine_tm"], chunk=t["down_tn"])


def kernel(x, mem, w_in, w_mem_kv, rel_bias, lambda_q1, lambda_k1, lambda_q2, lambda_k2, diff_norm_g,
           w_branch_a, w_branch_b, w_branch_c, w_gates, b_gates, w_o, ln1_g, ln1_b, w_router, b_router,
           w_mlp1, b_mlp1, w_mlp2, b_mlp2, ln2_g, ln2_b):
    batch, seq, d = x.shape
    mem_len = mem.shape[1]
    depth = w_in.shape[0]
    alpha = (2 * depth) ** 0.25
    stacked = dict(w_in=w_in, w_mem_kv=w_mem_kv, rel_bias=rel_bias, lambda_q1=lambda_q1,
                   lambda_k1=lambda_k1, lambda_q2=lambda_q2, lambda_k2=lambda_k2, diff_norm_g=diff_norm_g,
                   w_branch_a=w_branch_a, w_branch_b=w_branch_b, w_branch_c=w_branch_c, w_gates=w_gates,
                   b_gates=b_gates, w_o=w_o, ln1_g=ln1_g, ln1_b=ln1_b, w_router=w_router,
                   b_router=b_router, w_mlp1=w_mlp1, b_mlp1=b_mlp1, w_mlp2=w_mlp2, b_mlp2=b_mlp2,
                   ln2_g=ln2_g, ln2_b=ln2_b)
    h = x.reshape(batch * seq, d)
    mem2 = mem.reshape(batch * mem_len, d)
    for l in range(depth):
        lw = {name: w[l] for name, w in stacked.items()}
        lambda_init = 0.8 - 0.6 * math.exp(-0.3 * l)
        h = _layer(h, mem2, lw, batch=batch, seq=seq, mem_len=mem_len, lambda_init=lambda_init, alpha=alpha)
    return h.reshape(batch, seq, d)
```

```python
import functools
import math

import jax
import jax.numpy as jnp
from jax import lax
from jax.experimental import pallas as pl
from jax.experimental.pallas import tpu as pltpu

F32 = jnp.float32
BF16 = jnp.bfloat16
U32 = jnp.uint32
I32 = jnp.int32

CHUNK = 64
LEFT_CHUNKS = 8
MAX_REL = 128
A_HEADS = 16
A_HEAD_DIM = 128
B_HEADS = 4
B_HEAD_DIM = 128
C_HEADS = 4
C_HEAD_DIM = 256
N_BRANCHES = 3
ROPE_THETA = 10000.0
TOP_K = 4
SWIGLU_LIMIT = 7.0
SWIGLU_ALPHA = 1.702
LN_EPS = 1e-5
RMS_EPS = 1e-5
MASK_VALUE = -1e30

V7X_VMEM_BYTES = 64 * 1024 * 1024
V7X_VMEM_LIMIT = V7X_VMEM_BYTES - 8 * 1024 * 1024
LANES = 128

NT_DIMS = (((1,), (1,)), ((), ()))


def _params(*semantics):
    return pltpu.CompilerParams(dimension_semantics=semantics,
                                vmem_limit_bytes=V7X_VMEM_LIMIT)


def _dot(a, b):
    return jnp.dot(a, b, preferred_element_type=F32)


def _pack_halves(x):
    w = x.shape[1] // 2
    hi = lax.bitcast_convert_type(x[:, :w].astype(jnp.bfloat16).astype(F32), U32)
    lo = lax.bitcast_convert_type(x[:, w:].astype(jnp.bfloat16).astype(F32), U32)
    return hi | (lo >> 16)


def _unpack_halves(p):
    hi = lax.bitcast_convert_type(p & jnp.uint32(0xFFFF0000), F32)
    lo = lax.bitcast_convert_type(p << 16, F32)
    return hi, lo


def _cast_kernel(x_ref, o_ref):
    o_ref[...] = x_ref[...].astype(o_ref.dtype)


def _cast_bf16(x, tm):
    m, d = x.shape
    return pl.pallas_call(
        _cast_kernel,
        out_shape=jax.ShapeDtypeStruct((m, d), BF16),
        grid=(m // tm,),
        in_specs=[pl.BlockSpec((tm, d), lambda i: (i, 0))],
        out_specs=pl.BlockSpec((tm, d), lambda i: (i, 0)),
        compiler_params=_params("parallel"),
        name="cast_bf16",
    )(x)


def _mm_kernel(a_ref, w_ref, o_ref):
    a = a_ref[...].astype(BF16)
    o_ref[...] = _dot(a, w_ref[...].astype(BF16)).astype(o_ref.dtype)


def _matmul(a, w, *, tm, tn, out_dtype, name):
    m, k = a.shape
    n = w.shape[1]
    return pl.pallas_call(
        _mm_kernel,
        out_shape=jax.ShapeDtypeStruct((m, n), out_dtype),
        grid=(n // tn, m // tm),
        in_specs=[pl.BlockSpec((tm, k), lambda j, i: (i, 0)),
                  pl.BlockSpec((k, tn), lambda j, i: (0, j))],
        out_specs=pl.BlockSpec((tm, tn), lambda j, i: (i, j)),
        compiler_params=_params("parallel", "parallel"),
        name=name,
    )(a, w)


A_QCHUNKS = 4
A_TQ = A_QCHUNKS * CHUNK
A_WIN = (LEFT_CHUNKS + A_QCHUNKS) * CHUNK
A_VARIANTS = LEFT_CHUNKS * CHUNK // A_TQ + 1


A_BASE_W = A_WIN + A_TQ


def _band_bias_base(rel_bias):
    reach = A_WIN
    ext =jnp.pad(rel_bias.astype(F32), ((0, 0), (reach - MAX_REL, reach - MAX_REL)), mode="edge")
    rev = ext[:, ::-1]
    rows = []
    for v in range(A_VARIANTS):
        c = rev[:, reach - A_TQ * v - A_TQ: reach - A_TQ * v + A_WIN]
        rows.append(jnp.concatenate([c[:, A_TQ:], c[:, :A_TQ]], axis=1))
    return jnp.stack(rows)[:, :, None, :]


def _attn_a_kernel(q_ref, k_ref, v_ref, base_ref, o_ref, tb_ref, *, heads, dh, scale):
    i = pl.program_id(2)

    @pl.when(jnp.logical_and(pl.program_id(1) == 0, i == 0))
    def _():
        r = lax.broadcasted_iota(I32, (A_TQ, A_WIN), 0)
        j = lax.broadcasted_iota(I32, (A_TQ, A_WIN), 1)
        for var in range(A_VARIANTS):
            cdiff = (A_TQ * var + r) // CHUNK - j // CHUNK
            valid = (cdiff >= 0) & (cdiff <= LEFT_CHUNKS)
            for h in range(heads):
                rows = jnp.broadcast_to(base_ref[var, h], (A_TQ, A_BASE_W))
                toeplitz = pltpu.roll(rows, 0, 1, stride=1, stride_axis=0)[:, :A_WIN]
                tb_ref[var, h] = jnp.where(valid, toeplitz, MASK_VALUE)

    var = jnp.minimum(i, A_VARIANTS - 1)
    start = pl.multiple_of(jnp.maximum(i - (A_VARIANTS - 1), 0) * A_TQ, A_TQ)
    for h in range(heads):
        cs = slice(h * dh, (h + 1) * dh)
        q = (q_ref[:, cs].astype(F32) * scale).astype(BF16)
        k = k_ref[pl.ds(start, A_WIN), cs]
        v = v_ref[pl.ds(start, A_WIN), cs]
        s = lax.dot_general(q, k, NT_DIMS, preferred_element_type=F32) + tb_ref[var, h]
        m = jnp.max(s, axis=-1, keepdims=True)
        p = jnp.exp(s - m)
        l = jnp.sum(p, axis=-1, keepdims=True)
        o = _dot(p.astype(BF16), v)
        o_ref[:, cs] = (o / l).astype(o_ref.dtype)


def _attention_a(proj, base, *, batch, seq, heads, dh, col_q, col_k, col_v, heads_per_step=8):
    n = proj.shape[0]
    gw = heads_per_step * dh
    n_groups = heads // heads_per_step
    n_qb = seq // A_TQ
    kern = functools.partial(_attn_a_kernel, heads=heads_per_step, dh=dh, scale=dh ** -0.5)
    return pl.pallas_call(
        kern,
        out_shape=jax.ShapeDtypeStruct((n, heads * dh), BF16),
        grid=(n_groups, batch, n_qb),
        in_specs=[
            pl.BlockSpec((A_TQ, gw), lambda g, b, i: (b * n_qb + i, col_q // gw + g)),
            pl.BlockSpec((seq, gw), lambda g, b, i: (b, col_k // gw + g)),
            pl.BlockSpec((seq, gw), lambda g, b, i: (b, col_v // gw + g)),
            pl.BlockSpec((A_VARIANTS, heads_per_step, 1, A_BASE_W), lambda g, b, i: (0, g, 0, 0)),
        ],
        out_specs=pl.BlockSpec((A_TQ, gw), lambda g, b, i: (b * n_qb + i, g)),
        scratch_shapes=[pltpu.VMEM((A_VARIANTS, heads_per_step, A_TQ, A_WIN), F32)],
        compiler_params=_params("parallel", "arbitrary", "arbitrary"),
        name="attn_band",
    )(proj, proj, proj, base)


B_TQ = 256


def _rope_tables(seq, dim):
    inv = 1.0 / (ROPE_THETA ** (jnp.arange(0, dim, 2, dtype=F32) / dim))
    ang = jnp.arange(seq, dtype=F32)[:, None] * inv[None, :]
    ang = jnp.concatenate([ang, ang], -1)
    sign = jnp.where(jnp.arange(dim) < dim // 2, -1.0, 1.0).astype(F32)
    return jnp.cos(ang), jnp.sin(ang) * sign[None, :]


def _rope(x, cos, sin_signed):
    return x * cos + pltpu.roll(x, x.shape[1] // 2, 1) * sin_signed


def _attn_b_kernel(q_ref, k_ref, v_ref, cos_ref, sin_ref, lam_ref, g_ref, o_ref, krot_ref,
                   *, heads, dh, scale, lambda_init):
    qi = pl.program_id(2)
    seq = k_ref.shape[0]
    hw = 2 * dh

    @pl.when(qi == 0)
    def _():
        for hm in range(2 * heads):
            kf = k_ref[:, hm * dh:(hm + 1) * dh].astype(F32)
            krot_ref[hm] = _rope(kf, cos_ref[...], sin_ref[...]).astype(BF16)

    lv = lam_ref[...]
    lam = (jnp.exp(jnp.sum(lv[0:1] * lv[1:2], axis=-1, keepdims=True))
           - jnp.exp(jnp.sum(lv[2:3] * lv[3:4], axis=-1, keepdims=True)) + lambda_init)

    def block(blk):
        row0 = blk * B_TQ
        kl = row0 + B_TQ
        cos_q = cos_ref[row0:kl, :]
        sin_q = sin_ref[row0:kl, :]
        q_chunk = lax.broadcasted_iota(I32, (B_TQ, B_TQ), 0) // CHUNK
        k_chunk = lax.broadcasted_iota(I32, (B_TQ, B_TQ), 1) // CHUNK
        allowed = k_chunk <= q_chunk
        for hh in range(heads):
            probs = []
            for m in range(2):
                hm = 2 * hh + m
                qf = q_ref[:, hm * dh:(hm + 1) * dh].astype(F32)
                qr = (_rope(qf, cos_q, sin_q) * scale).astype(BF16)
                s = lax.dot_general(qr, krot_ref[hm, :kl, :], NT_DIMS, preferred_element_type=F32)
                diag = jnp.where(allowed, s[:, row0:], MASK_VALUE)
                s = diag if row0 == 0 else jnp.concatenate([s[:, :row0], diag], axis=1)
                e = jnp.exp(s - jnp.max(s, axis=-1, keepdims=True))
                probs.append(e / jnp.sum(e, axis=-1, keepdims=True))
            w = (probs[0] - lam * probs[1]).astype(BF16)
            o = _dot(w, v_ref[:kl, hh * hw:(hh + 1) * hw])
            ms = jnp.mean(o * o, axis=-1, keepdims=True)
            y = o * lax.rsqrt(ms + RMS_EPS) * g_ref[...] * (1.0 - lambda_init)
            o_ref[:, hh * hw:(hh + 1) * hw] = y.astype(o_ref.dtype)

    for blk in range(seq // B_TQ):
        pl.when(qi == blk)(functools.partial(block, blk))


def _attention_b(proj, cos, sin_signed, lam_vecs, norm_g, *, batch, seq, heads, dh,
                 col_q, col_k, col_v, lambda_init, heads_per_step=2):
    n = proj.shape[0]
    hw = 2 * dh
    gw = heads_per_step * hw
    n_qb = seq // B_TQ
    kern = functools.partial(_attn_b_kernel, heads=heads_per_step, dh=dh, scale=dh ** -0.5,
                             lambda_init=lambda_init)
    return pl.pallas_call(
        kern,
        out_shape=jax.ShapeDtypeStruct((n, heads * hw), BF16),
        grid=(batch, heads // heads_per_step, n_qb),
        in_specs=[
            pl.BlockSpec((B_TQ, gw), lambda b, h, i: (b * n_qb + i, col_q // gw + h)),
            pl.BlockSpec((seq, gw), lambda b, h, i: (b, col_k // gw + h)),
            pl.BlockSpec((seq, gw), lambda b, h, i: (b, col_v // gw + h)),
            pl.BlockSpec((seq, dh), lambda b, h, i: (0, 0)),
            pl.BlockSpec((seq, dh), lambda b, h, i: (0, 0)),
            pl.BlockSpec((4, dh), lambda b, h, i: (0, 0)),
            pl.BlockSpec((1, hw), lambda b, h, i: (0, 0)),
        ],
        out_specs=pl.BlockSpec((B_TQ, gw), lambda b, h, i: (b * n_qb + i, h)),
        scratch_shapes=[pltpu.VMEM((2 * heads_per_step, seq, dh), BF16)],
        compiler_params=_params("parallel", "parallel", "arbitrary"),
        name="attn_diff",
    )(proj, proj, proj, cos, sin_signed, lam_vecs, norm_g)


C_TQ = 1024


def _attn_c_kernel(q_ref, k_ref, v_ref, o_ref, *, scale):
    s = lax.dot_general(q_ref[...], k_ref[...], NT_DIMS, preferred_element_type=F32) * scale
    e = jnp.exp(s - jnp.max(s, axis=-1, keepdims=True))
    p = (e / jnp.sum(e, axis=-1, keepdims=True)).astype(BF16)
    o_ref[...] = _dot(p, v_ref[...]).astype(o_ref.dtype)


def _attention_c(proj, ckv, *, batch, seq, mem_len, heads, dh, col_q):
    n = proj.shape[0]
    n_qb = seq // C_TQ
    kern = functools.partial(_attn_c_kernel, scale=dh ** -0.5)
    return pl.pallas_call(
        kern,
        out_shape=jax.ShapeDtypeStruct((n, heads * dh), BF16),
        grid=(batch, heads, n_qb),
        in_specs=[
            pl.BlockSpec((C_TQ, dh), lambda b, h, i: (b * n_qb + i, col_q // dh + h)),
            pl.BlockSpec((mem_len, dh), lambda b, h, i: (b, h)),
            pl.BlockSpec((mem_len, dh), lambda b, h, i: (b, heads + h)),
        ],
        out_specs=pl.BlockSpec((C_TQ, dh), lambda b, h, i: (b * n_qb + i, h)),
        compiler_params=_params("parallel", "parallel", "parallel"),
        name="attn_mem",
    )(proj, ckv, ckv)


def _merge_kernel(x_ref, ya_ref, yb_ref, yc_ref, wga_ref, wgb_ref, wgc_ref,
                  bga_ref, bgb_ref, bgc_ref, pa_ref, pb_ref, pc_ref, o_ref):
    x = x_ref[...]
    acc = None
    for wg, bg, y, p in ((wga_ref, bga_ref, ya_ref, pa_ref),
                         (wgb_ref, bgb_ref, yb_ref, pb_ref),
                         (wgc_ref, bgc_ref, yc_ref, pc_ref)):
        gate = jax.nn.sigmoid(_dot(x, wg[...].astype(BF16)) + bg[...])
        term = gate * _dot(y[...], p[...].astype(BF16))
        acc = term if acc is None else acc + term
    o_ref[...] = acc.astype(o_ref.dtype)


def _gated_merge(xb, ya, yb, yc, w_gates, b_gates, pa, pb, pc, *, tm, tn):
    n, d = xb.shape
    nj = d // tn
    row = lambda width: pl.BlockSpec((tm, width), lambda j, i: (i, 0))
    gate_w = lambda br: pl.BlockSpec((d, tn), lambda j, i, br=br: (0, br * nj + j))
    gate_b = lambda br: pl.BlockSpec((1, tn), lambda j, i, br=br: (0, br * nj + j))
    branch_w = lambda width: pl.BlockSpec((width, tn), lambda j, i: (0, j))
    return pl.pallas_call(
        _merge_kernel,
        out_shape=jax.ShapeDtypeStruct((n, d), BF16),
        grid=(nj, n // tm),
        in_specs=[row(d), row(ya.shape[1]), row(yb.shape[1]), row(yc.shape[1]),
                  gate_w(0), gate_w(1), gate_w(2), gate_b(0), gate_b(1), gate_b(2),
                  branch_w(pa.shape[0]), branch_w(pb.shape[0]), branch_w(pc.shape[0])],
        out_specs=pl.BlockSpec((tm, tn), lambda j, i: (i, j)),
        compiler_params=_params("parallel", "parallel"),
        name="gated_merge",
    )(xb, ya, yb, yc, w_gates, w_gates, w_gates, b_gates, b_gates, b_gates, pa, pb, pc)


def _layer_norm(z, g, b):
    mu = jnp.mean(z, axis=-1, keepdims=True)
    zc = z - mu
    var = jnp.mean(zc * zc, axis=-1, keepdims=True)
    return zc * lax.rsqrt(var + LN_EPS) * g + b


def _ln_router_kernel(x_ref, m_ref, g_ref, b_ref, wr_ref, br_ref,
                      h_ref, hp_ref, idx_ref, gate_ref, rank_ref, cnt_ref, carry_ref,
                      *, alpha, n_exp):
    @pl.when(pl.program_id(0) == 0)
    def _():
        carry_ref[...] = jnp.zeros_like(carry_ref)

    tm = x_ref.shape[0]
    h = _layer_norm(alpha * x_ref[...] + m_ref[...], g_ref[...], b_ref[...])
    h_ref[...] = h
    hp_ref[...] = _pack_halves(h)

    h_hi = h.astype(BF16)
    h_lo = (h - h_hi.astype(F32)).astype(BF16)
    w = wr_ref[...]
    r1 = _dot(h_hi, w)
    logits = r1[:, :LANES] + r1[:, LANES:] + _dot(h_lo, w[:, :LANES]) + br_ref[...]

    lane = lax.broadcasted_iota(I32, (tm, LANES), 1)
    lane_f = lane.astype(F32)
    cur = jnp.where(lane < n_exp, logits, -jnp.inf)
    vals, idxs = [], []
    for _ in range(TOP_K):
        mx = jnp.max(cur, axis=-1, keepdims=True)
        ix = jnp.min(jnp.where(cur == mx, lane_f, float(LANES)), axis=-1, keepdims=True).astype(I32)
        vals.append(mx)
        idxs.append(ix)
        cur = jnp.where(lane == ix, -jnp.inf, cur)
    exps = [jnp.exp(v - vals[0]) for v in vals]
    den = exps[0]
    for e in exps[1:]:
        den = den + e

    tri = (lax.broadcasted_iota(I32, (tm, tm), 0) > lax.broadcasted_iota(I32, (tm, tm), 1)).astype(BF16)
    carry = carry_ref[...]
    idx_out = jnp.zeros((tm, LANES), I32)
    gate_out = jnp.zeros((tm, LANES), F32)
    rank_out = jnp.zeros((tm, LANES), I32)
    for k in range(TOP_K):
        onehot = (lane == idxs[k]).astype(F32)
        before = _dot(tri, onehot.astype(BF16)) + carry
        rank = jnp.sum(onehot * before, axis=-1, keepdims=True)
        carry = carry + jnp.sum(onehot, axis=0, keepdims=True)
        idx_out = jnp.where(lane == k, idxs[k], idx_out)
        gate_out = jnp.where(lane == k, exps[k] / den, gate_out)
        rank_out = jnp.where(lane == k, rank.astype(I32), rank_out)
    carry_ref[...] = carry
    idx_ref[...] = idx_out
    gate_ref[...] = gate_out
    rank_ref[...] = rank_out
    cnt_ref[...] = carry


def _ln_router(x, m, g, b, wr_split, br_pad, *, alpha, n_exp, tm):
    n, d = x.shape
    row = pl.BlockSpec((tm, d), lambda i: (i, 0))
    vec = pl.BlockSpec((1, d), lambda i: (0, 0))
    small = pl.BlockSpec((tm, LANES), lambda i: (i, 0))
    kern = functools.partial(_ln_router_kernel, alpha=alpha, n_exp=n_exp)
    return pl.pallas_call(
        kern,
        out_shape=(jax.ShapeDtypeStruct((n, d), F32),
                   jax.ShapeDtypeStruct((n, d // 2), U32),
                   jax.ShapeDtypeStruct((n, LANES), I32),
                   jax.ShapeDtypeStruct((n, LANES), F32),
                   jax.ShapeDtypeStruct((n, LANES), I32),
                   jax.ShapeDtypeStruct((1, LANES), F32)),
        grid=(n // tm,),
        in_specs=[row, row, vec, vec,
                  pl.BlockSpec((d, 2 * LANES), lambda i: (0, 0)),
                  pl.BlockSpec((1, LANES), lambda i: (0, 0))],
        out_specs=(row, pl.BlockSpec((tm, d // 2), lambda i: (i, 0)), small, small, small,
                   pl.BlockSpec((1, LANES), lambda i: (0, 0))),
        scratch_shapes=[pltpu.VMEM((1, LANES), F32)],
        compiler_params=_params("arbitrary"),
        name="ln_router",
    )(x, m, g, b, wr_split, br_pad)


def _row_copy(src, src_row, dst, dst_row, sem):
    return pltpu.make_async_copy(src.at[pl.ds(src_row, 1)], dst.at[pl.ds(dst_row, 1)], sem)


def _dispatch_kernel(dest_ref, hp_ref, xg_ref, sem):
    tm = hp_ref.shape[0]

    def issue(t, carry):
        for k in range(TOP_K):
            _row_copy(hp_ref, t, xg_ref, dest_ref[0, t * TOP_K + k], sem).start()
        return carry

    lax.fori_loop(0, tm, issue, 0)

    def drain(t, carry):
        for k in range(TOP_K):
            _row_copy(hp_ref, 0, xg_ref, 0, sem).wait()
        return carry

    lax.fori_loop(0, tm, drain, 0)


def _dispatch(dest, hp, *, tm):
    n, w = hp.shape
    dest3 = dest.reshape(n // tm, 1, tm * TOP_K)
    return pl.pallas_call(
        _dispatch_kernel,
        out_shape=jax.ShapeDtypeStruct((n * TOP_K, w), U32),
        grid=(n // tm,),
        in_specs=[pl.BlockSpec((None, 1, tm * TOP_K), lambda i: (i, 0, 0), memory_space=pltpu.SMEM),
                  pl.BlockSpec((tm, w), lambda i: (i, 0))],
        out_specs=pl.BlockSpec(memory_space=pl.ANY),
        scratch_shapes=[pltpu.SemaphoreType.DMA(())],
        compiler_params=_params("arbitrary"),
        name="dispatch",
    )(dest3, hp)


def _cumsum_small(x):
    n = x.shape[0]
    keep = jnp.arange(n)[:, None] >= jnp.arange(n)[None, :]
    return jnp.sum(jnp.where(keep, x[None, :], 0), axis=1).astype(x.dtype)


def _lookup(table, idx):
    hit = idx[..., None] == jnp.arange(table.shape[0], dtype=idx.dtype)
    return jnp.sum(jnp.where(hit, table, 0), axis=-1).astype(table.dtype)


def _visit_schedule(counts, n_rows, tr):
    n_exp = counts.shape[0]
    n_tiles = n_rows // tr
    n_vis = n_tiles + n_exp
    gend = _cumsum_small(counts)
    gstart = gend - counts
    first_tile = gstart // tr
    last_tile = jnp.maximum(gend - 1, 0) // tr
    nvis = jnp.where(counts > 0, last_tile - first_tile + 1, 0)
    vend = _cumsum_small(nvis)
    vstart = vend - nvis
    total = vend[-1]
    v = jnp.arange(n_vis, dtype=I32)
    vc = jnp.minimum(v, total - 1)
    e_v = jnp.minimum(jnp.sum((vend[None, :] <= vc[:, None]).astype(I32), axis=1), n_exp - 1)
    tile_v = _lookup(first_tile, e_v) + (vc - _lookup(vstart, e_v))
    lo = jnp.clip(_lookup(gstart, e_v) - tile_v * tr, 0, tr)
    hi = jnp.clip(_lookup(gend, e_v) - tile_v * tr, 0, tr)
    live = v < total
    lo = jnp.where(live, lo, 0).astype(I32)
    hi = jnp.where(live, hi, 0).astype(I32)
    changed = jnp.concatenate([jnp.ones((1,), I32), (e_v[1:] != e_v[:-1]).astype(I32)])
    run = _cumsum_small(changed) - 1
    later_other = (v[None, :] > v[:, None]) & (e_v[None, :] != e_v[:, None])
    nxt_pos = jnp.min(jnp.where(later_other, v[None, :], n_vis), axis=1)
    nxt_e = jnp.where(nxt_pos < n_vis, _lookup(e_v, jnp.minimum(nxt_pos, n_vis - 1)), -1).astype(I32)
    meta = jnp.stack([run[-1] + 1, e_v[0]]).astype(I32)
    return tile_v.astype(I32), e_v.astype(I32), lo, hi, run.astype(I32), nxt_e, meta


def _visit_state(vt, vlo, vhi):
    v = pl.program_id(1)
    lo = vlo[v]
    hi = vhi[v]
    first = jnp.logical_or(v == 0, vt[v] != vt[jnp.maximum(v - 1, 0)])
    return lo, hi, first


def _resident_weights(ve, run, nxt_e, meta, w_hbm, wbuf, wsem, n_chunks):
    c = pl.program_id(0)
    v = pl.program_id(1)
    tn = wbuf.shape[2]
    slot = lax.rem(c * meta[0] + run[v], 2)
    new_run = jnp.logical_or(v == 0, ve[v] != ve[jnp.maximum(v - 1, 0)])

    def fetch(e, chunk, s):
        for cc in range(n_chunks):
            @pl.when(chunk == cc)
            def _(cc=cc):
                pltpu.make_async_copy(w_hbm.at[e, :, pl.ds(cc * tn, tn)], wbuf.at[s], wsem.at[s]).start()

    @pl.when(jnp.logical_and(c == 0, v == 0))
    def _():
        fetch(ve[0], c, slot)

    @pl.when(new_run)
    def _():
        pltpu.make_async_copy(w_hbm.at[0, :, pl.ds(0, tn)], wbuf.at[slot], wsem.at[slot]).wait()
        more_here = nxt_e[v] >= 0

        @pl.when(more_here)
        def _():
            fetch(nxt_e[v], c, 1 - slot)

        @pl.when(jnp.logical_and(jnp.logical_not(more_here), c + 1 < n_chunks))
        def _():
            fetch(meta[1], c + 1, 1 - slot)

    return slot


def _store_rows(o_ref, val, lo, hi):
    rows = lax.broadcasted_iota(I32, (o_ref.shape[0], 1), 0)
    mine = (rows >= lo) & (rows < hi)
    o_ref[...] = jnp.where(mine, val, o_ref[...])


MOE_SUB = 256


def _for_live_sub_blocks(o_ref, lo, hi, first, body):
    @pl.when(jnp.logical_and(first, hi > lo))
    def _():
        o_ref[...] = jnp.zeros(o_ref.shape, o_ref.dtype)

    subs = []
    for sb in range(o_ref.shape[0] // MOE_SUB):
        r0 = sb * MOE_SUB
        lo_s = jnp.clip(lo - r0, 0, MOE_SUB)
        hi_s = jnp.clip(hi - r0, 0, MOE_SUB)
        subs.append((pl.ds(r0, MOE_SUB), lo_s, hi_s, hi_s > lo_s))
    all_live = functools.reduce(jnp.logical_and, [live for _, _, _, live in subs])

    @pl.when(all_live)
    def _():
        for rows, lo_s, hi_s, _ in subs:
            body(rows, lo_s, hi_s)

    for rows, lo_s, hi_s, live in subs:
        @pl.when(jnp.logical_and(live, jnp.logical_not(all_live)))
        def _(rows=rows, lo_s=lo_s, hi_s=hi_s):
            body(rows, lo_s, hi_s)


SEL_W = 512


def _even_lane_selector():
    r = jnp.arange(SEL_W)[:, None]
    c = jnp.arange(SEL_W // 2)[None, :]
    return (r == 2 * c).astype(BF16)


def _up_kernel(vt, ve, vlo, vhi, run, nxt_e, meta, xg_ref, w1_hbm, b1_ref, sel_ref, o_ref, wbuf, wsem,
               *, n_chunks):
    lo, hi, first = _visit_state(vt, vlo, vhi)
    slot = _resident_weights(ve, run, nxt_e, meta, w1_hbm, wbuf, wsem, n_chunks)
    half = wbuf.shape[1] // 2
    tn = wbuf.shape[2]

    def sub_block(rows, lo_s, hi_s):
        xa, xb = _unpack_halves(xg_ref[rows, :])
        h = (_dot(xa.astype(BF16), wbuf[slot, :half, :].astype(BF16))
             + _dot(xb.astype(BF16), wbuf[slot, half:, :].astype(BF16)) + b1_ref[...])
        glu = jnp.minimum(h, SWIGLU_LIMIT)
        lin = jnp.clip(h, -SWIGLU_LIMIT, SWIGLU_LIMIT) + 1.0
        gact = glu * jax.nn.sigmoid(SWIGLU_ALPHA * glu)
        parts = []
        for c in range(tn // LANES):
            cs = slice(c * LANES, (c + 1) * LANES)
            parts.append(gact[:, cs] * pltpu.roll(lin[:, cs], LANES - 1, 1))
        inter = jnp.concatenate(parts, axis=1).astype(BF16)
        acts = [_dot(inter[:, s * SEL_W:(s + 1) * SEL_W], sel_ref[...]) for s in range(tn // SEL_W)]
        act = jnp.concatenate(acts, axis=1).astype(o_ref.dtype)
        _store_rows(o_ref.at[rows], act, lo_s, hi_s)

    _for_live_sub_blocks(o_ref, lo, hi, first, sub_block)


def _expert_up(sched, xg, w1, b1, *, tr, tn):
    p_rows, w = xg.shape
    n_exp, d, f2 = w1.shape
    n_vis = sched[0].shape[0]
    n_chunks = f2 // tn
    grid_spec = pltpu.PrefetchScalarGridSpec(
        num_scalar_prefetch=len(sched),
        grid=(n_chunks, n_vis),
        in_specs=[
            pl.BlockSpec((tr, w), lambda c, v, vt, *_: (vt[v], 0)),
            pl.BlockSpec(memory_space=pl.ANY),
            pl.BlockSpec((None, 1, tn), lambda c, v, vt, ve, *_: (ve[v], 0, c)),
            pl.BlockSpec((SEL_W, SEL_W // 2), lambda c, v, *_: (0, 0)),
        ],
        out_specs=pl.BlockSpec((tr, tn // 2), lambda c, v, vt, *_: (vt[v], c)),
        scratch_shapes=[pltpu.VMEM((2, d, tn), F32), pltpu.SemaphoreType.DMA((2,))],
    )
    return pl.pallas_call(
        functools.partial(_up_kernel, n_chunks=n_chunks),
        out_shape=jax.ShapeDtypeStruct((p_rows, f2 // 2), BF16),
        grid_spec=grid_spec,
        compiler_params=_params("arbitrary", "arbitrary"),
        name="expert_up",
    )(*sched, xg, w1, b1.reshape(n_exp, 1, f2), _even_lane_selector())


def _down_kernel(vt, ve, vlo, vhi, run, nxt_e, meta, act_ref, w2_hbm, b2_ref, o_ref, wbuf, wsem, *, n_chunks):
    lo, hi, first = _visit_state(vt, vlo, vhi)
    slot = _resident_weights(ve, run, nxt_e, meta, w2_hbm, wbuf, wsem, n_chunks)

    def sub_block(rows, lo_s, hi_s):
        y = _dot(act_ref[rows, :], wbuf[slot].astype(BF16)) + b2_ref[...]
        _store_rows(o_ref.at[rows], _pack_halves(y), lo_s, hi_s)

    _for_live_sub_blocks(o_ref, lo, hi, first, sub_block)


def _expert_down(sched, act, w2, b2, *, tr, tn):
    p_rows, f = act.shape
    n_exp, _, d = w2.shape
    n_vis = sched[0].shape[0]
    n_chunks = d // tn
    grid_spec = pltpu.PrefetchScalarGridSpec(
        num_scalar_prefetch=len(sched),
        grid=(n_chunks, n_vis),
        in_specs=[
            pl.BlockSpec((tr, f), lambda c, v, vt, *_: (vt[v], 0)),
            pl.BlockSpec(memory_space=pl.ANY),
            pl.BlockSpec((None, 1, tn), lambda c, v, vt, ve, *_: (ve[v], 0, c)),
        ],
        out_specs=pl.BlockSpec((tr, tn // 2), lambda c, v, vt, *_: (vt[v], c)),
        scratch_shapes=[pltpu.VMEM((2, f, tn), F32), pltpu.SemaphoreType.DMA((2,))],
    )
    return pl.pallas_call(
        functools.partial(_down_kernel, n_chunks=n_chunks),
        out_shape=jax.ShapeDtypeStruct((p_rows, d // 2), U32),
        grid_spec=grid_spec,
        compiler_params=_params("arbitrary", "arbitrary"),
        name="expert_down",
    )(*sched, act, w2, b2.reshape(n_exp, 1, d))


def _combine_kernel(dest_ref, h_ref, gate_ref, g_ref, b_ref, y_ref, o_ref, buf_ref, sem,
                    *, alpha, chunk):
    tm = h_ref.shape[0]

    def issue(t, carry):
        for k in range(TOP_K):
            pltpu.make_async_copy(y_ref.at[pl.ds(dest_ref[0, t * TOP_K + k], 1)],
                                  buf_ref.at[k, pl.ds(t, 1)], sem).start()
        return carry

    lax.fori_loop(0, tm, issue, 0)

    def drain(t, carry):
        for k in range(TOP_K):
            pltpu.make_async_copy(y_ref.at[pl.ds(0, 1)], buf_ref.at[k, pl.ds(0, 1)], sem).wait()
        return carry

    lax.fori_loop(0, tm, drain, 0)

    gates = gate_ref[...]
    acc_hi = None
    acc_lo = None
    for k in range(TOP_K):
        hi, lo = _unpack_halves(buf_ref[k])
        gk = gates[:, k:k + 1]
        acc_hi = gk * hi if acc_hi is None else acc_hi + gk * hi
        acc_lo = gk * lo if acc_lo is None else acc_lo + gk * lo
    hw = chunk // 2
    pieces = []
    for c in range(acc_hi.shape[1] // hw):
        pieces.append(acc_hi[:, c * hw:(c + 1) * hw])
        pieces.append(acc_lo[:, c * hw:(c + 1) * hw])
    ffn = jnp.concatenate(pieces, axis=1)
    o_ref[...] = _layer_norm(alpha * h_ref[...] + ffn, g_ref[...], b_ref[...]).astype(o_ref.dtype)


def _combine(dest, h, gates, g, b, y, *, alpha, tm, chunk):
    n, d = h.shape
    dest3 = dest.reshape(n // tm, 1, tm * TOP_K)
    kern = functools.partial(_combine_kernel, alpha=alpha, chunk=chunk)
    return pl.pallas_call(
        kern,
        out_shape=jax.ShapeDtypeStruct((n, d), F32),
        grid=(n // tm,),
        in_specs=[pl.BlockSpec((None, 1, tm * TOP_K), lambda i: (i, 0, 0), memory_space=pltpu.SMEM),
                  pl.BlockSpec((tm, d), lambda i: (i, 0)),
                  pl.BlockSpec((tm, LANES), lambda i: (i, 0)),
                  pl.BlockSpec((1, d), lambda i: (0, 0)),
                  pl.BlockSpec((1, d), lambda i: (0, 0)),
                  pl.BlockSpec(memory_space=pl.ANY)],
        out_specs=pl.BlockSpec((tm, d), lambda i: (i, 0)),
        scratch_shapes=[pltpu.VMEM((TOP_K, tm, d // 2), U32), pltpu.SemaphoreType.DMA(())],
        compiler_params=_params("arbitrary"),
        name="combine_ln",
    )(dest3, h, gates, g, b, y)


def _tiles(n_tokens, d_model, d_expert):
    return dict(
        cast_tm=min(512, n_tokens),
        proj=dict(tm=min(1024, n_tokens), tn=512),
        memkv=dict(tm=512, tn=512),
        merge=dict(tm=min(512, n_tokens), tn=256),
        out=dict(tm=min(1024, n_tokens), tn=512),
        ln_tm=min(256, n_tokens),
        dispatch_tm=min(256, n_tokens),
        moe_tr=512,
        up_tn=min(1024, 2 * d_expert),
        down_tn=min(2048, d_model),
        combine_tm=min(128, n_tokens),
    )


def _layer(h, mem2, lw, *, batch, seq, mem_len, lambda_init, alpha):
    n, d = h.shape
    t = _tiles(n, d, lw["w_mlp2"].shape[1])
    a_width = A_HEADS * A_HEAD_DIM
    b_width = B_HEADS * 2 * B_HEAD_DIM
    col = dict(a_q=0, a_k=a_width, a_v=2 * a_width, b_q=3 * a_width, b_k=3 * a_width + b_width,
               b_v=3 * a_width + 2 * b_width, c_q=3 * a_width + 3 * b_width)

    xb = _cast_bf16(h, t["cast_tm"])
    proj = _matmul(xb, lw["w_in"], out_dtype=BF16, name="in_proj", **t["proj"])
    ckv = _matmul(mem2, lw["w_mem_kv"], out_dtype=BF16, name="mem_kv",
                  tm=min(t["memkv"]["tm"], mem2.shape[0]), tn=t["memkv"]["tn"])

    ya = _attention_a(proj, _band_bias_base(lw["rel_bias"]), batch=batch, seq=seq,
                      heads=A_HEADS, dh=A_HEAD_DIM, col_q=col["a_q"], col_k=col["a_k"], col_v=col["a_v"])
    cos, sin_signed = _rope_tables(seq, B_HEAD_DIM)
    lam_vecs = jnp.stack([lw["lambda_q1"], lw["lambda_k1"], lw["lambda_q2"], lw["lambda_k2"]]).astype(F32)
    yb = _attention_b(proj, cos, sin_signed, lam_vecs, lw["diff_norm_g"].reshape(1, -1),
                      batch=batch, seq=seq, heads=B_HEADS, dh=B_HEAD_DIM,
                      col_q=col["b_q"], col_k=col["b_k"], col_v=col["b_v"], lambda_init=lambda_init)
    yc = _attention_c(proj, ckv, batch=batch, seq=seq, mem_len=mem_len, heads=C_HEADS,
                      dh=C_HEAD_DIM, col_q=col["c_q"])

    merged = _gated_merge(xb, ya, yb, yc, lw["w_gates"], lw["b_gates"].reshape(1, -1),
                          lw["w_branch_a"], lw["w_branch_b"], lw["w_branch_c"], **t["merge"])
    mix = _matmul(merged, lw["w_o"], out_dtype=F32, name="out_proj", **t["out"])

    n_exp = lw["w_router"].shape[1]
    wr = jnp.pad(lw["w_router"], ((0, 0), (0, LANES - n_exp)))
    wr_hi = wr.astype(BF16)
    wr_lo = (wr - wr_hi.astype(F32)).astype(BF16)
    br = jnp.pad(lw["b_router"], (0, LANES - n_exp)).reshape(1, LANES)
    h1, h1_packed, top_idx, gates, rank, cnt = _ln_router(
        h, mix, lw["ln1_g"].reshape(1, -1), lw["ln1_b"].reshape(1, -1),
        jnp.concatenate([wr_hi, wr_lo], axis=1), br, alpha=alpha, n_exp=n_exp, tm=t["ln_tm"])

    counts = cnt[0, :n_exp].astype(I32)
    gstart = _cumsum_small(counts) - counts
    dest = _lookup(gstart, top_idx[:, :TOP_K]) + rank[:, :TOP_K]
    sched = _visit_schedule(counts, n * TOP_K, t["moe_tr"])

    xg = _dispatch(dest, h1_packed, tm=t["dispatch_tm"])
    act = _expert_up(sched, xg, lw["w_mlp1"], lw["b_mlp1"], tr=t["moe_tr"], tn=t["up_tn"])
    y = _expert_down(sched, act, lw["w_mlp2"], lw["b_mlp2"], tr=t["moe_tr"], tn=t["down_tn"])
    return _combine(dest, h1, gates, lw["ln2_g"].reshape(1, -1), lw["ln2_b"].reshape(1, -1), y,
                    alpha=alpha, tm=t["combine_tm"], chunk=t["down_tn"])


def kernel(x, mem, w_in, w_mem_kv, rel_bias, lambda_q1, lambda_k1, lambda_q2, lambda_k2, diff_norm_g,
           w_branch_a, w_branch_b, w_branch_c, w_gates, b_gates, w_o, ln1_g, ln1_b, w_router, b_router,
           w_mlp1, b_mlp1, w_mlp2, b_mlp2, ln2_g, ln2_b):
    batch, seq, d = x.shape
    mem_len = mem.shape[1]
    depth = w_in.shape[0]
    alpha = (2 * depth) ** 0.25
    stacked = dict(w_in=w_in, w_mem_kv=w_mem_kv, rel_bias=rel_bias, lambda_q1=lambda_q1,
                   lambda_k1=lambda_k1, lambda_q2=lambda_q2, lambda_k2=lambda_k2, diff_norm_g=diff_norm_g,
                   w_branch_a=w_branch_a, w_branch_b=w_branch_b, w_branch_c=w_branch_c, w_gates=w_gates,
                   b_gates=b_gates, w_o=w_o, ln1_g=ln1_g, ln1_b=ln1_b, w_router=w_router,
                   b_router=b_router, w_mlp1=w_mlp1, b_mlp1=b_mlp1, w_mlp2=w_mlp2, b_mlp2=b_mlp2,
                   ln2_g=ln2_g, ln2_b=ln2_b)
    h = x.reshape(batch * seq, d)
    mem2 = mem.reshape(batch * mem_len, d)
    for l in range(depth):
        lw = {name: w[l] for name, w in stacked.items()}
        lambda_init = 0.8 - 0.6 * math.exp(-0.3 * l)
        h = _layer(h, mem2, lw, batch=batch, seq=seq, mem_len=mem_len, lambda_init=lambda_init, alpha=alpha)
    return h.reshape(batch, seq, d)
```
